```python
import math
import jax
import jax.numpy as jnp
from jax import lax
import numpy as np

D_MODEL = 2048
BATCH = 8
SEQ = 2048
DEPTH = 2

GRID_W = 64
CTX_LEN = 256
EPS = 1e-6
ROPE_THETA = 10000.0
N_MOD = 9
FFN_RESIDUAL = 0.5
D_FF = 256 * ((8 * D_MODEL // 3 + 255) // 256)
N_BRANCH = 3
BRANCH_WIDTH = D_MODEL // 2
Q_BLOCK = 128

MLA_NOPE = 128
MLA_ROPE = 64
MLA_V = 128
MLA_QK_DIM = MLA_NOPE + MLA_ROPE
MLA_HEADS = BRANCH_WIDTH // MLA_V
MLA_KV_RANK = 512

SSD_HEADDIM = 64
SSD_INNER = BRANCH_WIDTH
SSD_HEADS = SSD_INNER // SSD_HEADDIM
SSD_GROUPS = 4
SSD_STATE = 128
SSD_CONV = 5
SSD_CHUNK = 128
SSD_XBC = SSD_INNER + 2 * SSD_GROUPS * SSD_STATE

NA_HEADDIM = 128
NA_HEADS = BRANCH_WIDTH // NA_HEADDIM
NA_WIN_ROWS = 8
NA_WIN_COLS = 16
NA_QBLOCK_COLS = 16
NA_BAND_COLS = NA_QBLOCK_COLS + NA_WIN_COLS

IN_SIZES = (MLA_HEADS * MLA_QK_DIM, MLA_KV_RANK, MLA_ROPE, SSD_INNER, SSD_XBC, 2 * SSD_HEADS, 3 * NA_HEADS * NA_HEADDIM, N_BRANCH * D_MODEL)
D_IN = sum(IN_SIZES)

kernel_name = 'hybrid_dit_mla_ssd_natten'


def rmsnorm(x, g):
    xf = x.astype(jnp.float32)
    y = xf * lax.rsqrt(jnp.mean(xf * xf, axis=-1, keepdims=True) + EPS)
    return (y * g.astype(jnp.float32)).astype(x.dtype)


def adaln(cond, w, b):
    return (jax.nn.silu(cond) @ w + b).reshape(cond.shape[0], N_MOD, D_MODEL)


def modulate(y, m, base):
    return y * (1.0 + m[:, base + 1][:, None]) + m[:, base][:, None]


def swiglu(u, w_gate, w_up, w_down):
    return (jax.nn.silu(u @ w_gate) * (u @ w_up)) @ w_down


def ffn_sublayer(h, m, base, g, w_gate, w_up, w_down):
    u = modulate(rmsnorm(h, g), m, base)
    return h + FFN_RESIDUAL * m[:, base + 2][:, None] * swiglu(u, w_gate, w_up, w_down)


def axial_rope_tables(n_tokens):
    t = jnp.arange(n_tokens)
    pos = jnp.stack([t // GRID_W, t % GRID_W], axis=-1).astype(jnp.float32)
    n_freq = MLA_ROPE // 4
    inv_freq = ROPE_THETA ** (-jnp.arange(n_freq, dtype=jnp.float32) / n_freq)
    ang = pos[:, :, None] * inv_freq
    return jnp.cos(ang), jnp.sin(ang)


def apply_axial_rope(x, cos, sin):
    shp = x.shape
    xr = x.reshape(shp[:-1] + (2, 2, shp[-1] // 4))
    x1, x2 = xr[..., 0, :], xr[..., 1, :]
    c = cos[:, None].astype(x.dtype)
    s = sin[:, None].astype(x.dtype)
    out = jnp.stack([x1 * c - x2 * s, x2 * c + x1 * s], axis=-2)
    return out.reshape(shp)


def blocked_attention(q, k, v):
    bsz, t, h, dq = q.shape
    scale = dq ** -0.5
    q_blocks = jnp.moveaxis(q.reshape(bsz, t // Q_BLOCK, Q_BLOCK, h, dq), 1, 0)

    def one_block(qb):
        logits = jnp.einsum('bqhd,bkhd->bhqk', qb, k).astype(jnp.float32) * scale
        p = jax.nn.softmax(logits, axis=-1).astype(v.dtype)
        return jnp.einsum('bhqk,bkhd->bqhd', p, v)

    o = lax.map(one_block, q_blocks)
    return jnp.moveaxis(o, 0, 1).reshape(bsz, t, h * v.shape[-1])


def mla_qkv(q_raw, ckv_raw, kr_raw, kv_norm_g, w_uk, w_uv, q_norm_g, k_norm_g, rope):
    bsz, t, _ = q_raw.shape
    q = rmsnorm(q_raw.reshape(bsz, t, MLA_HEADS, MLA_QK_DIM), q_norm_g)
    ckv = rmsnorm(ckv_raw, kv_norm_g)
    k_nope = (ckv @ w_uk).reshape(bsz, t, MLA_HEADS, MLA_NOPE)
    v = (ckv @ w_uv).reshape(bsz, t, MLA_HEADS, MLA_V)
    k_rope = jnp.broadcast_to(kr_raw[:, :, None, :], (bsz, t, MLA_HEADS, MLA_ROPE))
    k = rmsnorm(jnp.concatenate([k_nope, k_rope], axis=-1), k_norm_g)
    if rope is not None:
        cos, sin = rope
        q = jnp.concatenate([q[..., :MLA_NOPE], apply_axial_rope(q[..., MLA_NOPE:], cos, sin)], axis=-1)
        k = jnp.concatenate([k[..., :MLA_NOPE], apply_axial_rope(k[..., MLA_NOPE:], cos, sin)], axis=-1)
    return q, k, v


def mla_mixer(lat, ctx, kv_norm_g, w_uk, w_uv, q_norm_g, k_norm_g, cos, sin, need_ctx):
    q_l, k_l, v_l = mla_qkv(*lat, kv_norm_g, w_uk, w_uv, q_norm_g, k_norm_g, (cos, sin))
    q_c, k_c, v_c = mla_qkv(*ctx, kv_norm_g, w_uk, w_uv, q_norm_g, k_norm_g, None)
    out_l = blocked_attention(q_l, jnp.concatenate([k_c, k_l], axis=1), jnp.concatenate([v_c, v_l], axis=1))
    out_c = blocked_attention(q_c, k_c, v_c) if need_ctx else None
    return out_l, out_c


def centred_depthwise_conv(x, w, b):
    pad = SSD_CONV // 2
    y = lax.conv_general_dilated(x, w[:, None, :], window_strides=(1,), padding=((pad, pad),),
                                 dimension_numbers=('NWC', 'WIO', 'NWC'), feature_group_count=x.shape[-1])
    return y + b


def ssd_chunked(x, dt_a, b_mat, c_mat, init_state):
    bsz, t, h, p = x.shape
    g, n = b_mat.shape[2], b_mat.shape[3]
    r = h // g
    q = SSD_CHUNK
    nc = t // q
    xc = x.astype(jnp.float32).reshape(bsz, nc, q, g, r, p)
    ac = dt_a.astype(jnp.float32).reshape(bsz, nc, q, g, r)
    bc = b_mat.astype(jnp.float32).reshape(bsz, nc, q, g, n)
    cc = c_mat.astype(jnp.float32).reshape(bsz, nc, q, g, n)
    a_cum = jnp.cumsum(ac, axis=2)
    causal = jnp.tril(jnp.ones((q, q), dtype=bool))
    seg = a_cum[:, :, :, None] - a_cum[:, :, None, :]
    decay_in = jnp.exp(jnp.where(causal[None, None, :, :, None, None], seg, -jnp.inf))
    cb = jnp.einsum('bclgn,bcsgn->bclsg', cc, bc)
    y_diag = jnp.einsum('bclsgr,bcsgrp->bclgrp', cb[..., None] * decay_in, xc)
    decay_to_end = jnp.exp(a_cum[:, :, -1:] - a_cum)
    chunk_states = jnp.einsum('bclgn,bclgrp->bcgrpn', bc, xc * decay_to_end[..., None])
    chunk_decay = jnp.exp(a_cum[:, :, -1])

    def step(state, inp):
        dec, st = inp
        return state * dec[..., None, None] + st, state

    final, entering = lax.scan(step, init_state.astype(jnp.float32),
                               (jnp.moveaxis(chunk_decay, 1, 0), jnp.moveaxis(chunk_states, 1, 0)))
    entering = jnp.moveaxis(entering, 0, 1)
    y_off = jnp.einsum('bclgn,bcgrpn->bclgrp', cc, entering) * jnp.exp(a_cum)[..., None]
    return (y_diag + y_off).reshape(bsz, t, h, p), final


def ssd_inputs(xbc, dt_raw, conv_w, conv_b, dt_bias):
    bsz, t, _ = xbc.shape
    xbc = jax.nn.silu(centred_depthwise_conv(xbc, conv_w, conv_b))
    xs, bm, cm = jnp.split(xbc, [SSD_INNER, SSD_INNER + SSD_GROUPS * SSD_STATE], axis=-1)
    xs = xs.reshape(bsz, t, SSD_HEADS, SSD_HEADDIM)
    bm = bm.reshape(bsz, t, SSD_GROUPS, SSD_STATE)
    cm = cm.reshape(bsz, t, SSD_GROUPS, SSD_STATE)
    dt = jax.nn.softplus(dt_raw.astype(jnp.float32).reshape(bsz, t, 2, SSD_HEADS) + dt_bias.astype(jnp.float32))
    return xs, bm, cm, dt


def ssd_direction(xs, bm, cm, dt, a, d, init, reverse):
    xdt = xs.astype(jnp.float32) * dt[:, :, d, :, None]
    dta = dt[:, :, d] * a[d]
    if reverse:
        xdt, dta, bm, cm = (jnp.flip(v, axis=1) for v in (xdt, dta, bm, cm))
    y, final = ssd_chunked(xdt, dta, bm, cm, init)
    if reverse:
        y = jnp.flip(y, axis=1)
    return y, final


def ssd_output(y, xs, z, d_skip, norm_g):
    bsz, t = z.shape[:2]
    y = y + d_skip.astype(jnp.float32)[:, None] * xs.astype(jnp.float32)
    v = y.reshape(bsz, t, SSD_INNER) * jax.nn.silu(z.astype(jnp.float32))
    v = v.reshape(bsz, t, SSD_GROUPS, SSD_INNER // SSD_GROUPS)
    v = v * lax.rsqrt(jnp.mean(v * v, axis=-1, keepdims=True) + EPS)
    return (v.reshape(bsz, t, SSD_INNER) * norm_g.astype(jnp.float32)).astype(z.dtype)


def ssd_mixer(lat, ctx, conv_w, conv_b, a_log, dt_bias, d_skip, norm_g, need_ctx):
    z_l, xbc_l, dtr_l = lat
    z_c, xbc_c, dtr_c = ctx
    xs_l, b_l, c_l, dt_l = ssd_inputs(xbc_l, dtr_l, conv_w, conv_b, dt_bias)
    xs_c, b_c, c_c, dt_c = ssd_inputs(xbc_c, dtr_c, conv_w, conv_b, dt_bias)
    a = -jnp.exp(a_log.astype(jnp.float32))
    zero = jnp.zeros((xs_c.shape[0], SSD_GROUPS, SSD_HEADS // SSD_GROUPS, SSD_HEADDIM, SSD_STATE), jnp.float32)
    y_cf, s_f = ssd_direction(xs_c, b_c, c_c, dt_c, a, 0, zero, False)
    y_cb, s_b = ssd_direction(xs_c, b_c, c_c, dt_c, a, 1, zero, True)
    y_lf, _ = ssd_direction(xs_l, b_l, c_l, dt_l, a, 0, s_f, False)
    y_lb, _ = ssd_direction(xs_l, b_l, c_l, dt_l, a, 1, s_b, True)
    out_l = ssd_output(y_lf + y_lb, xs_l, z_l, d_skip, norm_g)
    out_c = ssd_output(y_cf + y_cb, xs_c, z_c, d_skip, norm_g) if need_ctx else None
    return out_l, out_c


def na_heads(qkv, q_norm_g, k_norm_g):
    bsz, t, _ = qkv.shape
    qkv = qkv.reshape(bsz, t, 3, NA_HEADS, NA_HEADDIM)
    return rmsnorm(qkv[:, :, 0], q_norm_g), rmsnorm(qkv[:, :, 1], k_norm_g), qkv[:, :, 2]


def na_mixer(qkv_l, qkv_c, q_norm_g, k_norm_g, rpb, need_ctx):
    q_l, k_l, v_l = na_heads(qkv_l, q_norm_g, k_norm_g)
    q_c, k_c, v_c = na_heads(qkv_c, q_norm_g, k_norm_g)
    bsz, seq = q_l.shape[:2]
    rows = seq // GRID_W
    kr = min(NA_WIN_ROWS, rows)
    scale = NA_HEADDIM ** -0.5
    r = np.arange(rows)
    row_idx = np.clip(r - kr // 2, 0, rows - kr)[:, None] + np.arange(kr)
    row_off = row_idx - r[:, None] + (NA_WIN_ROWS - 1)
    n_cb = GRID_W // NA_QBLOCK_COLS
    q_cols = np.arange(GRID_W).reshape(n_cb, NA_QBLOCK_COLS)
    band_cols = np.clip(q_cols[:, 0] - NA_WIN_COLS // 2, 0, GRID_W - NA_BAND_COLS)[:, None] + np.arange(NA_BAND_COLS)
    win_start = np.clip(q_cols - NA_WIN_COLS // 2, 0, GRID_W - NA_WIN_COLS)
    kc = band_cols[:, None, :]
    col_mask = (kc >= win_start[:, :, None]) & (kc < win_start[:, :, None] + NA_WIN_COLS)
    col_off = np.clip(kc - q_cols[:, :, None] + NA_WIN_COLS - 1, 0, 2 * NA_WIN_COLS - 2)
    rpb_cols = rpb[:, :, col_off]
    kg = k_l.reshape(bsz, rows, GRID_W, NA_HEADS, NA_HEADDIM)
    vg = v_l.reshape(bsz, rows, GRID_W, NA_HEADS, NA_HEADDIM)
    q_rows = jnp.moveaxis(q_l.reshape(bsz, rows, n_cb, NA_QBLOCK_COLS, NA_HEADS, NA_HEADDIM), 1, 0)
    n_win = kr * NA_BAND_COLS

    def one_row(args):
        qb, r_idx, r_off = args
        k_band = jnp.take(kg, r_idx, axis=1)[:, :, band_cols]
        v_band = jnp.take(vg, r_idx, axis=1)[:, :, band_cols]
        s_win = jnp.einsum('bnqhd,brnkhd->bnhqrk', qb, k_band).astype(jnp.float32) * scale
        bias = jnp.transpose(rpb_cols[:, r_off], (2, 0, 3, 1, 4))
        s_win = jnp.where(col_mask[:, None, :, None, :], s_win + bias, -jnp.inf)
        s_ctx = jnp.einsum('bnqhd,bchd->bnhqc', qb, k_c).astype(jnp.float32) * scale
        logits = jnp.concatenate([s_win.reshape(s_win.shape[:4] + (-1,)), s_ctx], axis=-1)
        p = jax.nn.softmax(logits, axis=-1).astype(v_l.dtype)
        p_win = p[..., :n_win].reshape(s_win.shape)
        return (jnp.einsum('bnhqrk,brnkhd->bnqhd', p_win, v_band)
                + jnp.einsum('bnhqc,bchd->bnqhd', p[..., n_win:], v_c))

    o = lax.map(one_row, (q_rows, jnp.asarray(row_idx), jnp.asarray(row_off)))
    out_l = jnp.moveaxis(o, 0, 1).reshape(bsz, seq, NA_HEADS * NA_HEADDIM)
    out_c = blocked_attention(q_c, k_c, v_c) if need_ctx else None
    return out_l, out_c


def merge_branches(branches, gate_cols, w_branch, w_out):
    bsz, t, _ = gate_cols.shape
    gates = jax.nn.sigmoid(gate_cols.reshape(bsz, t, N_BRANCH, D_MODEL))
    merged = gates[:, :, 0] * (branches[0] @ w_branch[0])
    for i in range(1, N_BRANCH):
        merged = merged + gates[:, :, i] * (branches[i] @ w_branch[i])
    return merged @ w_out


def token_mixers(u, uc, w_in, mla_kv_norm_g, mla_w_uk, mla_w_uv, mla_q_norm_g, mla_k_norm_g,
                 ssd_conv_w, ssd_conv_b, ssd_a_log, ssd_dt_bias, ssd_d, ssd_norm_g,
                 na_q_norm_g, na_k_norm_g, na_rpb, w_branch, w_out, cos, sin, need_ctx):
    split_at = np.cumsum(IN_SIZES)[:-1].tolist()
    pl = jnp.split(u @ w_in, split_at, axis=-1)
    pc = jnp.split(uc @ w_in, split_at, axis=-1)
    mla_l, mla_c = mla_mixer(pl[0:3], pc[0:3], mla_kv_norm_g, mla_w_uk, mla_w_uv, mla_q_norm_g, mla_k_norm_g, cos, sin, need_ctx)
    ssd_l, ssd_c = ssd_mixer(pl[3:6], pc[3:6], ssd_conv_w, ssd_conv_b, ssd_a_log, ssd_dt_bias, ssd_d, ssd_norm_g, need_ctx)
    na_l, na_c = na_mixer(pl[6], pc[6], na_q_norm_g, na_k_norm_g, na_rpb, need_ctx)
    out_l = merge_branches((mla_l, ssd_l, na_l), pl[7], w_branch, w_out)
    out_c = merge_branches((mla_c, ssd_c, na_c), pc[7], w_branch, w_out) if need_ctx else None
    return out_l, out_c


def setup_inputs(seed: int = 0) -> dict:
    key = jax.random.key(seed)
    keys = iter(jax.random.split(key, 40))

    def normal(shape, scale):
        return jax.random.normal(next(keys), shape, jnp.float32) * scale

    def gain(shape):
        return 1.0 + normal(shape, 0.02)

    L = DEPTH
    dt0 = jnp.exp(jax.random.uniform(next(keys), (L, 2, SSD_HEADS), jnp.float32, math.log(1e-3), math.log(1e-1)))
    a_init = jax.random.uniform(next(keys), (L, 2, SSD_HEADS), jnp.float32, 1.0, 16.0)
    return {
        'x': normal((BATCH, SEQ, D_MODEL), 1.0),
        'c': normal((BATCH, D_MODEL), 1.0),
        'ctx': normal((BATCH, CTX_LEN, D_MODEL), 1.0),
        'c_ctx': normal((D_MODEL,), 1.0),
        'w_ada': normal((L, D_MODEL, N_MOD * D_MODEL), 0.5 * D_MODEL ** -0.5),
        'b_ada': normal((L, N_MOD * D_MODEL), 0.02),
        'norm_g': gain((L, 3, D_MODEL)),
        'ffn1_w_gate': normal((L, D_MODEL, D_FF), D_MODEL ** -0.5),
        'ffn1_w_up': normal((L, D_MODEL, D_FF), D_MODEL ** -0.5),
        'ffn1_w_down': normal((L, D_FF, D_MODEL), D_FF ** -0.5),
        'ffn2_w_gate': normal((L, D_MODEL, D_FF), D_MODEL ** -0.5),
        'ffn2_w_up': normal((L, D_MODEL, D_FF), D_MODEL ** -0.5),
        'ffn2_w_down': normal((L, D_FF, D_MODEL), D_FF ** -0.5),
        'w_in': normal((L, D_MODEL, D_IN), D_MODEL ** -0.5),
        'mla_kv_norm_g': gain((L, MLA_KV_RANK)),
        'mla_w_uk': normal((L, MLA_KV_RANK, MLA_HEADS * MLA_NOPE), MLA_KV_RANK ** -0.5),
        'mla_w_uv': normal((L, MLA_KV_RANK, MLA_HEADS * MLA_V), MLA_KV_RANK ** -0.5),
        'mla_q_norm_g': gain((L, MLA_QK_DIM)),
        'mla_k_norm_g': gain((L, MLA_QK_DIM)),
        'ssd_conv_w': normal((L, SSD_CONV, SSD_XBC), SSD_CONV ** -0.5),
        'ssd_conv_b': normal((L, SSD_XBC), 0.02),
        'ssd_a_log': jnp.log(a_init),
        'ssd_dt_bias': dt0 + jnp.log(-jnp.expm1(-dt0)),
        'ssd_d': 1.0 + normal((L, SSD_HEADS), 0.1),
        'ssd_norm_g': gain((L, SSD_INNER)),
        'na_q_norm_g': gain((L, NA_HEADDIM)),
        'na_k_norm_g': gain((L, NA_HEADDIM)),
        'na_rpb': normal((L, NA_HEADS, 2 * NA_WIN_ROWS - 1, 2 * NA_WIN_COLS - 1), 0.1),
        'w_branch': normal((L, N_BRANCH, BRANCH_WIDTH, D_MODEL), BRANCH_WIDTH ** -0.5),
        'w_out': normal((L, D_MODEL, D_MODEL), D_MODEL ** -0.5),
    }


def reference(x, c, ctx, c_ctx, w_ada, b_ada, norm_g, ffn1_w_gate, ffn1_w_up, ffn1_w_down,
              ffn2_w_gate, ffn2_w_up, ffn2_w_down, w_in, mla_kv_norm_g, mla_w_uk, mla_w_uv,
              mla_q_norm_g, mla_k_norm_g, ssd_conv_w, ssd_conv_b, ssd_a_log, ssd_dt_bias, ssd_d,
              ssd_norm_g, na_q_norm_g, na_k_norm_g, na_rpb, w_branch, w_out):
    cos, sin = axial_rope_tables(x.shape[1])
    h, hc = x, ctx
    for l in range(DEPTH):
        need_ctx = l < DEPTH - 1
        m = adaln(c, w_ada[l], b_ada[l])
        mc = adaln(c_ctx[None], w_ada[l], b_ada[l])
        h = ffn_sublayer(h, m, 0, norm_g[l, 0], ffn1_w_gate[l], ffn1_w_up[l], ffn1_w_down[l])
        hc = ffn_sublayer(hc, mc, 0, norm_g[l, 0], ffn1_w_gate[l], ffn1_w_up[l], ffn1_w_down[l])
        u = modulate(rmsnorm(h, norm_g[l, 1]), m, 3)
        uc = modulate(rmsnorm(hc, norm_g[l, 1]), mc, 3)
        mix, mix_c = token_mixers(u, uc, w_in[l], mla_kv_norm_g[l], mla_w_uk[l], mla_w_uv[l],
                                  mla_q_norm_g[l], mla_k_norm_g[l], ssd_conv_w[l], ssd_conv_b[l],
                                  ssd_a_log[l], ssd_dt_bias[l], ssd_d[l], ssd_norm_g[l],
                                  na_q_norm_g[l], na_k_norm_g[l], na_rpb[l], w_branch[l], w_out[l],
                                  cos, sin, need_ctx)
        h = h + m[:, 5][:, None] * mix
        h = ffn_sublayer(h, m, 6, norm_g[l, 2], ffn2_w_gate[l], ffn2_w_up[l], ffn2_w_down[l])
        if need_ctx:
            hc = hc + mc[:, 5][:, None] * mix_c
            hc = ffn_sublayer(hc, mc, 6, norm_g[l, 2], ffn2_w_gate[l], ffn2_w_up[l], ffn2_w_down[l])
    return h
```

```python
import functools
import math

import numpy as np
import jax
import jax.numpy as jnp
from jax import lax
from jax.experimental import pallas as pl
from jax.experimental.pallas import tpu as pltpu

F32 = jnp.float32
BF16 = jnp.bfloat16

D = 2048
SEQ = 2048
CTX = 256
DEPTH = 2
GRID_W = 64
GRID_H = SEQ // GRID_W
EPS = 1e-6
ROPE_THETA = 10000.0
N_MOD = 9
D_FF = 5632
HALF = D // 2

MLA_NOPE, MLA_ROPE, MLA_V = 128, 64, 128
MLA_QK = MLA_NOPE + MLA_ROPE
MLA_HEADS = 8
MLA_RANK = 512
MLA_PAD = 256

SSD_HEADS, SSD_P, SSD_G, SSD_N, SSD_CONV, SSD_CHUNK = 16, 64, 4, 128, 5, 128
SSD_XBC = HALF + 2 * SSD_G * SSD_N

NA_HEADS, NA_DH = 8, 128
NA_WIN_ROWS, NA_WIN_COLS = 8, 16
NA_QROWS, NA_QCOLS = 8, 16
NA_KROWS, NA_KCOLS = NA_QROWS + NA_WIN_ROWS - 1, NA_QCOLS + NA_WIN_COLS
NA_NRG, NA_NCB = GRID_H // NA_QROWS, GRID_W // NA_QCOLS
NEG_BIG = -1e30

OFF_GATE = 0
OFF_NA = 3 * D
OFF_XBC = OFF_NA + 3 * HALF
OFF_QNOPE = OFF_XBC + SSD_XBC
OFF_Z = OFF_QNOPE + HALF
OFF_QROPE = OFF_Z + HALF
OFF_CKV = OFF_QROPE + MLA_HEADS * MLA_ROPE
OFF_MISC = OFF_CKV + MLA_RANK
N_PROJ = 29 * 512
MISC_DT = MLA_ROPE

VMEM_LIMIT = 56 * 1024 * 1024


def _cparams(sem):
    return pltpu.CompilerParams(dimension_semantics=sem, vmem_limit_bytes=VMEM_LIMIT)


def _row_tile(rows, preferred):
    tm = preferred
    while rows % tm:
        tm //= 2
    return tm


def _dot(a, b):
    return jnp.dot(a, b, preferred_element_type=F32)


def _dot_nt(a, b):
    return lax.dot_general(a, b, (((1,), (1,)), ((), ())), preferred_element_type=F32)


def _split3(x):
    hi = x.astype(BF16)
    r1 = x - hi.astype(F32)
    mid = r1.astype(BF16)
    lo = (r1 - mid.astype(F32)).astype(BF16)
    return hi, mid, lo


def _dot_sel_right(x, m01):
    hi, mid, lo = _split3(x)
    return _dot(hi, m01) + _dot(mid, m01) + _dot(lo, m01)


def _dot_sel_left(m01, x):
    hi, mid, lo = _split3(x)
    return _dot(m01, hi) + _dot(m01, mid) + _dot(m01, lo)


def _silu(x):
    return x * jax.nn.sigmoid(x)


def _softplus(x):
    return jnp.maximum(x, 0.0) + jnp.log1p(jnp.exp(-jnp.abs(x)))


def _rms(x, g):
    return x * lax.rsqrt(jnp.mean(x * x, axis=-1, keepdims=True) + EPS) * g


def _ada_kernel(c_ref, w_ref, b_ref, o_ref):
    s = _silu(c_ref[...]).astype(BF16)
    o_ref[...] = _dot(s, w_ref[...].astype(BF16)) + b_ref[...]


def _adaln(cond, w_ada, b_ada):
    depth = w_ada.shape[0]
    n = N_MOD * D
    tn = 1024
    return pl.pallas_call(
        _ada_kernel,
        grid=(depth, n // tn),
        in_specs=[
            pl.BlockSpec((16, D), lambda l, j: (0, 0)),
            pl.BlockSpec((None, D, tn), lambda l, j: (l, 0, j)),
            pl.BlockSpec((None, 1, tn), lambda l, j: (l, 0, j)),
        ],
        out_specs=pl.BlockSpec((None, 16, tn), lambda l, j: (l, 0, j)),
        out_shape=jax.ShapeDtypeStruct((depth, 16, n), F32),
        compiler_params=_cparams(("arbitrary", "arbitrary")),
        name="adaln",
    )(cond, w_ada, b_ada.reshape(depth, 1, n))


def _mod_spec(tm, n_lat_tiles, n_batch, k):
    per = SEQ // tm

    def imap(i, *_):
        return (jnp.where(i < n_lat_tiles, i // per, n_batch), k, 0, 0)

    return pl.BlockSpec((None, None, 1, D), imap)


def _normmod_kernel(h_ref, g_ref, shift_ref, scale_ref, o_ref):
    y = _rms(h_ref[...], g_ref[...])
    o_ref[...] = (y * (1.0 + scale_ref[...]) + shift_ref[...]).astype(o_ref.dtype)


def _normmod(h, rows, g, m4, base, n_lat, n_batch):
    tm = 256
    nlt = n_lat // tm
    return pl.pallas_call(
        _normmod_kernel,
        grid=(rows // tm,),
        in_specs=[
            pl.BlockSpec((tm, D), lambda i: (i, 0)),
            pl.BlockSpec((1, D), lambda i: (0, 0)),
            _mod_spec(tm, nlt, n_batch, base),
            _mod_spec(tm, nlt, n_batch, base + 1),
        ],
        out_specs=pl.BlockSpec((tm, D), lambda i: (i, 0)),
        out_shape=jax.ShapeDtypeStruct((rows, D), BF16),
        compiler_params=_cparams(("arbitrary",)),
        name="normmod",
    )(h, g.reshape(1, D), m4, m4)


def _ffn_kernel(u_ref, wg_ref, wu_ref, wd_ref, h_ref, gate_ref, o_ref, acc_ref, *, nf):
    f = pl.program_id(1)

    @pl.when(f == 0)
    def _():
        acc_ref[...] = jnp.zeros_like(acc_ref)

    u = u_ref[...]
    g = _dot(u, wg_ref[...])
    up = _dot(u, wu_ref[...])
    act = (_silu(g) * up).astype(BF16)
    acc_ref[...] += _dot(act, wd_ref[...])

    @pl.when(f == nf - 1)
    def _():
        o_ref[...] = h_ref[...] + (0.5 * gate_ref[...]) * acc_ref[...]


def _ffn(u, h, rows, wg, wu, wd, m4, base, n_lat, n_batch):
    tm, tf = _row_tile(rows, 512), 512
    nf = D_FF // tf
    return pl.pallas_call(
        functools.partial(_ffn_kernel, nf=nf),
        grid=(rows // tm, nf),
        in_specs=[
            pl.BlockSpec((tm, D), lambda i, f: (i, 0)),
            pl.BlockSpec((D, tf), lambda i, f: (0, f)),
            pl.BlockSpec((D, tf), lambda i, f: (0, f)),
            pl.BlockSpec((tf, D), lambda i, f: (f, 0)),
            pl.BlockSpec((tm, D), lambda i, f: (i, 0)),
            _mod_spec(tm, n_lat // tm, n_batch, base + 2),
        ],
        out_specs=pl.BlockSpec((tm, D), lambda i, f: (i, 0)),
        out_shape=jax.ShapeDtypeStruct((rows, D), F32),
        scratch_shapes=[pltpu.VMEM((tm, D), F32)],
        compiler_params=_cparams(("arbitrary", "arbitrary")),
        name="ffn",
    )(u, wg, wu, wd, h, m4)


def _mm_kernel(a_ref, w_ref, o_ref):
    o_ref[...] = _dot(a_ref[...], w_ref[...]).astype(o_ref.dtype)


def _in_proj(u, w):
    rows = u.shape[0]
    tm, tn = _row_tile(rows, 1024), 512
    return pl.pallas_call(
        _mm_kernel,
        grid=(rows // tm, N_PROJ // tn),
        in_specs=[pl.BlockSpec((tm, D), lambda i, j: (i, 0)), pl.BlockSpec((D, tn), lambda i, j: (0, j))],
        out_specs=pl.BlockSpec((tm, tn), lambda i, j: (i, j)),
        out_shape=jax.ShapeDtypeStruct((rows, N_PROJ), F32),
        compiler_params=_cparams(("arbitrary", "arbitrary")),
        name="in_proj",
    )(u, w)


def _mm_res_kernel(a_ref, w_ref, h_ref, gate_ref, o_ref):
    o_ref[...] = h_ref[...] + gate_ref[...] * _dot(a_ref[...], w_ref[...])


def _out_proj(merged, w_out, h, rows, m4, n_lat, n_batch):
    tm, tn = _row_tile(rows, 512), 512

    def gate_map(i, j):
        return (jnp.where(i < n_lat // tm, i // (SEQ // tm), n_batch), 5, 0, j)

    return pl.pallas_call(
        _mm_res_kernel,
        grid=(rows // tm, D // tn),
        in_specs=[
            pl.BlockSpec((tm, D), lambda i, j: (i, 0)),
            pl.BlockSpec((D, tn), lambda i, j: (0, j)),
            pl.BlockSpec((tm, tn), lambda i, j: (i, j)),
            pl.BlockSpec((None, None, 1, tn), gate_map),
        ],
        out_specs=pl.BlockSpec((tm, tn), lambda i, j: (i, j)),
        out_shape=jax.ShapeDtypeStruct((rows, D), F32),
        compiler_params=_cparams(("arbitrary", "arbitrary")),
        name="out_proj",
    )(merged, w_out, h, m4)


def _merge_kernel(b0_ref, b1_ref, b2_ref, wb_ref, g0_ref, g1_ref, g2_ref, o_ref):
    acc = jax.nn.sigmoid(g0_ref[...]) * _dot(b0_ref[...], wb_ref[0])
    acc = acc + jax.nn.sigmoid(g1_ref[...]) * _dot(b1_ref[...], wb_ref[1])
    acc = acc + jax.nn.sigmoid(g2_ref[...]) * _dot(b2_ref[...], wb_ref[2])
    o_ref[...] = acc.astype(o_ref.dtype)


def _merge(branches, proj, w_branch, rows):
    tm, tn = _row_tile(rows, 512), 512
    nb = D // tn
    gate_specs = [pl.BlockSpec((tm, tn), functools.partial(lambda i, j, k: (i, k * nb + j), k=k)) for k in range(3)]
    return pl.pallas_call(
        _merge_kernel,
        grid=(rows // tm, nb),
        in_specs=[pl.BlockSpec((tm, HALF), lambda i, j: (i, 0))] * 3
        + [pl.BlockSpec((3, HALF, tn), lambda i, j: (0, 0, j))]
        + gate_specs,
        out_specs=pl.BlockSpec((tm, tn), lambda i, j: (i, j)),
        out_shape=jax.ShapeDtypeStruct((rows, D), BF16),
        compiler_params=_cparams(("arbitrary", "arbitrary")),
        name="merge",
    )(*branches, w_branch, proj, proj, proj)


def _lane_sum(x, lo, hi):
    lane = lax.broadcasted_iota(jnp.int32, x.shape, 1)
    return jnp.sum(jnp.where((lane >= lo) & (lane < hi), x, 0.0), axis=-1, keepdims=True)


def _rope128(y, cos, sin):
    lane = lax.broadcasted_iota(jnp.int32, y.shape, 1)
    partner = jnp.where((lane % 32) < 16, pltpu.roll(y, 112, 1), pltpu.roll(y, 16, 1))
    return y * cos + partner * sin


def _mla_prep_kernel(qn_ref, qr_ref, ckv_ref, misc_ref, cos_ref, sin_ref, wuk_ref, wuv_ref,
                     gkv_ref, gqn_ref, gqr_ref, gkn_ref, gkr_ref, q_ref, k_ref, v_ref, *, scale):
    cos, sin = cos_ref[...], sin_ref[...]
    lane = lax.broadcasted_iota(jnp.int32, cos.shape, 1)
    low = lane < MLA_ROPE

    ckv = _rms(ckv_ref[...], gkv_ref[...]).astype(BF16)
    k_nope = _dot(ckv, wuk_ref[...])
    v_ref[...] = _dot(ckv, wuv_ref[...]).astype(v_ref.dtype)

    misc = misc_ref[...]
    kr_ss = _lane_sum(misc * misc, 0, MLA_ROPE)
    kr_rot = jnp.where(low, _rope128(misc * gkr_ref[...], cos, sin), 0.0)
    kr_rot_hi = pltpu.roll(kr_rot, MLA_ROPE, 1)

    gqn, gqr, gkn = gqn_ref[...], gqr_ref[...], gkn_ref[...]
    for pair in range(MLA_HEADS // 2):
        qr = qr_ref[:, pair * 128:(pair + 1) * 128]
        qr2 = qr * qr
        ss_lo = _lane_sum(qr2, 0, MLA_ROPE)
        ss_hi = _lane_sum(qr2, MLA_ROPE, 128)
        rs = []
        for sub, ss_r in ((0, ss_lo), (1, ss_hi)):
            hd = 2 * pair + sub
            qn = qn_ref[:, hd * 128:(hd + 1) * 128]
            r = lax.rsqrt((jnp.sum(qn * qn, axis=-1, keepdims=True) + ss_r) * (1.0 / MLA_QK) + EPS)
            rs.append(r)
            q_ref[:, hd * MLA_PAD:hd * MLA_PAD + 128] = (qn * r * gqn * scale).astype(q_ref.dtype)
            kn = k_nope[:, hd * 128:(hd + 1) * 128]
            rk = lax.rsqrt((jnp.sum(kn * kn, axis=-1, keepdims=True) + kr_ss) * (1.0 / MLA_QK) + EPS)
            k_ref[:, hd * MLA_PAD:hd * MLA_PAD + 128] = (kn * rk * gkn).astype(k_ref.dtype)
            k_ref[:, hd * MLA_PAD + 128:(hd + 1) * MLA_PAD] = ((kr_rot if sub == 0 else kr_rot_hi) * rk).astype(k_ref.dtype)
        q_rot = _rope128(qr * jnp.where(low, rs[0], rs[1]) * gqr, cos, sin) * scale
        q_ref[:, (2 * pair) * MLA_PAD + 128:(2 * pair + 1) * MLA_PAD] = jnp.where(low, q_rot, 0.0).astype(q_ref.dtype)
        q_ref[:, (2 * pair + 1) * MLA_PAD + 128:(2 * pair + 2) * MLA_PAD] = jnp.where(low, 0.0, q_rot).astype(q_ref.dtype)


def _mla_prep(proj, cos_t, sin_t, w_uk, w_uv, g_kv, g_q, g_k, n_lat):
    rows = proj.shape[0]
    tm = 256
    nlt = n_lat // tm
    per = SEQ // tm

    def tab_map(i):
        return (jnp.where(i < nlt, i % per, per), 0)

    def row(v):
        return v.reshape(1, -1)

    def rope_gain(g):
        return jnp.concatenate([g[MLA_NOPE:], g[MLA_NOPE:]]).reshape(1, 128)

    const = lambda i: (0, 0)
    return pl.pallas_call(
        functools.partial(_mla_prep_kernel, scale=MLA_QK ** -0.5),
        grid=(rows // tm,),
        in_specs=[
            pl.BlockSpec((tm, HALF), lambda i: (i, OFF_QNOPE // HALF)),
            pl.BlockSpec((tm, 512), lambda i: (i, OFF_QROPE // 512)),
            pl.BlockSpec((tm, MLA_RANK), lambda i: (i, OFF_CKV // MLA_RANK)),
            pl.BlockSpec((tm, 128), lambda i: (i, OFF_MISC // 128)),
            pl.BlockSpec((tm, 128), tab_map),
            pl.BlockSpec((tm, 128), tab_map),
            pl.BlockSpec((MLA_RANK, HALF), const),
            pl.BlockSpec((MLA_RANK, HALF), const),
            pl.BlockSpec((1, MLA_RANK), const),
            pl.BlockSpec((1, 128), const),
            pl.BlockSpec((1, 128), const),
            pl.BlockSpec((1, 128), const),
            pl.BlockSpec((1, 128), const),
        ],
        out_specs=[
            pl.BlockSpec((tm, MLA_HEADS * MLA_PAD), lambda i: (i, 0)),
            pl.BlockSpec((tm, MLA_HEADS * MLA_PAD), lambda i: (i, 0)),
            pl.BlockSpec((tm, HALF), lambda i: (i, 0)),
        ],
        out_shape=[
            jax.ShapeDtypeStruct((rows, MLA_HEADS * MLA_PAD), BF16),
            jax.ShapeDtypeStruct((rows, MLA_HEADS * MLA_PAD), BF16),
            jax.ShapeDtypeStruct((rows, HALF), BF16),
        ],
        compiler_params=_cparams(("arbitrary",)),
        name="mla_prep",
    )(proj, proj, proj, proj, cos_t, sin_t, w_uk, w_uv, row(g_kv),
      row(g_q[:MLA_NOPE]), rope_gain(g_q), row(g_k[:MLA_NOPE]), rope_gain(g_k))


def _softmax2_pv(s1, s2, v1, v2):
    m = jnp.maximum(jnp.max(s1, axis=-1, keepdims=True), jnp.max(s2, axis=-1, keepdims=True))
    p1 = jnp.exp(s1 - m)
    p2 = jnp.exp(s2 - m)
    denom = jnp.sum(p1, axis=-1, keepdims=True) + jnp.sum(p2, axis=-1, keepdims=True)
    return (_dot(p1.astype(BF16), v1) + _dot(p2.astype(BF16), v2)) / denom


def _softmax_pv(s, v):
    p = jnp.exp(s - jnp.max(s, axis=-1, keepdims=True))
    return _dot(p.astype(BF16), v) / jnp.sum(p, axis=-1, keepdims=True)


def _mla_attn_kernel(q_ref, kl_ref, kc_ref, vl_ref, vc_ref, o_ref):
    q = q_ref[...]
    o = _softmax2_pv(_dot_nt(q, kl_ref[...]), _dot_nt(q, kc_ref[...]), vl_ref[...], vc_ref[...])
    o_ref[...] = o.astype(o_ref.dtype)


def _mla_attn(q, k, v, n_batch):
    tq = 512
    nq = SEQ // tq
    lat_blocks = n_batch * SEQ // CTX
    return pl.pallas_call(
        _mla_attn_kernel,
        grid=(n_batch, MLA_HEADS, nq),
        in_specs=[
            pl.BlockSpec((tq, MLA_PAD), lambda b, h, i: (b * nq + i, h)),
            pl.BlockSpec((SEQ, MLA_PAD), lambda b, h, i: (b, h)),
            pl.BlockSpec((CTX, MLA_PAD), lambda b, h, i: (lat_blocks + b, h)),
            pl.BlockSpec((SEQ, MLA_V), lambda b, h, i: (b, h)),
            pl.BlockSpec((CTX, MLA_V), lambda b, h, i: (lat_blocks + b, h)),
        ],
        out_specs=pl.BlockSpec((tq, MLA_V), lambda b, h, i: (b * nq + i, h)),
        out_shape=jax.ShapeDtypeStruct((n_batch * SEQ, HALF), BF16),
        compiler_params=_cparams(("arbitrary", "arbitrary", "arbitrary")),
        name="mla_attn",
    )(q, k, k, v, v)


def _ctx_attn_kernel(q_ref, k_ref, v_ref, o_ref):
    o_ref[...] = _softmax_pv(_dot_nt(q_ref[...], k_ref[...]), v_ref[...]).astype(o_ref.dtype)


def _mla_ctx_attn(q, k, v, n_batch):
    lat_blocks = n_batch * SEQ // CTX
    return pl.pallas_call(
        _ctx_attn_kernel,
        grid=(n_batch, MLA_HEADS),
        in_specs=[
            pl.BlockSpec((CTX, MLA_PAD), lambda b, h: (lat_blocks + b, h)),
            pl.BlockSpec((CTX, MLA_PAD), lambda b, h: (lat_blocks + b, h)),
            pl.BlockSpec((CTX, MLA_V), lambda b, h: (lat_blocks + b, h)),
        ],
        out_specs=pl.BlockSpec((CTX, MLA_V), lambda b, h: (b, h)),
        out_shape=jax.ShapeDtypeStruct((n_batch * CTX, HALF), BF16),
        compiler_params=_cparams(("arbitrary", "arbitrary")),
        name="mla_ctx_attn",
    )(q, k, v)


def _na_key_row0(rg):
    return int(np.clip(NA_QROWS * rg - NA_WIN_ROWS // 2, 0, GRID_H - NA_KROWS))


def _na_band0(n):
    return int(np.clip(NA_QCOLS * n - NA_WIN_COLS // 2, 0, GRID_W - NA_KCOLS))


def _na_bias_table(rpb):
    rg, n = np.arange(NA_NRG), np.arange(NA_NCB)
    r = NA_QROWS * rg[:, None] + np.arange(NA_QROWS)[None]
    key_r = np.array([_na_key_row0(g) for g in rg])[:, None] + np.arange(NA_KROWS)[None]
    row_start = np.clip(r - NA_WIN_ROWS // 2, 0, GRID_H - NA_WIN_ROWS)
    r_valid = (key_r[:, None, :] >= row_start[:, :, None]) & (key_r[:, None, :] < row_start[:, :, None] + NA_WIN_ROWS)
    r_off = np.clip(key_r[:, None, :] - r[:, :, None] + NA_WIN_ROWS - 1, 0, 2 * NA_WIN_ROWS - 2)
    c = NA_QCOLS * n[:, None] + np.arange(NA_QCOLS)[None]
    key_c = np.array([_na_band0(b) for b in n])[:, None] + np.arange(NA_KCOLS)[None]
    win_start = np.clip(c - NA_WIN_COLS // 2, 0, GRID_W - NA_WIN_COLS)
    c_valid = (key_c[:, None, :] >= win_start[:, :, None]) & (key_c[:, None, :] < win_start[:, :, None] + NA_WIN_COLS)
    c_off = np.clip(key_c[:, None, :] - c[:, :, None] + NA_WIN_COLS - 1, 0, 2 * NA_WIN_COLS - 2)
    ro = r_off[:, None, :, None, :, None]
    co = c_off[None, :, None, :, None, :]
    valid = r_valid[:, None, :, None, :, None] & c_valid[None, :, None, :, None, :]
    bias = rpb[:, ro, co]
    bias = jnp.where(valid[None], bias, NEG_BIG)
    return bias.reshape(NA_HEADS, NA_NRG * NA_NCB, NA_QROWS * NA_QCOLS, NA_KROWS * NA_KCOLS)


def _na_kernel(q_ref, k_ref, v_ref, qc_ref, kc_ref, vc_ref, gq_ref, gk_ref, bias_ref, o_ref, oc_ref,
               qn_s, kn_s, *, scale):
    gq, gk = gq_ref[...], gk_ref[...]
    qn_s[...] = _rms(q_ref[...], gq) * scale
    kn_s[...] = _rms(k_ref[...], gk)
    kc = _rms(kc_ref[...], gk).astype(BF16)
    vc = vc_ref[...].astype(BF16)
    for rg in range(NA_NRG):
        kr0 = _na_key_row0(rg)
        for n in range(NA_NCB):
            b0 = _na_band0(n)
            q_rows = [(NA_QROWS * rg + i) * GRID_W + NA_QCOLS * n for i in range(NA_QROWS)]
            k_rows = [(kr0 + kr) * GRID_W + b0 for kr in range(NA_KROWS)]
            qb = jnp.concatenate([qn_s[pl.ds(t, NA_QCOLS), :] for t in q_rows], axis=0).astype(BF16)
            kb = jnp.concatenate([kn_s[pl.ds(t, NA_KCOLS), :] for t in k_rows], axis=0).astype(BF16)
            vb = jnp.concatenate([v_ref[pl.ds(t, NA_KCOLS), :] for t in k_rows], axis=0).astype(BF16)
            s_win = _dot_nt(qb, kb) + bias_ref[rg * NA_NCB + n]
            o = _softmax2_pv(s_win, _dot_nt(qb, kc), vb, vc).astype(o_ref.dtype)
            for i, t in enumerate(q_rows):
                o_ref[pl.ds(t, NA_QCOLS), :] = o[i * NA_QCOLS:(i + 1) * NA_QCOLS]
    qc = (_rms(qc_ref[...], gq) * scale).astype(BF16)
    oc_ref[...] = _softmax_pv(_dot_nt(qc, kc), vc).astype(oc_ref.dtype)


def _na_attn(proj, g_q, g_k, bias, n_batch):
    base = OFF_NA // NA_DH
    lat_blocks = n_batch * SEQ // CTX

    def lat(part):
        return pl.BlockSpec((SEQ, NA_DH), lambda h, b: (b, base + part * NA_HEADS + h))

    def cx(part):
        return pl.BlockSpec((CTX, NA_DH), lambda h, b: (lat_blocks + b, base + part * NA_HEADS + h))

    const = lambda h, b: (0, 0)
    nblk, nq, nk = bias.shape[1:]
    return pl.pallas_call(
        functools.partial(_na_kernel, scale=NA_DH ** -0.5),
        grid=(NA_HEADS, n_batch),
        in_specs=[lat(0), lat(1), lat(2), cx(0), cx(1), cx(2),
                  pl.BlockSpec((1, NA_DH), const), pl.BlockSpec((1, NA_DH), const),
                  pl.BlockSpec((None, nblk, nq, nk), lambda h, b: (h, 0, 0, 0))],
        out_specs=[pl.BlockSpec((SEQ, NA_DH), lambda h, b: (b, h)),
                   pl.BlockSpec((CTX, NA_DH), lambda h, b: (b, h))],
        out_shape=[jax.ShapeDtypeStruct((n_batch * SEQ, HALF), BF16),
                   jax.ShapeDtypeStruct((n_batch * CTX, HALF), BF16)],
        scratch_shapes=[pltpu.VMEM((SEQ, NA_DH), F32), pltpu.VMEM((SEQ, NA_DH), F32)],
        compiler_params=_cparams(("arbitrary", "arbitrary")),
        name="na_attn",
    )(proj, proj, proj, proj, proj, proj, g_q.reshape(1, NA_DH), g_k.reshape(1, NA_DH), bias)


CONV_TM = 256
CONV_HALO = 8


def _conv_kernel(prev_ref, cur_ref, next_ref, w_ref, b_ref, o_ref, pad_s, *, n_lat_tiles):
    i = pl.program_id(0)
    per = SEQ // CONV_TM
    is_ctx = i >= n_lat_tiles
    at_start = is_ctx | (i % per == 0)
    at_end = is_ctx | (i % per == per - 1)
    pad_s[0:CONV_HALO, :] = jnp.where(at_start, 0.0, prev_ref[...])
    pad_s[CONV_HALO:CONV_HALO + CONV_TM, :] = cur_ref[...]
    pad_s[CONV_HALO + CONV_TM:, :] = jnp.where(at_end, 0.0, next_ref[...])
    acc = jnp.broadcast_to(b_ref[...], cur_ref.shape)
    for k in range(SSD_CONV):
        acc = acc + w_ref[k:k + 1, :] * pad_s[pl.ds(CONV_HALO - SSD_CONV // 2 + k, CONV_TM), :]
    o_ref[...] = _silu(acc)


def _ssd_conv(proj, conv_w, conv_b, n_lat):
    rows = proj.shape[0]
    tc = 512
    nlt = n_lat // CONV_TM
    cb0 = OFF_XBC // tc
    hb = CONV_TM // CONV_HALO
    last = rows // CONV_HALO - 1
    return pl.pallas_call(
        functools.partial(_conv_kernel, n_lat_tiles=nlt),
        grid=(rows // CONV_TM, SSD_XBC // tc),
        in_specs=[
            pl.BlockSpec((CONV_HALO, tc), lambda i, j: (jnp.maximum(i * hb - 1, 0), cb0 + j)),
            pl.BlockSpec((CONV_TM, tc), lambda i, j: (i, cb0 + j)),
            pl.BlockSpec((CONV_HALO, tc), lambda i, j: (jnp.minimum((i + 1) * hb, last), cb0 + j)),
            pl.BlockSpec((SSD_CONV, tc), lambda i, j: (0, j)),
            pl.BlockSpec((1, tc), lambda i, j: (0, j)),
        ],
        out_specs=pl.BlockSpec((CONV_TM, tc), lambda i, j: (i, j)),
        out_shape=jax.ShapeDtypeStruct((rows, SSD_XBC), F32),
        scratch_shapes=[pltpu.VMEM((CONV_TM + 2 * CONV_HALO, tc), F32)],
        compiler_params=_cparams(("arbitrary", "arbitrary")),
        name="ssd_conv",
    )(proj, proj, proj, conv_w, conv_b.reshape(1, SSD_XBC))


def _ssd_scan_kernel(xbc_ref, misc_ref, dtt_ref, brow_ref, arow_ref, mrow_ref, bcol_ref, acol_ref,
                     e1_ref, e2_ref, y_ref, state_s, *, reverse):
    L = SSD_CHUNK

    @pl.when(pl.program_id(1) == 0)
    def _():
        state_s[...] = jnp.zeros_like(state_s)

    li = lax.broadcasted_iota(jnp.int32, (L, L), 0)
    si = lax.broadcasted_iota(jnp.int32, (L, L), 1)
    one_if = lambda cond: jnp.where(cond, 1.0, 0.0).astype(BF16)
    if reverse:
        cum_l = one_if(si >= li)
        cum_r = one_if(li >= si)
        causal = li <= si
    else:
        cum_l = one_if(si <= li)
        cum_r = one_if(li <= si)
        causal = li >= si

    dt_full = _softplus(misc_ref[...] + brow_ref[...])
    a_row = -jnp.exp(arow_ref[...]) * mrow_ref[...]
    ac_full = _dot_sel_left(cum_l, dt_full * a_row)
    hi, mid, lo = _split3(ac_full)
    e1, e2 = e1_ref[...], e2_ref[...]
    ac_exp = _dot(hi, e1) + _dot(mid, e1) + _dot(lo, e1)
    ac_exp2 = _dot(hi, e2) + _dot(mid, e2) + _dot(lo, e2)
    dt_exp = _dot_sel_right(dt_full, e1)
    dt_row = _softplus(dtt_ref[...] + bcol_ref[...])
    ac_row = _dot_sel_right(dt_row * (-jnp.exp(acol_ref[...])), cum_r)

    end = 0 if reverse else L - 1
    tot_exp = ac_exp[end:end + 1, :]
    xbc = xbc_ref[...]
    xdt = xbc[:, :HALF] * dt_exp
    xw = (xdt * jnp.exp(tot_exp - ac_exp)).astype(BF16)
    xdt_b = xdt.astype(BF16)
    eac = jnp.exp(ac_exp)
    chunk_decay = jnp.exp(tot_exp)
    lane = lax.broadcasted_iota(jnp.int32, (L, 128), 1)
    gw = SSD_HEADS // SSD_G * SSD_P
    for g in range(SSD_G):
        b_t = xbc[:, HALF + g * SSD_N:HALF + (g + 1) * SSD_N].T.astype(BF16)
        c_g = xbc[:, HALF + SSD_G * SSD_N + g * SSD_N:HALF + SSD_G * SSD_N + (g + 1) * SSD_N].astype(BF16)
        cb = _dot(c_g, b_t)
        st = state_s[:, g * gw:(g + 1) * gw]
        y_off = _dot(c_g, st.astype(BF16)) * eac[:, g * gw:(g + 1) * gw]
        state_s[:, g * gw:(g + 1) * gw] = st * chunk_decay[:, g * gw:(g + 1) * gw] + _dot(b_t, xw[:, g * gw:(g + 1) * gw])
        for pair in range(2):
            ha = 4 * g + 2 * pair
            xp = xdt_b[:, ha * SSD_P:ha * SSD_P + 128]
            ys = []
            for hd in (ha, ha + 1):
                seg = ac_exp2[:, hd * 128:(hd + 1) * 128] - ac_row[hd:hd + 1, :]
                m = (cb * jnp.exp(jnp.where(causal, seg, -jnp.inf))).astype(BF16)
                ys.append(_dot(m, xp))
            y_ref[:, ha * SSD_P:ha * SSD_P + 128] = (jnp.where(lane < SSD_P, ys[0], ys[1])
                                                     + y_off[:, pair * 128:(pair + 1) * 128])


def _ssd_scan(xbc_act, proj, dt_t, dt_bias, a_log, direction, n_batch):
    rows = xbc_act.shape[0]
    reverse = direction == 1
    nlc, ncc = SEQ // SSD_CHUNK, CTX // SSD_CHUNK
    lat_chunks = n_batch * nlc

    def blk(b, j):
        jc = (ncc - 1 - j) if reverse else j
        jl = (nlc - 1 - (j - ncc)) if reverse else (j - ncc)
        return jnp.where(j < ncc, lat_chunks + b * ncc + jc, b * nlc + jl)

    lane0 = MISC_DT + SSD_HEADS * direction
    pad = lambda v: jnp.zeros((1, 128), F32).at[0, lane0:lane0 + SSD_HEADS].set(v)
    b_row, a_row = pad(dt_bias[direction]), pad(a_log[direction])
    m_row = pad(jnp.ones((SSD_HEADS,), F32))
    b_col, a_col = dt_bias[direction].reshape(SSD_HEADS, 1), a_log[direction].reshape(SSD_HEADS, 1)
    sel = np.arange(128)[:, None] - lane0
    e1 = jnp.asarray(sel == (np.arange(HALF)[None] // SSD_P), BF16)
    e2 = jnp.asarray(sel == (np.arange(SSD_HEADS * 128)[None] // 128), BF16)
    const = lambda b, j: (0, 0)
    return pl.pallas_call(
        functools.partial(_ssd_scan_kernel, reverse=reverse),
        grid=(n_batch, ncc + nlc),
        in_specs=[
            pl.BlockSpec((SSD_CHUNK, SSD_XBC), lambda b, j: (blk(b, j), 0)),
            pl.BlockSpec((SSD_CHUNK, 128), lambda b, j: (blk(b, j), OFF_MISC // 128)),
            pl.BlockSpec((SSD_HEADS, SSD_CHUNK), lambda b, j: (direction, blk(b, j))),
            pl.BlockSpec((1, 128), const), pl.BlockSpec((1, 128), const), pl.BlockSpec((1, 128), const),
            pl.BlockSpec((SSD_HEADS, 1), const), pl.BlockSpec((SSD_HEADS, 1), const),
            pl.BlockSpec((128, HALF), const), pl.BlockSpec((128, SSD_HEADS * 128), const),
        ],
        out_specs=pl.BlockSpec((SSD_CHUNK, HALF), lambda b, j: (blk(b, j), 0)),
        out_shape=jax.ShapeDtypeStruct((rows, HALF), F32),
        scratch_shapes=[pltpu.VMEM((SSD_N, HALF), F32)],
        compiler_params=_cparams(("arbitrary", "arbitrary")),
        name="ssd_scan_bwd" if reverse else "ssd_scan_fwd",
    )(xbc_act, proj, dt_t, b_row, a_row, m_row, b_col, a_col, e1, e2)


def _ssd_out_kernel(yf_ref, yb_ref, xs_ref, z_ref, d_ref, g_ref, o_ref):
    y = yf_ref[...] + yb_ref[...] + d_ref[...] * xs_ref[...]
    v = y * _silu(z_ref[...])
    gw = HALF // SSD_G
    for g in range(SSD_G):
        vg = v[:, g * gw:(g + 1) * gw]
        o_ref[:, g * gw:(g + 1) * gw] = _rms(vg, g_ref[:, g * gw:(g + 1) * gw]).astype(o_ref.dtype)


def _ssd_out(y_f, y_b, xbc_act, proj, d_skip, norm_g, rows):
    tm = 256
    spec = pl.BlockSpec((tm, HALF), lambda i: (i, 0))
    const = pl.BlockSpec((1, HALF), lambda i: (0, 0))
    return pl.pallas_call(
        _ssd_out_kernel,
        grid=(rows // tm,),
        in_specs=[spec, spec, spec, pl.BlockSpec((tm, HALF), lambda i: (i, OFF_Z // HALF)), const, const],
        out_specs=spec,
        out_shape=jax.ShapeDtypeStruct((rows, HALF), BF16),
        compiler_params=_cparams(("arbitrary",)),
        name="ssd_out",
    )(y_f, y_b, xbc_act, proj, jnp.repeat(d_skip, SSD_P).reshape(1, HALF), norm_g.reshape(1, HALF))


def _permute_w_in(w):
    o = np.cumsum((0, MLA_HEADS * MLA_QK, MLA_RANK, MLA_ROPE, HALF, SSD_XBC, 2 * SSD_HEADS, 3 * HALF, 3 * D))
    q = w[:, o[0]:o[1]].reshape(D, MLA_HEADS, MLA_QK)
    parts = [
        w[:, o[7]:o[8]], w[:, o[6]:o[7]], w[:, o[4]:o[5]],
        q[:, :, :MLA_NOPE].reshape(D, HALF), w[:, o[3]:o[4]], q[:, :, MLA_NOPE:].reshape(D, MLA_HEADS * MLA_ROPE),
        w[:, o[1]:o[2]], w[:, o[2]:o[3]], w[:, o[5]:o[6]],
    ]
    used = sum(p.shape[1] for p in parts)
    parts.append(jnp.zeros((D, N_PROJ - used), w.dtype))
    return jnp.concatenate(parts, axis=1).astype(BF16)


def _rope_tables():
    t = jnp.arange(SEQ)
    pos = jnp.stack([t // GRID_W, t % GRID_W], axis=-1).astype(F32)
    n_freq = MLA_ROPE // 4
    inv_freq = ROPE_THETA ** (-jnp.arange(n_freq, dtype=F32) / n_freq)
    ang = pos[:, :, None] * inv_freq
    cos, sin = jnp.cos(ang), jnp.sin(ang)
    c64 = jnp.concatenate([cos[:, 0], cos[:, 0], cos[:, 1], cos[:, 1]], axis=-1)
    s64 = jnp.concatenate([-sin[:, 0], sin[:, 0], -sin[:, 1], sin[:, 1]], axis=-1)
    ident = 256
    cos_t = jnp.concatenate([jnp.tile(c64, (1, 2)), jnp.ones((ident, 128), F32)], axis=0)
    sin_t = jnp.concatenate([jnp.tile(s64, (1, 2)), jnp.zeros((ident, 128), F32)], axis=0)
    return cos_t, sin_t


def kernel(x, c, ctx, c_ctx, w_ada, b_ada, norm_g, ffn1_w_gate, ffn1_w_up, ffn1_w_down, ffn2_w_gate, ffn2_w_up, ffn2_w_down, w_in, mla_kv_norm_g, mla_w_uk, mla_w_uv, mla_q_norm_g, mla_k_norm_g, ssd_conv_w, ssd_conv_b, ssd_a_log, ssd_dt_bias, ssd_d, ssd_norm_g, na_q_norm_g, na_k_norm_g, na_rpb, w_branch, w_out):
    nb = x.shape[0]
    assert x.shape[1:] == (SEQ, D) and ctx.shape[1:] == (CTX, D) and nb < 16
    n_lat, n_ctx = nb * SEQ, nb * CTX
    n_tok = n_lat + n_ctx
    depth = w_ada.shape[0]

    h = jnp.concatenate([x.reshape(n_lat, D), ctx.reshape(n_ctx, D)], axis=0)
    cond = jnp.concatenate([c, c_ctx[None], jnp.zeros((16 - nb - 1, D), F32)], axis=0)
    m_all = _adaln(cond, w_ada, b_ada)
    cos_t, sin_t = _rope_tables()

    for l in range(depth):
        need_ctx = l < depth - 1
        rows_out = n_tok if need_ctx else n_lat
        m4 = m_all[l].reshape(16, N_MOD, 1, D)
        mod = dict(m4=m4, n_lat=n_lat, n_batch=nb)
        bf = lambda w: w.astype(BF16)

        u = _normmod(h, n_tok, norm_g[l, 0], base=0, **mod)
        h = _ffn(u, h, n_tok, bf(ffn1_w_gate[l]), bf(ffn1_w_up[l]), bf(ffn1_w_down[l]), base=0, **mod)

        u = _normmod(h, n_tok, norm_g[l, 1], base=3, **mod)
        proj = _in_proj(u, _permute_w_in(w_in[l]))

        q, k, v = _mla_prep(proj, cos_t, sin_t, bf(mla_w_uk[l]), bf(mla_w_uv[l]), mla_kv_norm_g[l],
                            mla_q_norm_g[l], mla_k_norm_g[l], n_lat)
        mla = _mla_attn(q, k, v, nb)

        na, na_c = _na_attn(proj, na_q_norm_g[l], na_k_norm_g[l], _na_bias_table(na_rpb[l]), nb)

        xbc_act = _ssd_conv(proj, ssd_conv_w[l], ssd_conv_b[l], n_lat)
        dt_t = proj[:, OFF_MISC + MISC_DT:OFF_MISC + MISC_DT + 2 * SSD_HEADS].T
        y_f = _ssd_scan(xbc_act, proj, dt_t, ssd_dt_bias[l], ssd_a_log[l], 0, nb)
        y_b = _ssd_scan(xbc_act, proj, dt_t, ssd_dt_bias[l], ssd_a_log[l], 1, nb)
        ssd = _ssd_out(y_f, y_b, xbc_act, proj, ssd_d[l], ssd_norm_g[l], rows_out)

        if need_ctx:
            mla = jnp.concatenate([mla, _mla_ctx_attn(q, k, v, nb)], axis=0)
            na = jnp.concatenate([na, na_c], axis=0)
        merged = _merge((mla, ssd, na), proj, bf(w_branch[l]), rows_out)
        h = _out_proj(merged, bf(w_out[l]), h, rows_out, **mod)

        u = _normmod(h, rows_out, norm_g[l, 2], base=6, **mod)
        h = _ffn(u, h, rows_out, bf(ffn2_w_gate[l]), bf(ffn2_w_up[l]), bf(ffn2_w_down[l]), base=6, **mod)

    return h[:n_lat].reshape(nb, SEQ, D)
```

```python
import functools
import math

import numpy as np
import jax
import jax.numpy as jnp
from jax import lax
from jax.experimental import pallas as pl
from jax.experimental.pallas import tpu as pltpu

F32 = jnp.float32
BF16 = jnp.bfloat16

D = 2048
SEQ = 2048
CTX = 256
DEPTH = 2
GRID_W = 64
GRID_H = SEQ // GRID_W
EPS = 1e-6
ROPE_THETA = 10000.0
N_MOD = 9
D_FF = 5632
HALF = D // 2

MLA_NOPE, MLA_ROPE, MLA_V = 128, 64, 128
MLA_QK = MLA_NOPE + MLA_ROPE
MLA_HEADS = 8
MLA_RANK = 512
MLA_PAD = 256

SSD_HEADS, SSD_P, SSD_G, SSD_N, SSD_CONV, SSD_CHUNK = 16, 64, 4, 128, 5, 128
SSD_XBC = HALF + 2 * SSD_G * SSD_N

NA_HEADS, NA_DH = 8, 128
NA_WIN_ROWS, NA_WIN_COLS = 8, 16
NA_QROWS, NA_QCOLS = 8, 16
NA_KROWS, NA_KCOLS = NA_QROWS + NA_WIN_ROWS - 1, NA_QCOLS + NA_WIN_COLS
NA_NRG, NA_NCB = GRID_H // NA_QROWS, GRID_W // NA_QCOLS
NEG_BIG = -1e30

OFF_GATE = 0
OFF_NA = 3 * D
OFF_XBC = OFF_NA + 3 * HALF
OFF_QNOPE = OFF_XBC + SSD_XBC
OFF_Z = OFF_QNOPE + HALF
OFF_QROPE = OFF_Z + HALF
OFF_CKV = OFF_QROPE + MLA_HEADS * MLA_ROPE
OFF_MISC = OFF_CKV + MLA_RANK
N_PROJ = 29 * 512
MISC_DT = MLA_ROPE

VMEM_LIMIT = 56 * 1024 * 1024


def _cparams(sem):
    return pltpu.CompilerParams(dimension_semantics=sem, vmem_limit_bytes=VMEM_LIMIT)


def _row_tile(rows, preferred):
    tm = preferred
    while rows % tm:
        tm //= 2
    return tm


def _dot(a, b):
    return jnp.dot(a, b, preferred_element_type=F32)


def _dot_nt(a, b):
    return lax.dot_general(a, b, (((1,), (1,)), ((), ())), preferred_element_type=F32)


def _split3(x):
    hi = x.astype(BF16)
    r1 = x - hi.astype(F32)
    mid = r1.astype(BF16)
    lo = (r1 - mid.astype(F32)).astype(BF16)
    return hi, mid, lo


def _dot_sel_right(x, m01):
    hi, mid, lo = _split3(x)
    return _dot(hi, m01) + _dot(mid, m01) + _dot(lo, m01)


def _dot_sel_left(m01, x):
    hi, mid, lo = _split3(x)
    return _dot(m01, hi) + _dot(m01, mid) + _dot(m01, lo)


def _silu(x):
    return x * jax.nn.sigmoid(x)


def _softplus(x):
    return jnp.maximum(x, 0.0) + jnp.log1p(jnp.exp(-jnp.abs(x)))


def _rms(x, g):
    return x * lax.rsqrt(jnp.mean(x * x, axis=-1, keepdims=True) + EPS) * g


def _ada_kernel(c_ref, w_ref, b_ref, o_ref):
    s = _silu(c_ref[...]).astype(BF16)
    o_ref[...] = _dot(s, w_ref[...].astype(BF16)) + b_ref[...]


def _adaln(cond, w_ada, b_ada):
    depth = w_ada.shape[0]
    n = N_MOD * D
    tn = 1024
    return pl.pallas_call(
        _ada_kernel,
        grid=(depth, n // tn),
        in_specs=[
            pl.BlockSpec((16, D), lambda l, j: (0, 0)),
            pl.BlockSpec((None, D, tn), lambda l, j: (l, 0, j)),
            pl.BlockSpec((None, 1, tn), lambda l, j: (l, 0, j)),
        ],
        out_specs=pl.BlockSpec((None, 16, tn), lambda l, j: (l, 0, j)),
        out_shape=jax.ShapeDtypeStruct((depth, 16, n), F32),
        compiler_params=_cparams(("arbitrary", "arbitrary")),
        name="adaln",
    )(cond, w_ada, b_ada.reshape(depth, 1, n))


def _mod_spec(tm, n_lat_tiles, n_batch, k):
    per = SEQ // tm

    def imap(i, *_):
        return (jnp.where(i < n_lat_tiles, i // per, n_batch), k, 0, 0)

    return pl.BlockSpec((None, None, 1, D), imap)


def _normmod_kernel(h_ref, g_ref, shift_ref, scale_ref, o_ref):
    y = _rms(h_ref[...], g_ref[...])
    o_ref[...] = (y * (1.0 + scale_ref[...]) + shift_ref[...]).astype(o_ref.dtype)


def _normmod(h, rows, g, m4, base, n_lat, n_batch):
    tm = 256
    nlt = n_lat // tm
    return pl.pallas_call(
        _normmod_kernel,
        grid=(rows // tm,),
        in_specs=[
            pl.BlockSpec((tm, D), lambda i: (i, 0)),
            pl.BlockSpec((1, D), lambda i: (0, 0)),
            _mod_spec(tm, nlt, n_batch, base),
            _mod_spec(tm, nlt, n_batch, base + 1),
        ],
        out_specs=pl.BlockSpec((tm, D), lambda i: (i, 0)),
        out_shape=jax.ShapeDtypeStruct((rows, D), BF16),
        compiler_params=_cparams(("arbitrary",)),
        name="normmod",
    )(h, g.reshape(1, D), m4, m4)


def _ffn_kernel(u_ref, wg_ref, wu_ref, wd_ref, h_ref, gate_ref, *rest, nf, with_next):
    if with_next:
        ng_ref, nshift_ref, nscale_ref, o_ref, un_ref, acc_ref = rest
    else:
        o_ref, acc_ref = rest
    f = pl.program_id(1)

    @pl.when(f == 0)
    def _():
        acc_ref[...] = jnp.zeros_like(acc_ref)

    u = u_ref[...]
    g = _dot(u, wg_ref[...])
    up = _dot(u, wu_ref[...])
    act = (_silu(g) * up).astype(BF16)
    acc_ref[...] += _dot(act, wd_ref[...])

    @pl.when(f == nf - 1)
    def _():
        h_new = h_ref[...] + (0.5 * gate_ref[...]) * acc_ref[...]
        o_ref[...] = h_new
        if with_next:
            y = _rms(h_new, ng_ref[...])
            un_ref[...] = (y * (1.0 + nscale_ref[...]) + nshift_ref[...]).astype(un_ref.dtype)


def _ffn(u, h, rows, wg, wu, wd, m4, base, n_lat, n_batch, nxt=None):
    tm, tf = _row_tile(rows, 512), 512
    nf = D_FF // tf
    nlt = n_lat // tm
    row_spec = pl.BlockSpec((tm, D), lambda i, f: (i, 0))
    in_specs = [
        row_spec,
        pl.BlockSpec((D, tf), lambda i, f: (0, f)),
        pl.BlockSpec((D, tf), lambda i, f: (0, f)),
        pl.BlockSpec((tf, D), lambda i, f: (f, 0)),
        row_spec,
        _mod_spec(tm, nlt, n_batch, base + 2),
    ]
    args = [u, wg, wu, wd, h, m4]
    out_specs, out_shape = row_spec, jax.ShapeDtypeStruct((rows, D), F32)
    if nxt is not None:
        n_g, n_m4, n_base = nxt
        in_specs += [pl.BlockSpec((1, D), lambda i, f: (0, 0)),
                     _mod_spec(tm, nlt, n_batch, n_base), _mod_spec(tm, nlt, n_batch, n_base + 1)]
        args += [n_g.reshape(1, D), n_m4, n_m4]
        out_specs = [row_spec, row_spec]
        out_shape = [out_shape, jax.ShapeDtypeStruct((rows, D), BF16)]
    return pl.pallas_call(
        functools.partial(_ffn_kernel, nf=nf, with_next=nxt is not None),
        grid=(rows // tm, nf),
        in_specs=in_specs,
        out_specs=out_specs,
        out_shape=out_shape,
        scratch_shapes=[pltpu.VMEM((tm, D), F32)],
        compiler_params=_cparams(("arbitrary", "arbitrary")),
        name="ffn",
    )(*args)


def _mm_kernel(a_ref, w_ref, o_ref):
    o_ref[...] = _dot(a_ref[...], w_ref[...]).astype(o_ref.dtype)


def _in_proj(u, w):
    rows = u.shape[0]
    tm, tn = _row_tile(rows, 1024), 512
    return pl.pallas_call(
        _mm_kernel,
        grid=(rows // tm, N_PROJ // tn),
        in_specs=[pl.BlockSpec((tm, D), lambda i, j: (i, 0)), pl.BlockSpec((D, tn), lambda i, j: (0, j))],
        out_specs=pl.BlockSpec((tm, tn), lambda i, j: (i, j)),
        out_shape=jax.ShapeDtypeStruct((rows, N_PROJ), F32),
        compiler_params=_cparams(("arbitrary", "arbitrary")),
        name="in_proj",
    )(u, w)


def _mix_kernel(b0_ref, b1_ref, b2_ref, wb_ref, g0_ref, g1_ref, g2_ref, wo_ref, h_ref, gate_ref,
                ng_ref, nshift_ref, nscale_ref, o_ref, un_ref):
    acc = jax.nn.sigmoid(g0_ref[...]) * _dot(b0_ref[...], wb_ref[0])
    acc = acc + jax.nn.sigmoid(g1_ref[...]) * _dot(b1_ref[...], wb_ref[1])
    acc = acc + jax.nn.sigmoid(g2_ref[...]) * _dot(b2_ref[...], wb_ref[2])
    h_new = h_ref[...] + gate_ref[...] * _dot(acc.astype(BF16), wo_ref[...])
    o_ref[...] = h_new
    y = _rms(h_new, ng_ref[...])
    un_ref[...] = (y * (1.0 + nscale_ref[...]) + nshift_ref[...]).astype(un_ref.dtype)


def _mix(branches, proj, w_branch, w_out, h, rows, m4, n_lat, n_batch, n_g):
    tm = 256
    nlt = n_lat // tm
    row = lambda i: (i, 0)
    once = pl.Buffered(1)
    return pl.pallas_call(
        _mix_kernel,
        grid=(rows // tm,),
        in_specs=[pl.BlockSpec((tm, HALF), row)] * 3
        + [pl.BlockSpec((3, HALF, D), lambda i: (0, 0, 0), pipeline_mode=once)]
        + [pl.BlockSpec((tm, D), functools.partial(lambda i, k: (i, k), k=k)) for k in range(3)]
        + [pl.BlockSpec((D, D), lambda i: (0, 0), pipeline_mode=once),
           pl.BlockSpec((tm, D), row),
           _mod_spec(tm, nlt, n_batch, 5),
           pl.BlockSpec((1, D), lambda i: (0, 0)),
           _mod_spec(tm, nlt, n_batch, 6), _mod_spec(tm, nlt, n_batch, 7)],
        out_specs=[pl.BlockSpec((tm, D), row), pl.BlockSpec((tm, D), row)],
        out_shape=[jax.ShapeDtypeStruct((rows, D), F32), jax.ShapeDtypeStruct((rows, D), BF16)],
        compiler_params=_cparams(("arbitrary",)),
        name="mix",
    )(*branches, w_branch, proj, proj, proj, w_out, h, m4, n_g.reshape(1, D), m4, m4)


def _lane_sum(x, lo, hi):
    lane = lax.broadcasted_iota(jnp.int32, x.shape, 1)
    return jnp.sum(jnp.where((lane >= lo) & (lane < hi), x, 0.0), axis=-1, keepdims=True)


def _rope128(y, cos, sin):
    lane = lax.broadcasted_iota(jnp.int32, y.shape, 1)
    partner = jnp.where((lane % 32) < 16, pltpu.roll(y, 112, 1), pltpu.roll(y, 16, 1))
    return y * cos + partner * sin


def _mla_prep_kernel(qn_ref, qr_ref, ckv_ref, misc_ref, cos_ref, sin_ref, wuk_ref, wuv_ref,
                     gkv_ref, gqn_ref, gqr_ref, gkn_ref, gkr_ref, q_ref, k_ref, v_ref, *, scale):
    cos, sin = cos_ref[...], sin_ref[...]
    lane = lax.broadcasted_iota(jnp.int32, cos.shape, 1)
    low = lane < MLA_ROPE

    ckv = _rms(ckv_ref[...], gkv_ref[...]).astype(BF16)
    k_nope = _dot(ckv, wuk_ref[...])
    v_ref[...] = _dot(ckv, wuv_ref[...]).astype(v_ref.dtype)

    misc = misc_ref[...]
    kr_ss = _lane_sum(misc * misc, 0, MLA_ROPE)
    kr_rot = jnp.where(low, _rope128(misc * gkr_ref[...], cos, sin), 0.0)
    kr_rot_hi = pltpu.roll(kr_rot, MLA_ROPE, 1)

    gqn, gqr, gkn = gqn_ref[...], gqr_ref[...], gkn_ref[...]
    for pair in range(MLA_HEADS // 2):
        qr = qr_ref[:, pair * 128:(pair + 1) * 128]
        qr2 = qr * qr
        ss_lo = _lane_sum(qr2, 0, MLA_ROPE)
        ss_hi = _lane_sum(qr2, MLA_ROPE, 128)
        rs = []
        for sub, ss_r in ((0, ss_lo), (1, ss_hi)):
            hd = 2 * pair + sub
            qn = qn_ref[:, hd * 128:(hd + 1) * 128]
            r = lax.rsqrt((jnp.sum(qn * qn, axis=-1, keepdims=True) + ss_r) * (1.0 / MLA_QK) + EPS)
            rs.append(r)
            q_ref[:, hd * MLA_PAD:hd * MLA_PAD + 128] = (qn * r * gqn * scale).astype(q_ref.dtype)
            kn = k_nope[:, hd * 128:(hd + 1) * 128]
            rk = lax.rsqrt((jnp.sum(kn * kn, axis=-1, keepdims=True) + kr_ss) * (1.0 / MLA_QK) + EPS)
            k_ref[:, hd * MLA_PAD:hd * MLA_PAD + 128] = (kn * rk * gkn).astype(k_ref.dtype)
            k_ref[:, hd * MLA_PAD + 128:(hd + 1) * MLA_PAD] = ((kr_rot if sub == 0 else kr_rot_hi) * rk).astype(k_ref.dtype)
        q_rot = _rope128(qr * jnp.where(low, rs[0], rs[1]) * gqr, cos, sin) * scale
        q_ref[:, (2 * pair) * MLA_PAD + 128:(2 * pair + 1) * MLA_PAD] = jnp.where(low, q_rot, 0.0).astype(q_ref.dtype)
        q_ref[:, (2 * pair + 1) * MLA_PAD + 128:(2 * pair + 2) * MLA_PAD] = jnp.where(low, 0.0, q_rot).astype(q_ref.dtype)


def _mla_prep(proj, cos_t, sin_t, w_uk, w_uv, g_kv, g_q, g_k, n_lat):
    rows = proj.shape[0]
    tm = 256
    nlt = n_lat // tm
    per = SEQ // tm

    def tab_map(i):
        return (jnp.where(i < nlt, i % per, per), 0)

    def row(v):
        return v.reshape(1, -1)

    def rope_gain(g):
        return jnp.concatenate([g[MLA_NOPE:], g[MLA_NOPE:]]).reshape(1, 128)

    const = lambda i: (0, 0)
    return pl.pallas_call(
        functools.partial(_mla_prep_kernel, scale=MLA_QK ** -0.5),
        grid=(rows // tm,),
        in_specs=[
            pl.BlockSpec((tm, HALF), lambda i: (i, OFF_QNOPE // HALF)),
            pl.BlockSpec((tm, 512), lambda i: (i, OFF_QROPE // 512)),
            pl.BlockSpec((tm, MLA_RANK), lambda i: (i, OFF_CKV // MLA_RANK)),
            pl.BlockSpec((tm, 128), lambda i: (i, OFF_MISC // 128)),
            pl.BlockSpec((tm, 128), tab_map),
            pl.BlockSpec((tm, 128), tab_map),
            pl.BlockSpec((MLA_RANK, HALF), const),
            pl.BlockSpec((MLA_RANK, HALF), const),
            pl.BlockSpec((1, MLA_RANK), const),
            pl.BlockSpec((1, 128), const),
            pl.BlockSpec((1, 128), const),
            pl.BlockSpec((1, 128), const),
            pl.BlockSpec((1, 128), const),
        ],
        out_specs=[
            pl.BlockSpec((tm, MLA_HEADS * MLA_PAD), lambda i: (i, 0)),
            pl.BlockSpec((tm, MLA_HEADS * MLA_PAD), lambda i: (i, 0)),
            pl.BlockSpec((tm, HALF), lambda i: (i, 0)),
        ],
        out_shape=[
            jax.ShapeDtypeStruct((rows, MLA_HEADS * MLA_PAD), BF16),
            jax.ShapeDtypeStruct((rows, MLA_HEADS * MLA_PAD), BF16),
            jax.ShapeDtypeStruct((rows, HALF), BF16),
        ],
        compiler_params=_cparams(("arbitrary",)),
        name="mla_prep",
    )(proj, proj, proj, proj, cos_t, sin_t, w_uk, w_uv, row(g_kv),
      row(g_q[:MLA_NOPE]), rope_gain(g_q), row(g_k[:MLA_NOPE]), rope_gain(g_k))


def _softmax2_pv(s1, s2, v1, v2):
    m = jnp.maximum(jnp.max(s1, axis=-1, keepdims=True), jnp.max(s2, axis=-1, keepdims=True))
    p1 = jnp.exp(s1 - m)
    p2 = jnp.exp(s2 - m)
    denom = jnp.sum(p1, axis=-1, keepdims=True) + jnp.sum(p2, axis=-1, keepdims=True)
    return (_dot(p1.astype(BF16), v1) + _dot(p2.astype(BF16), v2)) / denom


def _softmax_pv(s, v):
    p = jnp.exp(s - jnp.max(s, axis=-1, keepdims=True))
    return _dot(p.astype(BF16), v) / jnp.sum(p, axis=-1, keepdims=True)


def _mla_attn_kernel(q_ref, kl_ref, kc_ref, vl_ref, vc_ref, o_ref):
    q = q_ref[...]
    o = _softmax2_pv(_dot_nt(q, kl_ref[...]), _dot_nt(q, kc_ref[...]), vl_ref[...], vc_ref[...])
    o_ref[...] = o.astype(o_ref.dtype)


def _mla_attn(q, k, v, n_batch):
    tq = 512
    nq = SEQ // tq
    lat_blocks = n_batch * SEQ // CTX
    return pl.pallas_call(
        _mla_attn_kernel,
        grid=(n_batch, MLA_HEADS, nq),
        in_specs=[
            pl.BlockSpec((tq, MLA_PAD), lambda b, h, i: (b * nq + i, h)),
            pl.BlockSpec((SEQ, MLA_PAD), lambda b, h, i: (b, h)),
            pl.BlockSpec((CTX, MLA_PAD), lambda b, h, i: (lat_blocks + b, h)),
            pl.BlockSpec((SEQ, MLA_V), lambda b, h, i: (b, h)),
            pl.BlockSpec((CTX, MLA_V), lambda b, h, i: (lat_blocks + b, h)),
        ],
        out_specs=pl.BlockSpec((tq, MLA_V), lambda b, h, i: (b * nq + i, h)),
        out_shape=jax.ShapeDtypeStruct((n_batch * SEQ, HALF), BF16),
        compiler_params=_cparams(("arbitrary", "arbitrary", "arbitrary")),
        name="mla_attn",
    )(q, k, k, v, v)


def _ctx_attn_kernel(q_ref, k_ref, v_ref, o_ref):
    o_ref[...] = _softmax_pv(_dot_nt(q_ref[...], k_ref[...]), v_ref[...]).astype(o_ref.dtype)


def _mla_ctx_attn(q, k, v, n_batch):
    lat_blocks = n_batch * SEQ // CTX
    return pl.pallas_call(
        _ctx_attn_kernel,
        grid=(n_batch, MLA_HEADS),
        in_specs=[
            pl.BlockSpec((CTX, MLA_PAD), lambda b, h: (lat_blocks + b, h)),
            pl.BlockSpec((CTX, MLA_PAD), lambda b, h: (lat_blocks + b, h)),
            pl.BlockSpec((CTX, MLA_V), lambda b, h: (lat_blocks + b, h)),
        ],
        out_specs=pl.BlockSpec((CTX, MLA_V), lambda b, h: (b, h)),
        out_shape=jax.ShapeDtypeStruct((n_batch * CTX, HALF), BF16),
        compiler_params=_cparams(("arbitrary", "arbitrary")),
        name="mla_ctx_attn",
    )(q, k, v)


def _na_key_row0(rg):
    return int(np.clip(NA_QROWS * rg - NA_WIN_ROWS // 2, 0, GRID_H - NA_KROWS))


def _na_band0(n):
    return int(np.clip(NA_QCOLS * n - NA_WIN_COLS // 2, 0, GRID_W - NA_KCOLS))


def _na_bias_table(rpb):
    rg, n = np.arange(NA_NRG), np.arange(NA_NCB)
    r = NA_QROWS * rg[:, None] + np.arange(NA_QROWS)[None]
    key_r = np.array([_na_key_row0(g) for g in rg])[:, None] + np.arange(NA_KROWS)[None]
    row_start = np.clip(r - NA_WIN_ROWS // 2, 0, GRID_H - NA_WIN_ROWS)
    r_valid = (key_r[:, None, :] >= row_start[:, :, None]) & (key_r[:, None, :] < row_start[:, :, None] + NA_WIN_ROWS)
    r_off = np.clip(key_r[:, None, :] - r[:, :, None] + NA_WIN_ROWS - 1, 0, 2 * NA_WIN_ROWS - 2)
    c = NA_QCOLS * n[:, None] + np.arange(NA_QCOLS)[None]
    key_c = np.array([_na_band0(b) for b in n])[:, None] + np.arange(NA_KCOLS)[None]
    win_start = np.clip(c - NA_WIN_COLS // 2, 0, GRID_W - NA_WIN_COLS)
    c_valid = (key_c[:, None, :] >= win_start[:, :, None]) & (key_c[:, None, :] < win_start[:, :, None] + NA_WIN_COLS)
    c_off = np.clip(key_c[:, None, :] - c[:, :, None] + NA_WIN_COLS - 1, 0, 2 * NA_WIN_COLS - 2)
    valid = r_valid[:, None, :, None, :, None] & c_valid[None, :, None, :, None, :]
    r_sel = jnp.asarray(r_off[..., None] == np.arange(2 * NA_WIN_ROWS - 1), F32)
    c_sel = jnp.asarray(c_off[..., None] == np.arange(2 * NA_WIN_COLS - 1), F32)
    bias = jnp.einsum('gika,hab,njcb->hgnijkc', r_sel, rpb, c_sel, precision=lax.Precision.HIGHEST)
    bias = jnp.where(valid[None], bias, NEG_BIG)
    return bias.reshape(NA_HEADS, NA_NRG * NA_NCB, NA_QROWS * NA_QCOLS, NA_KROWS * NA_KCOLS)


def _na_kernel(q_ref, k_ref, v_ref, qc_ref, kc_ref, vc_ref, gq_ref, gk_ref, bias_ref, o_ref, oc_ref,
               qn_s, kn_s, *, scale):
    gq, gk = gq_ref[...], gk_ref[...]
    qn_s[...] = _rms(q_ref[...], gq) * scale
    kn_s[...] = _rms(k_ref[...], gk)
    kc = _rms(kc_ref[...], gk).astype(BF16)
    vc = vc_ref[...].astype(BF16)
    for rg in range(NA_NRG):
        kr0 = _na_key_row0(rg)
        for n in range(NA_NCB):
            b0 = _na_band0(n)
            q_rows = [(NA_QROWS * rg + i) * GRID_W + NA_QCOLS * n for i in range(NA_QROWS)]
            k_rows = [(kr0 + kr) * GRID_W + b0 for kr in range(NA_KROWS)]
            qb = jnp.concatenate([qn_s[pl.ds(t, NA_QCOLS), :] for t in q_rows], axis=0).astype(BF16)
            kb = jnp.concatenate([kn_s[pl.ds(t, NA_KCOLS), :] for t in k_rows], axis=0).astype(BF16)
            vb = jnp.concatenate([v_ref[pl.ds(t, NA_KCOLS), :] for t in k_rows], axis=0).astype(BF16)
            s_win = _dot_nt(qb, kb) + bias_ref[rg * NA_NCB + n]
            o = _softmax2_pv(s_win, _dot_nt(qb, kc), vb, vc).astype(o_ref.dtype)
            for i, t in enumerate(q_rows):
                o_ref[pl.ds(t, NA_QCOLS), :] = o[i * NA_QCOLS:(i + 1) * NA_QCOLS]
    qc = (_rms(qc_ref[...], gq) * scale).astype(BF16)
    oc_ref[...] = _softmax_pv(_dot_nt(qc, kc), vc).astype(oc_ref.dtype)


def _na_attn(proj, g_q, g_k, bias, n_batch):
    base = OFF_NA // NA_DH
    lat_blocks = n_batch * SEQ // CTX

    def lat(part):
        return pl.BlockSpec((SEQ, NA_DH), lambda h, b: (b, base + part * NA_HEADS + h))

    def cx(part):
        return pl.BlockSpec((CTX, NA_DH), lambda h, b: (lat_blocks + b, base + part * NA_HEADS + h))

    const = lambda h, b: (0, 0)
    nblk, nq, nk = bias.shape[1:]
    return pl.pallas_call(
        functools.partial(_na_kernel, scale=NA_DH ** -0.5),
        grid=(NA_HEADS, n_batch),
        in_specs=[lat(0), lat(1), lat(2), cx(0), cx(1), cx(2),
                  pl.BlockSpec((1, NA_DH), const), pl.BlockSpec((1, NA_DH), const),
                  pl.BlockSpec((None, nblk, nq, nk), lambda h, b: (h, 0, 0, 0))],
        out_specs=[pl.BlockSpec((SEQ, NA_DH), lambda h, b: (b, h)),
                   pl.BlockSpec((CTX, NA_DH), lambda h, b: (b, h))],
        out_shape=[jax.ShapeDtypeStruct((n_batch * SEQ, HALF), BF16),
                   jax.ShapeDtypeStruct((n_batch * CTX, HALF), BF16)],
        scratch_shapes=[pltpu.VMEM((SEQ, NA_DH), F32), pltpu.VMEM((SEQ, NA_DH), F32)],
        compiler_params=_cparams(("arbitrary", "arbitrary")),
        name="na_attn",
    )(proj, proj, proj, proj, proj, proj, g_q.reshape(1, NA_DH), g_k.reshape(1, NA_DH), bias)


CONV_TM = 256
CONV_HALO = 8


def _conv_kernel(prev_ref, cur_ref, next_ref, w_ref, b_ref, o_ref, pad_s, *, n_lat_tiles):
    i = pl.program_id(0)
    per = SEQ // CONV_TM
    is_ctx = i >= n_lat_tiles
    at_start = is_ctx | (i % per == 0)
    at_end = is_ctx | (i % per == per - 1)
    pad_s[0:CONV_HALO, :] = jnp.where(at_start, 0.0, prev_ref[...])
    pad_s[CONV_HALO:CONV_HALO + CONV_TM, :] = cur_ref[...]
    pad_s[CONV_HALO + CONV_TM:, :] = jnp.where(at_end, 0.0, next_ref[...])
    acc = jnp.broadcast_to(b_ref[...], cur_ref.shape)
    for k in range(SSD_CONV):
        acc = acc + w_ref[k:k + 1, :] * pad_s[pl.ds(CONV_HALO - SSD_CONV // 2 + k, CONV_TM), :]
    o_ref[...] = _silu(acc)


def _ssd_conv(proj, conv_w, conv_b, n_lat):
    rows = proj.shape[0]
    tc = 512
    nlt = n_lat // CONV_TM
    cb0 = OFF_XBC // tc
    hb = CONV_TM // CONV_HALO
    last = rows // CONV_HALO - 1
    return pl.pallas_call(
        functools.partial(_conv_kernel, n_lat_tiles=nlt),
        grid=(rows // CONV_TM, SSD_XBC // tc),
        in_specs=[
            pl.BlockSpec((CONV_HALO, tc), lambda i, j: (jnp.maximum(i * hb - 1, 0), cb0 + j)),
            pl.BlockSpec((CONV_TM, tc), lambda i, j: (i, cb0 + j)),
            pl.BlockSpec((CONV_HALO, tc), lambda i, j: (jnp.minimum((i + 1) * hb, last), cb0 + j)),
            pl.BlockSpec((SSD_CONV, tc), lambda i, j: (0, j)),
            pl.BlockSpec((1, tc), lambda i, j: (0, j)),
        ],
        out_specs=pl.BlockSpec((CONV_TM, tc), lambda i, j: (i, j)),
        out_shape=jax.ShapeDtypeStruct((rows, SSD_XBC), F32),
        scratch_shapes=[pltpu.VMEM((CONV_TM + 2 * CONV_HALO, tc), F32)],
        compiler_params=_cparams(("arbitrary", "arbitrary")),
        name="ssd_conv",
    )(proj, proj, proj, conv_w, conv_b.reshape(1, SSD_XBC))


def _ssd_scan_kernel(xbc_ref, misc_ref, dtt_ref, brow_ref, arow_ref, mrow_ref, bcol_ref, acol_ref,
                     e1_ref, e2_ref, y_ref, state_s, *, reverse):
    L = SSD_CHUNK

    @pl.when(pl.program_id(1) == 0)
    def _():
        state_s[...] = jnp.zeros_like(state_s)

    li = lax.broadcasted_iota(jnp.int32, (L, L), 0)
    si = lax.broadcasted_iota(jnp.int32, (L, L), 1)
    one_if = lambda cond: jnp.where(cond, 1.0, 0.0).astype(BF16)
    if reverse:
        cum_l = one_if(si >= li)
        cum_r = one_if(li >= si)
        causal = li <= si
    else:
        cum_l = one_if(si <= li)
        cum_r = one_if(li <= si)
        causal = li >= si

    dt_full = _softplus(misc_ref[...] + brow_ref[...])
    a_row = -jnp.exp(arow_ref[...]) * mrow_ref[...]
    ac_full = _dot_sel_left(cum_l, dt_full * a_row)
    hi, mid, lo = _split3(ac_full)
    e1, e2 = e1_ref[...], e2_ref[...]
    ac_exp = _dot(hi, e1) + _dot(mid, e1) + _dot(lo, e1)
    ac_exp2 = _dot(hi, e2) + _dot(mid, e2) + _dot(lo, e2)
    dt_exp = _dot_sel_right(dt_full, e1)
    dt_row = _softplus(dtt_ref[...] + bcol_ref[...])
    ac_row = _dot_sel_right(dt_row * (-jnp.exp(acol_ref[...])), cum_r)

    end = 0 if reverse else L - 1
    tot_exp = ac_exp[end:end + 1, :]
    xbc = xbc_ref[...]
    xdt = xbc[:, :HALF] * dt_exp
    xw = (xdt * jnp.exp(tot_exp - ac_exp)).astype(BF16)
    xdt_b = xdt.astype(BF16)
    eac = jnp.exp(ac_exp)
    chunk_decay = jnp.exp(tot_exp)
    lane = lax.broadcasted_iota(jnp.int32, (L, 128), 1)
    gw = SSD_HEADS // SSD_G * SSD_P
    for g in range(SSD_G):
        b_t = xbc[:, HALF + g * SSD_N:HALF + (g + 1) * SSD_N].T.astype(BF16)
        c_g = xbc[:, HALF + SSD_G * SSD_N + g * SSD_N:HALF + SSD_G * SSD_N + (g + 1) * SSD_N].astype(BF16)
        cb = _dot(c_g, b_t)
        st = state_s[:, g * gw:(g + 1) * gw]
        y_off = _dot(c_g, st.astype(BF16)) * eac[:, g * gw:(g + 1) * gw]
        state_s[:, g * gw:(g + 1) * gw] = st * chunk_decay[:, g * gw:(g + 1) * gw] + _dot(b_t, xw[:, g * gw:(g + 1) * gw])
        for pair in range(2):
            ha = 4 * g + 2 * pair
            xp = xdt_b[:, ha * SSD_P:ha * SSD_P + 128]
            ys = []
            for hd in (ha, ha + 1):
                seg = ac_exp2[:, hd * 128:(hd + 1) * 128] - ac_row[hd:hd + 1, :]
                m = (cb * jnp.exp(jnp.where(causal, seg, -jnp.inf))).astype(BF16)
                ys.append(_dot(m, xp))
            y_ref[:, ha * SSD_P:ha * SSD_P + 128] = (jnp.where(lane < SSD_P, ys[0], ys[1])
                                                     + y_off[:, pair * 128:(pair + 1) * 128])


def _ssd_scan(xbc_act, proj, dt_t, dt_bias, a_log, direction, n_batch):
    rows = xbc_act.shape[0]
    reverse = direction == 1
    nlc, ncc = SEQ // SSD_CHUNK, CTX // SSD_CHUNK
    lat_chunks = n_batch * nlc

    def blk(b, j):
        jc = (ncc - 1 - j) if reverse else j
        jl = (nlc - 1 - (j - ncc)) if reverse else (j - ncc)
        return jnp.where(j < ncc, lat_chunks + b * ncc + jc, b * nlc + jl)

    lane0 = MISC_DT + SSD_HEADS * direction
    pad = lambda v: jnp.zeros((1, 128), F32).at[0, lane0:lane0 + SSD_HEADS].set(v)
    b_row, a_row = pad(dt_bias[direction]), pad(a_log[direction])
    m_row = pad(jnp.ones((SSD_HEADS,), F32))
    b_col, a_col = dt_bias[direction].reshape(SSD_HEADS, 1), a_log[direction].reshape(SSD_HEADS, 1)
    sel = np.arange(128)[:, None] - lane0
    e1 = jnp.asarray(sel == (np.arange(HALF)[None] // SSD_P), BF16)
    e2 = jnp.asarray(sel == (np.arange(SSD_HEADS * 128)[None] // 128), BF16)
    const = lambda b, j: (0, 0)
    return pl.pallas_call(
        functools.partial(_ssd_scan_kernel, reverse=reverse),
        grid=(n_batch, ncc + nlc),
        in_specs=[
            pl.BlockSpec((SSD_CHUNK, SSD_XBC), lambda b, j: (blk(b, j), 0)),
            pl.BlockSpec((SSD_CHUNK, 128), lambda b, j: (blk(b, j), OFF_MISC // 128)),
            pl.BlockSpec((SSD_HEADS, SSD_CHUNK), lambda b, j: (direction, blk(b, j))),
            pl.BlockSpec((1, 128), const), pl.BlockSpec((1, 128), const), pl.BlockSpec((1, 128), const),
            pl.BlockSpec((SSD_HEADS, 1), const), pl.BlockSpec((SSD_HEADS, 1), const),
            pl.BlockSpec((128, HALF), const), pl.BlockSpec((128, SSD_HEADS * 128), const),
        ],
        out_specs=pl.BlockSpec((SSD_CHUNK, HALF), lambda b, j: (blk(b, j), 0)),
        out_shape=jax.ShapeDtypeStruct((rows, HALF), F32),
        scratch_shapes=[pltpu.VMEM((SSD_N, HALF), F32)],
        compiler_params=_cparams(("arbitrary", "arbitrary")),
        name="ssd_scan_bwd" if reverse else "ssd_scan_fwd",
    )(xbc_act, proj, dt_t, b_row, a_row, m_row, b_col, a_col, e1, e2)


def _ssd_out_kernel(yf_ref, yb_ref, xs_ref, z_ref, d_ref, g_ref, o_ref):
    y = yf_ref[...] + yb_ref[...] + d_ref[...] * xs_ref[...]
    v = y * _silu(z_ref[...])
    gw = HALF // SSD_G
    for g in range(SSD_G):
        vg = v[:, g * gw:(g + 1) * gw]
        o_ref[:, g * gw:(g + 1) * gw] = _rms(vg, g_ref[:, g * gw:(g + 1) * gw]).astype(o_ref.dtype)


def _ssd_out(y_f, y_b, xbc_act, proj, d_skip, norm_g, rows):
    tm = 256
    spec = pl.BlockSpec((tm, HALF), lambda i: (i, 0))
    const = pl.BlockSpec((1, HALF), lambda i: (0, 0))
    return pl.pallas_call(
        _ssd_out_kernel,
        grid=(rows // tm,),
        in_specs=[spec, spec, spec, pl.BlockSpec((tm, HALF), lambda i: (i, OFF_Z // HALF)), const, const],
        out_specs=spec,
        out_shape=jax.ShapeDtypeStruct((rows, HALF), BF16),
        compiler_params=_cparams(("arbitrary",)),
        name="ssd_out",
    )(y_f, y_b, xbc_act, proj, jnp.repeat(d_skip, SSD_P).reshape(1, HALF), norm_g.reshape(1, HALF))


def _permute_w_in(w):
    o = np.cumsum((0, MLA_HEADS * MLA_QK, MLA_RANK, MLA_ROPE, HALF, SSD_XBC, 2 * SSD_HEADS, 3 * HALF, 3 * D))
    q = w[:, o[0]:o[1]].reshape(D, MLA_HEADS, MLA_QK)
    parts = [
        w[:, o[7]:o[8]], w[:, o[6]:o[7]], w[:, o[4]:o[5]],
        q[:, :, :MLA_NOPE].reshape(D, HALF), w[:, o[3]:o[4]], q[:, :, MLA_NOPE:].reshape(D, MLA_HEADS * MLA_ROPE),
        w[:, o[1]:o[2]], w[:, o[2]:o[3]], w[:, o[5]:o[6]],
    ]
    used = sum(p.shape[1] for p in parts)
    parts.append(jnp.zeros((D, N_PROJ - used), w.dtype))
    return jnp.concatenate(parts, axis=1).astype(BF16)


def _rope_tables():
    t = jnp.arange(SEQ)
    pos = jnp.stack([t // GRID_W, t % GRID_W], axis=-1).astype(F32)
    n_freq = MLA_ROPE // 4
    inv_freq = ROPE_THETA ** (-jnp.arange(n_freq, dtype=F32) / n_freq)
    ang = pos[:, :, None] * inv_freq
    cos, sin = jnp.cos(ang), jnp.sin(ang)
    c64 = jnp.concatenate([cos[:, 0], cos[:, 0], cos[:, 1], cos[:, 1]], axis=-1)
    s64 = jnp.concatenate([-sin[:, 0], sin[:, 0], -sin[:, 1], sin[:, 1]], axis=-1)
    ident = 256
    cos_t = jnp.concatenate([jnp.tile(c64, (1, 2)), jnp.ones((ident, 128), F32)], axis=0)
    sin_t = jnp.concatenate([jnp.tile(s64, (1, 2)), jnp.zeros((ident, 128), F32)], axis=0)
    return cos_t, sin_t


def kernel(x, c, ctx, c_ctx, w_ada, b_ada, norm_g, ffn1_w_gate, ffn1_w_up, ffn1_w_down, ffn2_w_gate, ffn2_w_up, ffn2_w_down, w_in, mla_kv_norm_g, mla_w_uk, mla_w_uv, mla_q_norm_g, mla_k_norm_g, ssd_conv_w, ssd_conv_b, ssd_a_log, ssd_dt_bias, ssd_d, ssd_norm_g, na_q_norm_g, na_k_norm_g, na_rpb, w_branch, w_out):
    nb = x.shape[0]
    assert x.shape[1:] == (SEQ, D) and ctx.shape[1:] == (CTX, D) and nb < 16
    n_lat, n_ctx = nb * SEQ, nb * CTX
    n_tok = n_lat + n_ctx
    depth = w_ada.shape[0]

    h = jnp.concatenate([x.reshape(n_lat, D), ctx.reshape(n_ctx, D)], axis=0)
    cond = jnp.concatenate([c, c_ctx[None], jnp.zeros((16 - nb - 1, D), F32)], axis=0)
    m_all = _adaln(cond, w_ada, b_ada)
    cos_t, sin_t = _rope_tables()

    m4s = [m_all[l].reshape(16, N_MOD, 1, D) for l in range(depth)]
    u = _normmod(h, n_tok, norm_g[0, 0], m4s[0], base=0, n_lat=n_lat, n_batch=nb)
    for l in range(depth):
        need_ctx = l < depth - 1
        rows_out = n_tok if need_ctx else n_lat
        m4 = m4s[l]
        mod = dict(m4=m4, n_lat=n_lat, n_batch=nb)
        bf = lambda w: w.astype(BF16)

        h, u = _ffn(u, h, n_tok, bf(ffn1_w_gate[l]), bf(ffn1_w_up[l]), bf(ffn1_w_down[l]), base=0,
                    nxt=(norm_g[l, 1], m4, 3), **mod)
        proj = _in_proj(u, _permute_w_in(w_in[l]))

        q, k, v = _mla_prep(proj, cos_t, sin_t, bf(mla_w_uk[l]), bf(mla_w_uv[l]), mla_kv_norm_g[l],
                            mla_q_norm_g[l], mla_k_norm_g[l], n_lat)
        mla = _mla_attn(q, k, v, nb)

        na, na_c = _na_attn(proj, na_q_norm_g[l], na_k_norm_g[l], _na_bias_table(na_rpb[l]), nb)

        xbc_act = _ssd_conv(proj, ssd_conv_w[l], ssd_conv_b[l], n_lat)
        dt_t = proj[:, OFF_MISC + MISC_DT:OFF_MISC + MISC_DT + 2 * SSD_HEADS].T
        y_f = _ssd_scan(xbc_act, proj, dt_t, ssd_dt_bias[l], ssd_a_log[l], 0, nb)
        y_b = _ssd_scan(xbc_act, proj, dt_t, ssd_dt_bias[l], ssd_a_log[l], 1, nb)
        ssd = _ssd_out(y_f, y_b, xbc_act, proj, ssd_d[l], ssd_norm_g[l], rows_out)

        if need_ctx:
            mla = jnp.concatenate([mla, _mla_ctx_attn(q, k, v, nb)], axis=0)
            na = jnp.concatenate([na, na_c], axis=0)
        h, u = _mix((mla, ssd, na), proj, bf(w_branch[l]), bf(w_out[l]), h, rows_out, n_g=norm_g[l, 2], **mod)

        ffn2 = (bf(ffn2_w_gate[l]), bf(ffn2_w_up[l]), bf(ffn2_w_down[l]))
        if need_ctx:
            h, u = _ffn(u, h, rows_out, *ffn2, base=6, nxt=(norm_g[l + 1, 0], m4s[l + 1], 0), **mod)
        else:
            h = _ffn(u, h, rows_out, *ffn2, base=6, **mod)

    return h[:n_lat].reshape(nb, SEQ, D)
```

```python
import functools
import math

import numpy as np
import jax
import jax.numpy as jnp
from jax import lax
from jax.experimental import pallas as pl
from jax.experimental.pallas import tpu as pltpu

F32 = jnp.float32
BF16 = jnp.bfloat16

D = 2048
SEQ = 2048
CTX = 256
DEPTH = 2
GRID_W = 64
GRID_H = SEQ // GRID_W
EPS = 1e-6
ROPE_THETA = 10000.0
N_MOD = 9
D_FF = 5632
HALF = D // 2

MLA_NOPE, MLA_ROPE, MLA_V = 128, 64, 128
MLA_QK = MLA_NOPE + MLA_ROPE
MLA_HEADS = 8
MLA_RANK = 512
MLA_PAD = 256

SSD_HEADS, SSD_P, SSD_G, SSD_N, SSD_CONV, SSD_CHUNK = 16, 64, 4, 128, 5, 128
SSD_XBC = HALF + 2 * SSD_G * SSD_N

NA_HEADS, NA_DH = 8, 128
NA_WIN_ROWS, NA_WIN_COLS = 8, 16
NA_QROWS, NA_QCOLS = 8, 16
NA_KROWS, NA_KCOLS = NA_QROWS + NA_WIN_ROWS - 1, NA_QCOLS + NA_WIN_COLS
NA_NRG, NA_NCB = GRID_H // NA_QROWS, GRID_W // NA_QCOLS
NEG_BIG = -1e30

OFF_GATE = 0
OFF_NA = 3 * D
OFF_XBC = OFF_NA + 3 * HALF
OFF_QNOPE = OFF_XBC + SSD_XBC
OFF_Z = OFF_QNOPE + HALF
OFF_QROPE = OFF_Z + HALF
OFF_CKV = OFF_QROPE + MLA_HEADS * MLA_ROPE
OFF_MISC = OFF_CKV + MLA_RANK
N_PROJ = 29 * 512
MISC_DT = MLA_ROPE

VMEM_LIMIT = 56 * 1024 * 1024


def _cparams(sem):
    return pltpu.CompilerParams(dimension_semantics=sem, vmem_limit_bytes=VMEM_LIMIT)


def _row_tile(rows, preferred):
    tm = preferred
    while rows % tm:
        tm //= 2
    return tm


def _dot(a, b):
    return jnp.dot(a, b, preferred_element_type=F32)


def _dot_nt(a, b):
    return lax.dot_general(a, b, (((1,), (1,)), ((), ())), preferred_element_type=F32)


def _split3(x):
    hi = x.astype(BF16)
    r1 = x - hi.astype(F32)
    mid = r1.astype(BF16)
    lo = (r1 - mid.astype(F32)).astype(BF16)
    return hi, mid, lo


def _dot_sel_right(x, m01):
    hi, mid, lo = _split3(x)
    return _dot(hi, m01) + _dot(mid, m01) + _dot(lo, m01)


def _dot_sel_left(m01, x):
    hi, mid, lo = _split3(x)
    return _dot(m01, hi) + _dot(m01, mid) + _dot(m01, lo)


def _silu(x):
    return x * jax.nn.sigmoid(x)


def _softplus(x):
    return jnp.maximum(x, 0.0) + jnp.log1p(jnp.exp(-jnp.abs(x)))


def _rms(x, g):
    return x * lax.rsqrt(jnp.mean(x * x, axis=-1, keepdims=True) + EPS) * g


def _ada_kernel(c_ref, w_ref, b_ref, o_ref):
    s = _silu(c_ref[...]).astype(BF16)
    o_ref[...] = _dot(s, w_ref[...].astype(BF16)) + b_ref[...]


def _adaln(cond, w_ada, b_ada):
    depth = w_ada.shape[0]
    n = N_MOD * D
    tn = 1024
    return pl.pallas_call(
        _ada_kernel,
        grid=(depth, n // tn),
        in_specs=[
            pl.BlockSpec((16, D), lambda l, j: (0, 0)),
            pl.BlockSpec((None, D, tn), lambda l, j: (l, 0, j)),
            pl.BlockSpec((None, 1, tn), lambda l, j: (l, 0, j)),
        ],
        out_specs=pl.BlockSpec((None, 16, tn), lambda l, j: (l, 0, j)),
        out_shape=jax.ShapeDtypeStruct((depth, 16, n), F32),
        compiler_params=_cparams(("arbitrary", "arbitrary")),
        name="adaln",
    )(cond, w_ada, b_ada.reshape(depth, 1, n))


def _mod_spec(tm, n_lat_tiles, n_batch, k):
    per = SEQ // tm

    def imap(i, *_):
        return (jnp.where(i < n_lat_tiles, i // per, n_batch), k, 0, 0)

    return pl.BlockSpec((None, None, 1, D), imap)


def _normmod_kernel(h_ref, g_ref, shift_ref, scale_ref, o_ref):
    y = _rms(h_ref[...], g_ref[...])
    o_ref[...] = (y * (1.0 + scale_ref[...]) + shift_ref[...]).astype(o_ref.dtype)


def _normmod(h, rows, g, m4, base, n_lat, n_batch):
    tm = 256
    nlt = n_lat // tm
    return pl.pallas_call(
        _normmod_kernel,
        grid=(rows // tm,),
        in_specs=[
            pl.BlockSpec((tm, D), lambda i: (i, 0)),
            pl.BlockSpec((1, D), lambda i: (0, 0)),
            _mod_spec(tm, nlt, n_batch, base),
            _mod_spec(tm, nlt, n_batch, base + 1),
        ],
        out_specs=pl.BlockSpec((tm, D), lambda i: (i, 0)),
        out_shape=jax.ShapeDtypeStruct((rows, D), BF16),
        compiler_params=_cparams(("arbitrary",)),
        name="normmod",
    )(h, g.reshape(1, D), m4, m4)


def _ffn_kernel(u_ref, wg_ref, wu_ref, wd_ref, h_ref, gate_ref, *rest, nf, with_next):
    if with_next:
        ng_ref, nshift_ref, nscale_ref, o_ref, un_ref, acc_ref = rest
    else:
        o_ref, acc_ref = rest
    f = pl.program_id(1)

    @pl.when(f == 0)
    def _():
        acc_ref[...] = jnp.zeros_like(acc_ref)

    u = u_ref[...]
    g = _dot(u, wg_ref[...])
    up = _dot(u, wu_ref[...])
    act = (_silu(g) * up).astype(BF16)
    acc_ref[...] += _dot(act, wd_ref[...])

    @pl.when(f == nf - 1)
    def _():
        h_new = h_ref[...] + (0.5 * gate_ref[...]) * acc_ref[...]
        o_ref[...] = h_new
        if with_next:
            y = _rms(h_new, ng_ref[...])
            un_ref[...] = (y * (1.0 + nscale_ref[...]) + nshift_ref[...]).astype(un_ref.dtype)


def _ffn(u, h, rows, layer, wg, wu, wd, m4, base, n_lat, n_batch, nxt=None):
    tm, tf = _row_tile(rows, 512), 512
    nf = D_FF // tf
    nlt = n_lat // tm
    row_spec = pl.BlockSpec((tm, D), lambda i, f: (i, 0))
    in_specs = [
        row_spec,
        pl.BlockSpec((None, D, tf), lambda i, f: (layer, 0, f)),
        pl.BlockSpec((None, D, tf), lambda i, f: (layer, 0, f)),
        pl.BlockSpec((None, tf, D), lambda i, f: (layer, f, 0)),
        row_spec,
        _mod_spec(tm, nlt, n_batch, base + 2),
    ]
    args = [u, wg, wu, wd, h, m4]
    out_specs, out_shape = row_spec, jax.ShapeDtypeStruct((rows, D), F32)
    if nxt is not None:
        n_g, n_m4, n_base = nxt
        in_specs += [pl.BlockSpec((1, D), lambda i, f: (0, 0)),
                     _mod_spec(tm, nlt, n_batch, n_base), _mod_spec(tm, nlt, n_batch, n_base + 1)]
        args += [n_g.reshape(1, D), n_m4, n_m4]
        out_specs = [row_spec, row_spec]
        out_shape = [out_shape, jax.ShapeDtypeStruct((rows, D), BF16)]
    return pl.pallas_call(
        functools.partial(_ffn_kernel, nf=nf, with_next=nxt is not None),
        grid=(rows // tm, nf),
        in_specs=in_specs,
        out_specs=out_specs,
        out_shape=out_shape,
        scratch_shapes=[pltpu.VMEM((tm, D), F32)],
        compiler_params=_cparams(("arbitrary", "arbitrary")),
        name="ffn",
    )(*args)


def _mm_kernel(a_ref, w_ref, o_ref):
    o_ref[...] = _dot(a_ref[...], w_ref[...]).astype(o_ref.dtype)


def _in_proj(u, w, layer):
    rows = u.shape[0]
    tm, tn = _row_tile(rows, 2048), 512
    return pl.pallas_call(
        _mm_kernel,
        grid=(rows // tm, N_PROJ // tn),
        in_specs=[pl.BlockSpec((tm, D), lambda i, j: (i, 0)),
                  pl.BlockSpec((None, D, tn), lambda i, j: (layer, 0, j))],
        out_specs=pl.BlockSpec((tm, tn), lambda i, j: (i, j)),
        out_shape=jax.ShapeDtypeStruct((rows, N_PROJ), F32),
        compiler_params=_cparams(("arbitrary", "arbitrary")),
        name="in_proj",
    )(u, w)


def _mix_kernel(b0_ref, b1_ref, b2_ref, wb_ref, g0_ref, g1_ref, g2_ref, wo_ref, h_ref, gate_ref,
                ng_ref, nshift_ref, nscale_ref, o_ref, un_ref):
    acc = jax.nn.sigmoid(g0_ref[...]) * _dot(b0_ref[...], wb_ref[0])
    acc = acc + jax.nn.sigmoid(g1_ref[...]) * _dot(b1_ref[...], wb_ref[1])
    acc = acc + jax.nn.sigmoid(g2_ref[...]) * _dot(b2_ref[...], wb_ref[2])
    h_new = h_ref[...] + gate_ref[...] * _dot(acc.astype(BF16), wo_ref[...])
    o_ref[...] = h_new
    y = _rms(h_new, ng_ref[...])
    un_ref[...] = (y * (1.0 + nscale_ref[...]) + nshift_ref[...]).astype(un_ref.dtype)


def _mix(branches, proj, layer, w_branch, w_out, h, rows, m4, n_lat, n_batch, n_g):
    tm = 256
    nlt = n_lat // tm
    row = lambda i: (i, 0)
    once = pl.Buffered(1)
    return pl.pallas_call(
        _mix_kernel,
        grid=(rows // tm,),
        in_specs=[pl.BlockSpec((tm, HALF), row)] * 3
        + [pl.BlockSpec((None, 3, HALF, D), lambda i: (layer, 0, 0, 0), pipeline_mode=once)]
        + [pl.BlockSpec((tm, D), functools.partial(lambda i, k: (i, k), k=k)) for k in range(3)]
        + [pl.BlockSpec((None, D, D), lambda i: (layer, 0, 0), pipeline_mode=once),
           pl.BlockSpec((tm, D), row),
           _mod_spec(tm, nlt, n_batch, 5),
           pl.BlockSpec((1, D), lambda i: (0, 0)),
           _mod_spec(tm, nlt, n_batch, 6), _mod_spec(tm, nlt, n_batch, 7)],
        out_specs=[pl.BlockSpec((tm, D), row), pl.BlockSpec((tm, D), row)],
        out_shape=[jax.ShapeDtypeStruct((rows, D), F32), jax.ShapeDtypeStruct((rows, D), BF16)],
        compiler_params=_cparams(("arbitrary",)),
        name="mix",
    )(*branches, w_branch, proj, proj, proj, w_out, h, m4, n_g.reshape(1, D), m4, m4)


def _lane_sum(x, lo, hi):
    lane = lax.broadcasted_iota(jnp.int32, x.shape, 1)
    return jnp.sum(jnp.where((lane >= lo) & (lane < hi), x, 0.0), axis=-1, keepdims=True)


def _rope128(y, cos, sin):
    lane = lax.broadcasted_iota(jnp.int32, y.shape, 1)
    partner = jnp.where((lane % 32) < 16, pltpu.roll(y, 112, 1), pltpu.roll(y, 16, 1))
    return y * cos + partner * sin


def _mla_prep_kernel(qn_ref, qr_ref, ckv_ref, misc_ref, cos_ref, sin_ref, wuk_ref, wuv_ref,
                     gkv_ref, gqn_ref, gqr_ref, gkn_ref, gkr_ref, q_ref, k_ref, v_ref, *, scale):
    cos, sin = cos_ref[...], sin_ref[...]
    lane = lax.broadcasted_iota(jnp.int32, cos.shape, 1)
    low = lane < MLA_ROPE

    ckv = _rms(ckv_ref[...], gkv_ref[...]).astype(BF16)
    k_nope = _dot(ckv, wuk_ref[...])
    v_ref[...] = _dot(ckv, wuv_ref[...]).astype(v_ref.dtype)

    misc = misc_ref[...]
    kr_ss = _lane_sum(misc * misc, 0, MLA_ROPE)
    kr_rot = jnp.where(low, _rope128(misc * gkr_ref[...], cos, sin), 0.0)
    kr_rot_hi = pltpu.roll(kr_rot, MLA_ROPE, 1)

    gqn, gqr, gkn = gqn_ref[...], gqr_ref[...], gkn_ref[...]
    for pair in range(MLA_HEADS // 2):
        qr = qr_ref[:, pair * 128:(pair + 1) * 128]
        qr2 = qr * qr
        ss_lo = _lane_sum(qr2, 0, MLA_ROPE)
        ss_hi = _lane_sum(qr2, MLA_ROPE, 128)
        rs = []
        for sub, ss_r in ((0, ss_lo), (1, ss_hi)):
            hd = 2 * pair + sub
            qn = qn_ref[:, hd * 128:(hd + 1) * 128]
            r = lax.rsqrt((jnp.sum(qn * qn, axis=-1, keepdims=True) + ss_r) * (1.0 / MLA_QK) + EPS)
            rs.append(r)
            q_ref[:, hd * MLA_PAD:hd * MLA_PAD + 128] = (qn * r * gqn * scale).astype(q_ref.dtype)
            kn = k_nope[:, hd * 128:(hd + 1) * 128]
            rk = lax.rsqrt((jnp.sum(kn * kn, axis=-1, keepdims=True) + kr_ss) * (1.0 / MLA_QK) + EPS)
            k_ref[:, hd * MLA_PAD:hd * MLA_PAD + 128] = (kn * rk * gkn).astype(k_ref.dtype)
            k_ref[:, hd * MLA_PAD + 128:(hd + 1) * MLA_PAD] = ((kr_rot if sub == 0 else kr_rot_hi) * rk).astype(k_ref.dtype)
        q_rot = _rope128(qr * jnp.where(low, rs[0], rs[1]) * gqr, cos, sin) * scale
        q_ref[:, (2 * pair) * MLA_PAD + 128:(2 * pair + 1) * MLA_PAD] = jnp.where(low, q_rot, 0.0).astype(q_ref.dtype)
        q_ref[:, (2 * pair + 1) * MLA_PAD + 128:(2 * pair + 2) * MLA_PAD] = jnp.where(low, 0.0, q_rot).astype(q_ref.dtype)


def _mla_prep(proj, cos_t, sin_t, layer, w_uk, w_uv, g_kv, g_q, g_k, n_lat):
    rows = proj.shape[0]
    tm = 256
    nlt = n_lat // tm
    per = SEQ // tm

    def tab_map(i):
        return (jnp.where(i < nlt, i % per, per), 0)

    def row(v):
        return v.reshape(1, -1)

    def rope_gain(g):
        return jnp.concatenate([g[MLA_NOPE:], g[MLA_NOPE:]]).reshape(1, 128)

    const = lambda i: (0, 0)
    return pl.pallas_call(
        functools.partial(_mla_prep_kernel, scale=MLA_QK ** -0.5 * math.log2(math.e)),
        grid=(rows // tm,),
        in_specs=[
            pl.BlockSpec((tm, HALF), lambda i: (i, OFF_QNOPE // HALF)),
            pl.BlockSpec((tm, 512), lambda i: (i, OFF_QROPE // 512)),
            pl.BlockSpec((tm, MLA_RANK), lambda i: (i, OFF_CKV // MLA_RANK)),
            pl.BlockSpec((tm, 128), lambda i: (i, OFF_MISC // 128)),
            pl.BlockSpec((tm, 128), tab_map),
            pl.BlockSpec((tm, 128), tab_map),
            pl.BlockSpec((None, MLA_RANK, HALF), lambda i: (layer, 0, 0)),
            pl.BlockSpec((None, MLA_RANK, HALF), lambda i: (layer, 0, 0)),
            pl.BlockSpec((1, MLA_RANK), const),
            pl.BlockSpec((1, 128), const),
            pl.BlockSpec((1, 128), const),
            pl.BlockSpec((1, 128), const),
            pl.BlockSpec((1, 128), const),
        ],
        out_specs=[
            pl.BlockSpec((tm, MLA_HEADS * MLA_PAD), lambda i: (i, 0)),
            pl.BlockSpec((tm, MLA_HEADS * MLA_PAD), lambda i: (i, 0)),
            pl.BlockSpec((tm, HALF), lambda i: (i, 0)),
        ],
        out_shape=[
            jax.ShapeDtypeStruct((rows, MLA_HEADS * MLA_PAD), BF16),
            jax.ShapeDtypeStruct((rows, MLA_HEADS * MLA_PAD), BF16),
            jax.ShapeDtypeStruct((rows, HALF), BF16),
        ],
        compiler_params=_cparams(("arbitrary",)),
        name="mla_prep",
    )(proj, proj, proj, proj, cos_t, sin_t, w_uk, w_uv, row(g_kv),
      row(g_q[:MLA_NOPE]), rope_gain(g_q), row(g_k[:MLA_NOPE]), rope_gain(g_k))


LOG2E = math.log2(math.e)


def _softmax2_pv(s1, s2, v1, v2):
    m = jnp.maximum(jnp.max(s1, axis=-1, keepdims=True), jnp.max(s2, axis=-1, keepdims=True))
    p1 = jnp.exp2(s1 - m)
    p2 = jnp.exp2(s2 - m)
    denom = jnp.sum(p1, axis=-1, keepdims=True) + jnp.sum(p2, axis=-1, keepdims=True)
    return (_dot(p1.astype(BF16), v1) + _dot(p2.astype(BF16), v2)) / denom


def _softmax_pv(s, v):
    p = jnp.exp2(s - jnp.max(s, axis=-1, keepdims=True))
    return _dot(p.astype(BF16), v) / jnp.sum(p, axis=-1, keepdims=True)


def _mla_attn_kernel(q_ref, kl_ref, kc_ref, vl_ref, vc_ref, o_ref, *, sub):
    kl, kc, vl, vc = kl_ref[...], kc_ref[...], vl_ref[...], vc_ref[...]
    rows = q_ref.shape[0] // sub
    for t in range(sub):
        q = q_ref[t * rows:(t + 1) * rows, :]
        o = _softmax2_pv(_dot_nt(q, kl), _dot_nt(q, kc), vl, vc)
        o_ref[t * rows:(t + 1) * rows, :] = o.astype(o_ref.dtype)


def _mla_attn(q, k, v, n_batch):
    tq = 1024
    nq = SEQ // tq
    lat_blocks = n_batch * SEQ // CTX
    return pl.pallas_call(
        functools.partial(_mla_attn_kernel, sub=4),
        grid=(n_batch, MLA_HEADS, nq),
        in_specs=[
            pl.BlockSpec((tq, MLA_PAD), lambda b, h, i: (b * nq + i, h)),
            pl.BlockSpec((SEQ, MLA_PAD), lambda b, h, i: (b, h)),
            pl.BlockSpec((CTX, MLA_PAD), lambda b, h, i: (lat_blocks + b, h)),
            pl.BlockSpec((SEQ, MLA_V), lambda b, h, i: (b, h)),
            pl.BlockSpec((CTX, MLA_V), lambda b, h, i: (lat_blocks + b, h)),
        ],
        out_specs=pl.BlockSpec((tq, MLA_V), lambda b, h, i: (b * nq + i, h)),
        out_shape=jax.ShapeDtypeStruct((n_batch * SEQ, HALF), BF16),
        compiler_params=_cparams(("arbitrary", "arbitrary", "arbitrary")),
        name="mla_attn",
    )(q, k, k, v, v)


def _ctx_attn_kernel(q_ref, k_ref, v_ref, o_ref):
    o_ref[...] = _softmax_pv(_dot_nt(q_ref[...], k_ref[...]), v_ref[...]).astype(o_ref.dtype)


def _mla_ctx_attn(q, k, v, n_batch):
    lat_blocks = n_batch * SEQ // CTX
    return pl.pallas_call(
        _ctx_attn_kernel,
        grid=(n_batch, MLA_HEADS),
        in_specs=[
            pl.BlockSpec((CTX, MLA_PAD), lambda b, h: (lat_blocks + b, h)),
            pl.BlockSpec((CTX, MLA_PAD), lambda b, h: (lat_blocks + b, h)),
            pl.BlockSpec((CTX, MLA_V), lambda b, h: (lat_blocks + b, h)),
        ],
        out_specs=pl.BlockSpec((CTX, MLA_V), lambda b, h: (b, h)),
        out_shape=jax.ShapeDtypeStruct((n_batch * CTX, HALF), BF16),
        compiler_params=_cparams(("arbitrary", "arbitrary")),
        name="mla_ctx_attn",
    )(q, k, v)


def _na_key_row0(rg):
    return int(np.clip(NA_QROWS * rg - NA_WIN_ROWS // 2, 0, GRID_H - NA_KROWS))


def _na_band0(n):
    return int(np.clip(NA_QCOLS * n - NA_WIN_COLS // 2, 0, GRID_W - NA_KCOLS))


def _na_offsets(q0, k0, nq, nk, win, extent):
    qpos = q0 + np.arange(nq)[:, None]
    kpos = k0 + np.arange(nk)[None, :]
    start = np.clip(qpos - win // 2, 0, extent - win)
    valid = (kpos >= start) & (kpos < start + win)
    return np.where(valid, kpos - qpos + win - 1, 2 * win - 1)


def _na_block_classes():
    rows = [_na_offsets(NA_QROWS * g, _na_key_row0(g), NA_QROWS, NA_KROWS, NA_WIN_ROWS, GRID_H) for g in range(NA_NRG)]
    cols = [_na_offsets(NA_QCOLS * n, _na_band0(n), NA_QCOLS, NA_KCOLS, NA_WIN_COLS, GRID_W) for n in range(NA_NCB)]

    def classes(pats):
        reps, ids = [], []
        for p in pats:
            for k, r in enumerate(reps):
                if np.array_equal(p, r):
                    ids.append(k)
                    break
            else:
                ids.append(len(reps))
                reps.append(p)
        return np.stack(reps), ids

    return classes(rows), classes(cols)


def _na_bias_table(rpb):
    (r_off, _), (c_off, _) = _na_block_classes()
    ext = jnp.pad(rpb * LOG2E, ((0, 0), (0, 1), (0, 1)), constant_values=NEG_BIG)
    r_sel = jnp.asarray(r_off[..., None] == np.arange(2 * NA_WIN_ROWS), F32)
    c_sel = jnp.asarray(c_off[..., None] == np.arange(2 * NA_WIN_COLS), F32)
    bias = jnp.einsum('gika,hab,njcb->hgnijkc', r_sel, ext, c_sel, precision=lax.Precision.HIGHEST)
    return bias.reshape(NA_HEADS, r_off.shape[0] * c_off.shape[0], NA_QROWS * NA_QCOLS, NA_KROWS * NA_KCOLS)


def _na_kernel(q_ref, k_ref, v_ref, qc_ref, kc_ref, vc_ref, gq_ref, gk_ref, bias_ref, o_ref, oc_ref,
               qn_s, kn_s, *, scale):
    gq, gk = gq_ref[...], gk_ref[...]
    qn_s[...] = _rms(q_ref[...], gq) * scale
    kn_s[...] = _rms(k_ref[...], gk)
    kc = _rms(kc_ref[...], gk).astype(BF16)
    vc = vc_ref[...].astype(BF16)
    (_, row_cls), (col_reps, col_cls) = _na_block_classes()
    for rg in range(NA_NRG):
        kr0 = _na_key_row0(rg)
        for n in range(NA_NCB):
            b0 = _na_band0(n)
            q_rows = [(NA_QROWS * rg + i) * GRID_W + NA_QCOLS * n for i in range(NA_QROWS)]
            k_rows = [(kr0 + kr) * GRID_W + b0 for kr in range(NA_KROWS)]
            qb = jnp.concatenate([qn_s[pl.ds(t, NA_QCOLS), :] for t in q_rows], axis=0).astype(BF16)
            kb = jnp.concatenate([kn_s[pl.ds(t, NA_KCOLS), :] for t in k_rows], axis=0).astype(BF16)
            vb = jnp.concatenate([v_ref[pl.ds(t, NA_KCOLS), :] for t in k_rows], axis=0).astype(BF16)
            s_win = _dot_nt(qb, kb) + bias_ref[row_cls[rg] * len(col_reps) + col_cls[n]]
            o = _softmax2_pv(s_win, _dot_nt(qb, kc), vb, vc).astype(o_ref.dtype)
            for i, t in enumerate(q_rows):
                o_ref[pl.ds(t, NA_QCOLS), :] = o[i * NA_QCOLS:(i + 1) * NA_QCOLS]
    qc = (_rms(qc_ref[...], gq) * scale).astype(BF16)
    oc_ref[...] = _softmax_pv(_dot_nt(qc, kc), vc).astype(oc_ref.dtype)


def _na_attn(proj, g_q, g_k, bias, n_batch):
    base = OFF_NA // NA_DH
    lat_blocks = n_batch * SEQ // CTX

    def lat(part):
        return pl.BlockSpec((SEQ, NA_DH), lambda h, b: (b, base + part * NA_HEADS + h))

    def cx(part):
        return pl.BlockSpec((CTX, NA_DH), lambda h, b: (lat_blocks + b, base + part * NA_HEADS + h))

    const = lambda h, b: (0, 0)
    nblk, nq, nk = bias.shape[1:]
    return pl.pallas_call(
        functools.partial(_na_kernel, scale=NA_DH ** -0.5 * LOG2E),
        grid=(NA_HEADS, n_batch),
        in_specs=[lat(0), lat(1), lat(2), cx(0), cx(1), cx(2),
                  pl.BlockSpec((1, NA_DH), const), pl.BlockSpec((1, NA_DH), const),
                  pl.BlockSpec((None, nblk, nq, nk), lambda h, b: (h, 0, 0, 0))],
        out_specs=[pl.BlockSpec((SEQ, NA_DH), lambda h, b: (b, h)),
                   pl.BlockSpec((CTX, NA_DH), lambda h, b: (b, h))],
        out_shape=[jax.ShapeDtypeStruct((n_batch * SEQ, HALF), BF16),
                   jax.ShapeDtypeStruct((n_batch * CTX, HALF), BF16)],
        scratch_shapes=[pltpu.VMEM((SEQ, NA_DH), F32), pltpu.VMEM((SEQ, NA_DH), F32)],
        compiler_params=_cparams(("arbitrary", "arbitrary")),
        name="na_attn",
    )(proj, proj, proj, proj, proj, proj, g_q.reshape(1, NA_DH), g_k.reshape(1, NA_DH), bias)


CONV_TM = 256
CONV_HALO = 8


def _conv_kernel(prev_ref, cur_ref, next_ref, w_ref, b_ref, o_ref, pad_s, *, n_lat_tiles):
    i = pl.program_id(0)
    per = SEQ // CONV_TM
    is_ctx = i >= n_lat_tiles
    at_start = is_ctx | (i % per == 0)
    at_end = is_ctx | (i % per == per - 1)
    pad_s[0:CONV_HALO, :] = jnp.where(at_start, 0.0, prev_ref[...])
    pad_s[CONV_HALO:CONV_HALO + CONV_TM, :] = cur_ref[...]
    pad_s[CONV_HALO + CONV_TM:, :] = jnp.where(at_end, 0.0, next_ref[...])
    acc = jnp.broadcast_to(b_ref[...], cur_ref.shape)
    for k in range(SSD_CONV):
        acc = acc + w_ref[k:k + 1, :] * pad_s[pl.ds(CONV_HALO - SSD_CONV // 2 + k, CONV_TM), :]
    o_ref[...] = _silu(acc)


def _ssd_conv(proj, conv_w, conv_b, n_lat):
    rows = proj.shape[0]
    tc = 512
    nlt = n_lat // CONV_TM
    cb0 = OFF_XBC // tc
    hb = CONV_TM // CONV_HALO
    last = rows // CONV_HALO - 1
    return pl.pallas_call(
        functools.partial(_conv_kernel, n_lat_tiles=nlt),
        grid=(rows // CONV_TM, SSD_XBC // tc),
        in_specs=[
            pl.BlockSpec((CONV_HALO, tc), lambda i, j: (jnp.maximum(i * hb - 1, 0), cb0 + j)),
            pl.BlockSpec((CONV_TM, tc), lambda i, j: (i, cb0 + j)),
            pl.BlockSpec((CONV_HALO, tc), lambda i, j: (jnp.minimum((i + 1) * hb, last), cb0 + j)),
            pl.BlockSpec((SSD_CONV, tc), lambda i, j: (0, j)),
            pl.BlockSpec((1, tc), lambda i, j: (0, j)),
        ],
        out_specs=pl.BlockSpec((CONV_TM, tc), lambda i, j: (i, j)),
        out_shape=jax.ShapeDtypeStruct((rows, SSD_XBC), F32),
        scratch_shapes=[pltpu.VMEM((CONV_TM + 2 * CONV_HALO, tc), F32)],
        compiler_params=_cparams(("arbitrary", "arbitrary")),
        name="ssd_conv",
    )(proj, proj, proj, conv_w, conv_b.reshape(1, SSD_XBC))


def _ssd_scan_kernel(xbc_ref, misc_ref, dtt_ref, brow_ref, bcol_ref, acol_ref, e1_ref, *rest, reverse, final):
    if final:
        yprev_ref, z_ref, d_ref, g_ref, y_ref, state_s = rest
    else:
        y_ref, state_s = rest
    L = SSD_CHUNK

    @pl.when(pl.program_id(1) == 0)
    def _():
        state_s[...] = jnp.zeros_like(state_s)

    li = lax.broadcasted_iota(jnp.int32, (L, L), 0)
    si = lax.broadcasted_iota(jnp.int32, (L, L), 1)
    lane = lax.broadcasted_iota(jnp.int32, (L, 128), 1)
    if reverse:
        cum_r = jnp.where(li >= si, 1.0, 0.0).astype(BF16)
        causal = li <= si
    else:
        cum_r = jnp.where(li <= si, 1.0, 0.0).astype(BF16)
        causal = li >= si

    dt_exp = _dot_sel_right(_softplus(misc_ref[...] + brow_ref[...]), e1_ref[...])
    dt_row = _softplus(dtt_ref[...] + bcol_ref[...])
    ac_row = _dot_sel_right(dt_row * (-jnp.exp(acol_ref[...])), cum_r)
    ac_col = [jnp.broadcast_to(ac_row[hd:hd + 1, :], (L, L)).T for hd in range(SSD_HEADS)]
    ac_exp = jnp.concatenate([jnp.where(lane < SSD_P, ac_col[2 * p], ac_col[2 * p + 1])
                              for p in range(SSD_HEADS // 2)], axis=1)

    end = 0 if reverse else L - 1
    tot_exp = ac_exp[end:end + 1, :]
    xbc = xbc_ref[...]
    xdt = xbc[:, :HALF] * dt_exp
    xw = (xdt * jnp.exp(tot_exp - ac_exp)).astype(BF16)
    xdt_b = xdt.astype(BF16)
    eac = jnp.exp(ac_exp)
    chunk_decay = jnp.exp(tot_exp)
    gw = SSD_HEADS // SSD_G * SSD_P
    for g in range(SSD_G):
        gs = slice(g * gw, (g + 1) * gw)
        b_t = xbc[:, HALF + g * SSD_N:HALF + (g + 1) * SSD_N].T.astype(BF16)
        c_g = xbc[:, HALF + SSD_G * SSD_N + g * SSD_N:HALF + SSD_G * SSD_N + (g + 1) * SSD_N].astype(BF16)
        cb = _dot(c_g, b_t)
        st = state_s[:, gs]
        y_off = _dot(c_g, st.astype(BF16)) * eac[:, gs]
        state_s[:, gs] = st * chunk_decay[:, gs] + _dot(b_t, xw[:, gs])
        y_pairs = []
        for pair in range(2):
            ha = 4 * g + 2 * pair
            xp = xdt_b[:, ha * SSD_P:ha * SSD_P + 128]
            ys = []
            for hd in (ha, ha + 1):
                seg = ac_col[hd] - ac_row[hd:hd + 1, :]
                m = (cb * jnp.exp(jnp.where(causal, seg, -jnp.inf))).astype(BF16)
                ys.append(_dot(m, xp))
            y_pairs.append(jnp.where(lane < SSD_P, ys[0], ys[1]))
        y_g = jnp.concatenate(y_pairs, axis=1) + y_off
        if final:
            y_all = yprev_ref[:, gs] + y_g + d_ref[:, gs] * xbc[:, gs]
            y_ref[:, gs] = _rms(y_all * _silu(z_ref[:, gs]), g_ref[:, gs]).astype(y_ref.dtype)
        else:
            y_ref[:, gs] = y_g


def _ssd_scan(xbc_act, proj, dt_t, dt_bias, a_log, direction, n_batch, final=None):
    rows = xbc_act.shape[0]
    reverse = direction == 1
    nlc, ncc = SEQ // SSD_CHUNK, CTX // SSD_CHUNK
    lat_chunks = n_batch * nlc

    def blk(b, j):
        jc = (ncc - 1 - j) if reverse else j
        jl = (nlc - 1 - (j - ncc)) if reverse else (j - ncc)
        return jnp.where(j < ncc, lat_chunks + b * ncc + jc, b * nlc + jl)

    lane0 = MISC_DT + SSD_HEADS * direction
    b_row = jnp.zeros((1, 128), F32).at[0, lane0:lane0 + SSD_HEADS].set(dt_bias[direction])
    b_col, a_col = dt_bias[direction].reshape(SSD_HEADS, 1), a_log[direction].reshape(SSD_HEADS, 1)
    e1 = jnp.asarray((np.arange(128)[:, None] - lane0) == (np.arange(HALF)[None] // SSD_P), BF16)
    const = lambda b, j: (0, 0)
    chunk = pl.BlockSpec((SSD_CHUNK, HALF), lambda b, j: (blk(b, j), 0))
    in_specs = [
        pl.BlockSpec((SSD_CHUNK, SSD_XBC), lambda b, j: (blk(b, j), 0)),
        pl.BlockSpec((SSD_CHUNK, 128), lambda b, j: (blk(b, j), OFF_MISC // 128)),
        pl.BlockSpec((SSD_HEADS, SSD_CHUNK), lambda b, j: (direction, blk(b, j))),
        pl.BlockSpec((1, 128), const),
        pl.BlockSpec((SSD_HEADS, 1), const), pl.BlockSpec((SSD_HEADS, 1), const),
        pl.BlockSpec((128, HALF), const),
    ]
    args = [xbc_act, proj, dt_t, b_row, b_col, a_col, e1]
    if final is not None:
        y_prev, d_skip, norm_g = final
        in_specs += [chunk, pl.BlockSpec((SSD_CHUNK, HALF), lambda b, j: (blk(b, j), OFF_Z // HALF)),
                     pl.BlockSpec((1, HALF), const), pl.BlockSpec((1, HALF), const)]
        args += [y_prev, proj, jnp.repeat(d_skip, SSD_P).reshape(1, HALF), norm_g.reshape(1, HALF)]
    return pl.pallas_call(
        functools.partial(_ssd_scan_kernel, reverse=reverse, final=final is not None),
        grid=(n_batch, ncc + nlc),
        in_specs=in_specs,
        out_specs=chunk,
        out_shape=jax.ShapeDtypeStruct((rows, HALF), F32 if final is None else BF16),
        scratch_shapes=[pltpu.VMEM((SSD_N, HALF), F32)],
        compiler_params=_cparams(("arbitrary", "arbitrary")),
        name="ssd_scan_bwd" if reverse else "ssd_scan_fwd",
    )(*args)


def _permute_w_in(w):
    w = w.astype(BF16)
    o = np.cumsum((0, MLA_HEADS * MLA_QK, MLA_RANK, MLA_ROPE, HALF, SSD_XBC, 2 * SSD_HEADS, 3 * HALF, 3 * D))
    q_nope = [w[..., hd * MLA_QK:hd * MLA_QK + MLA_NOPE] for hd in range(MLA_HEADS)]
    q_rope = [w[..., hd * MLA_QK + MLA_NOPE:(hd + 1) * MLA_QK] for hd in range(MLA_HEADS)]
    parts = [w[..., o[7]:o[8]], w[..., o[6]:o[7]], w[..., o[4]:o[5]], *q_nope, w[..., o[3]:o[4]], *q_rope,
             w[..., o[1]:o[2]], w[..., o[2]:o[3]], w[..., o[5]:o[6]]]
    used = sum(p.shape[-1] for p in parts)
    parts.append(jnp.zeros(w.shape[:-1] + (N_PROJ - used,), BF16))
    return jnp.concatenate(parts, axis=-1)


def _rope_tables():
    t = jnp.arange(SEQ)
    pos = jnp.stack([t // GRID_W, t % GRID_W], axis=-1).astype(F32)
    n_freq = MLA_ROPE // 4
    inv_freq = ROPE_THETA ** (-jnp.arange(n_freq, dtype=F32) / n_freq)
    ang = pos[:, :, None] * inv_freq
    cos, sin = jnp.cos(ang), jnp.sin(ang)
    c64 = jnp.concatenate([cos[:, 0], cos[:, 0], cos[:, 1], cos[:, 1]], axis=-1)
    s64 = jnp.concatenate([-sin[:, 0], sin[:, 0], -sin[:, 1], sin[:, 1]], axis=-1)
    ident = 256
    cos_t = jnp.concatenate([jnp.tile(c64, (1, 2)), jnp.ones((ident, 128), F32)], axis=0)
    sin_t = jnp.concatenate([jnp.tile(s64, (1, 2)), jnp.zeros((ident, 128), F32)], axis=0)
    return cos_t, sin_t


def kernel(x, c, ctx, c_ctx, w_ada, b_ada, norm_g, ffn1_w_gate, ffn1_w_up, ffn1_w_down, ffn2_w_gate, ffn2_w_up, ffn2_w_down, w_in, mla_kv_norm_g, mla_w_uk, mla_w_uv, mla_q_norm_g, mla_k_norm_g, ssd_conv_w, ssd_conv_b, ssd_a_log, ssd_dt_bias, ssd_d, ssd_norm_g, na_q_norm_g, na_k_norm_g, na_rpb, w_branch, w_out):
    nb = x.shape[0]
    assert x.shape[1:] == (SEQ, D) and ctx.shape[1:] == (CTX, D) and nb < 16
    n_lat, n_ctx = nb * SEQ, nb * CTX
    n_tok = n_lat + n_ctx
    depth = w_ada.shape[0]

    h = jnp.concatenate([x.reshape(n_lat, D), ctx.reshape(n_ctx, D)], axis=0)
    cond = jnp.concatenate([c, c_ctx[None], jnp.zeros((16 - nb - 1, D), F32)], axis=0)
    m_all = _adaln(cond, w_ada, b_ada)
    cos_t, sin_t = _rope_tables()

    bf = lambda w: w.astype(BF16)
    ffn1 = (bf(ffn1_w_gate), bf(ffn1_w_up), bf(ffn1_w_down))
    ffn2 = (bf(ffn2_w_gate), bf(ffn2_w_up), bf(ffn2_w_down))
    w_in_p, w_uk, w_uv, w_br, w_o = _permute_w_in(w_in), bf(mla_w_uk), bf(mla_w_uv), bf(w_branch), bf(w_out)

    m4s = [m_all[l].reshape(16, N_MOD, 1, D) for l in range(depth)]
    u = _normmod(h, n_tok, norm_g[0, 0], m4s[0], base=0, n_lat=n_lat, n_batch=nb)
    for l in range(depth):
        need_ctx = l < depth - 1
        rows_out = n_tok if need_ctx else n_lat
        m4 = m4s[l]
        mod = dict(m4=m4, n_lat=n_lat, n_batch=nb)

        h, u = _ffn(u, h, n_tok, l, *ffn1, base=0, nxt=(norm_g[l, 1], m4, 3), **mod)
        proj = _in_proj(u, w_in_p, l)

        q, k, v = _mla_prep(proj, cos_t, sin_t, l, w_uk, w_uv, mla_kv_norm_g[l],
                            mla_q_norm_g[l], mla_k_norm_g[l], n_lat)
        mla = _mla_attn(q, k, v, nb)

        na, na_c = _na_attn(proj, na_q_norm_g[l], na_k_norm_g[l], _na_bias_table(na_rpb[l]), nb)

        xbc_act = _ssd_conv(proj, ssd_conv_w[l], ssd_conv_b[l], n_lat)
        dt_t = proj[:, OFF_MISC + MISC_DT:OFF_MISC + MISC_DT + 2 * SSD_HEADS].T
        y_f = _ssd_scan(xbc_act, proj, dt_t, ssd_dt_bias[l], ssd_a_log[l], 0, nb)
        ssd = _ssd_scan(xbc_act, proj, dt_t, ssd_dt_bias[l], ssd_a_log[l], 1, nb,
                        final=(y_f, ssd_d[l], ssd_norm_g[l]))

        if need_ctx:
            mla = jnp.concatenate([mla, _mla_ctx_attn(q, k, v, nb)], axis=0)
            na = jnp.concatenate([na, na_c], axis=0)
        h, u = _mix((mla, ssd, na), proj, l, w_br, w_o, h, rows_out, n_g=norm_g[l, 2], **mod)

        if need_ctx:
            h, u = _ffn(u, h, rows_out, l, *ffn2, base=6, nxt=(norm_g[l + 1, 0], m4s[l + 1], 0), **mod)
        else:
            h = _ffn(u, h, rows_out, l, *ffn2, base=6, **mod)

    return h[:n_lat].reshape(nb, SEQ, D)
```

```python
import functools
import math

import numpy as np
import jax
import jax.numpy as jnp
from jax import lax
from jax.experimental import pallas as pl
from jax.experimental.pallas import tpu as pltpu

F32 = jnp.float32
BF16 = jnp.bfloat16

D = 2048
SEQ = 2048
CTX = 256
DEPTH = 2
GRID_W = 64
GRID_H = SEQ // GRID_W
EPS = 1e-6
ROPE_THETA = 10000.0
N_MOD = 9
D_FF = 5632
HALF = D // 2

MLA_NOPE, MLA_ROPE, MLA_V = 128, 64, 128
MLA_QK = MLA_NOPE + MLA_ROPE
MLA_HEADS = 8
MLA_RANK = 512
MLA_PAD = 256

SSD_HEADS, SSD_P, SSD_G, SSD_N, SSD_CONV, SSD_CHUNK = 16, 64, 4, 128, 5, 128
SSD_XBC = HALF + 2 * SSD_G * SSD_N

NA_HEADS, NA_DH = 8, 128
NA_WIN_ROWS, NA_WIN_COLS = 8, 16
NA_QROWS, NA_QCOLS = 8, 16
NA_KROWS, NA_KCOLS = NA_QROWS + NA_WIN_ROWS - 1, NA_QCOLS + NA_WIN_COLS
NA_NRG, NA_NCB = GRID_H // NA_QROWS, GRID_W // NA_QCOLS
NEG_BIG = -1e30

OFF_GATE = 0
OFF_NA = 3 * D
OFF_XBC = OFF_NA + 3 * HALF
OFF_QNOPE = OFF_XBC + SSD_XBC
OFF_Z = OFF_QNOPE + HALF
OFF_QROPE = OFF_Z + HALF
OFF_CKV = OFF_QROPE + MLA_HEADS * MLA_ROPE
OFF_MISC = OFF_CKV + MLA_RANK
N_PROJ = 29 * 512
MISC_DT = MLA_ROPE

VMEM_LIMIT = 56 * 1024 * 1024


def _cparams(sem):
    return pltpu.CompilerParams(dimension_semantics=sem, vmem_limit_bytes=VMEM_LIMIT)


def _row_tile(rows, preferred):
    tm = preferred
    while rows % tm:
        tm //= 2
    return tm


def _dot(a, b):
    return jnp.dot(a, b, preferred_element_type=F32)


def _dot_nt(a, b):
    return lax.dot_general(a, b, (((1,), (1,)), ((), ())), preferred_element_type=F32)


def _split3(x):
    hi = x.astype(BF16)
    r1 = x - hi.astype(F32)
    mid = r1.astype(BF16)
    lo = (r1 - mid.astype(F32)).astype(BF16)
    return hi, mid, lo


def _dot_sel_right(x, m01):
    hi, mid, lo = _split3(x)
    return _dot(hi, m01) + _dot(mid, m01) + _dot(lo, m01)


def _dot_sel_left(m01, x):
    hi, mid, lo = _split3(x)
    return _dot(m01, hi) + _dot(m01, mid) + _dot(m01, lo)


def _silu(x):
    return x * jax.nn.sigmoid(x)


def _softplus(x):
    return jnp.maximum(x, 0.0) + jnp.log1p(jnp.exp(-jnp.abs(x)))


def _rms(x, g):
    return x * lax.rsqrt(jnp.mean(x * x, axis=-1, keepdims=True) + EPS) * g


def _ada_kernel(c_ref, w_ref, b_ref, o_ref):
    s = _silu(c_ref[...]).astype(BF16)
    o_ref[...] = _dot(s, w_ref[...].astype(BF16)) + b_ref[...]


def _adaln(cond, w_ada, b_ada):
    depth = w_ada.shape[0]
    n = N_MOD * D
    tn = 1024
    return pl.pallas_call(
        _ada_kernel,
        grid=(depth, n // tn),
        in_specs=[
            pl.BlockSpec((16, D), lambda l, j: (0, 0)),
            pl.BlockSpec((None, D, tn), lambda l, j: (l, 0, j)),
            pl.BlockSpec((None, 1, tn), lambda l, j: (l, 0, j)),
        ],
        out_specs=pl.BlockSpec((None, 16, tn), lambda l, j: (l, 0, j)),
        out_shape=jax.ShapeDtypeStruct((depth, 16, n), F32),
        compiler_params=_cparams(("arbitrary", "arbitrary")),
        name="adaln",
    )(cond, w_ada, b_ada.reshape(depth, 1, n))


def _mod_spec(tm, n_lat_tiles, n_batch, k):
    per = SEQ // tm

    def imap(i, *_):
        return (jnp.where(i < n_lat_tiles, i // per, n_batch), k, 0, 0)

    return pl.BlockSpec((None, None, 1, D), imap)


def _normmod_kernel(x_ref, c_ref, g_ref, shift_ref, scale_ref, h_ref, o_ref, *, n_lat_tiles):
    is_lat = pl.program_id(0) < n_lat_tiles
    h = jnp.where(is_lat, x_ref[...], c_ref[...])
    h_ref[...] = h
    y = _rms(h, g_ref[...])
    o_ref[...] = (y * (1.0 + scale_ref[...]) + shift_ref[...]).astype(o_ref.dtype)


def _normmod(x2d, ctx2d, g, m4, base, n_batch):
    tm = 256
    n_lat, n_ctx = x2d.shape[0], ctx2d.shape[0]
    nlt = n_lat // tm
    rows = n_lat + n_ctx
    row = pl.BlockSpec((tm, D), lambda i: (i, 0))
    return pl.pallas_call(
        functools.partial(_normmod_kernel, n_lat_tiles=nlt),
        grid=(rows // tm,),
        in_specs=[
            pl.BlockSpec((tm, D), lambda i: (jnp.minimum(i, nlt - 1), 0)),
            pl.BlockSpec((tm, D), lambda i: (jnp.maximum(i - nlt, 0), 0)),
            pl.BlockSpec((1, D), lambda i: (0, 0)),
            _mod_spec(tm, nlt, n_batch, base),
            _mod_spec(tm, nlt, n_batch, base + 1),
        ],
        out_specs=[row, row],
        out_shape=[jax.ShapeDtypeStruct((rows, D), F32), jax.ShapeDtypeStruct((rows, D), BF16)],
        compiler_params=_cparams(("arbitrary",)),
        name="normmod",
    )(x2d, ctx2d, g.reshape(1, D), m4, m4)


def _ffn_kernel(u_ref, wg_ref, wu_ref, wd_ref, h_ref, gate_ref, *rest, nf, with_next):
    if with_next:
        ng_ref, nshift_ref, nscale_ref, o_ref, un_ref = rest
    else:
        (o_ref,) = rest
    f = pl.program_id(1)

    @pl.when(f == 0)
    def _():
        o_ref[...] = jnp.zeros_like(o_ref)

    u = u_ref[...]
    g = _dot(u, wg_ref[...])
    up = _dot(u, wu_ref[...])
    act = (_silu(g) * up).astype(BF16)
    o_ref[...] += _dot(act, wd_ref[...])

    @pl.when(f == nf - 1)
    def _():
        h_new = h_ref[...] + (0.5 * gate_ref[...]) * o_ref[...]
        o_ref[...] = h_new
        if with_next:
            y = _rms(h_new, ng_ref[...])
            un_ref[...] = (y * (1.0 + nscale_ref[...]) + nshift_ref[...]).astype(un_ref.dtype)


def _ffn(u, h, rows, layer, wg, wu, wd, m4, base, n_lat, n_batch, nxt=None):
    tm, tf = _row_tile(rows, 1024), 256
    nf = D_FF // tf
    nlt = n_lat // tm
    row_spec = pl.BlockSpec((tm, D), lambda i, f: (i, 0))
    in_specs = [
        row_spec,
        pl.BlockSpec((None, D, tf), lambda i, f: (layer, 0, f)),
        pl.BlockSpec((None, D, tf), lambda i, f: (layer, 0, f)),
        pl.BlockSpec((None, tf, D), lambda i, f: (layer, f, 0)),
        pl.BlockSpec((tm, D), lambda i, f: (i, 0), pipeline_mode=pl.Buffered(1)),
        _mod_spec(tm, nlt, n_batch, base + 2),
    ]
    args = [u, wg, wu, wd, h, m4]
    out_specs, out_shape = row_spec, jax.ShapeDtypeStruct((rows, D), F32)
    if nxt is not None:
        n_g, n_m4, n_base = nxt
        in_specs += [pl.BlockSpec((1, D), lambda i, f: (0, 0)),
                     _mod_spec(tm, nlt, n_batch, n_base), _mod_spec(tm, nlt, n_batch, n_base + 1)]
        args += [n_g.reshape(1, D), n_m4, n_m4]
        out_specs = [row_spec, row_spec]
        out_shape = [out_shape, jax.ShapeDtypeStruct((rows, D), BF16)]
    return pl.pallas_call(
        functools.partial(_ffn_kernel, nf=nf, with_next=nxt is not None),
        grid=(rows // tm, nf),
        in_specs=in_specs,
        out_specs=out_specs,
        out_shape=out_shape,
        compiler_params=_cparams(("arbitrary", "arbitrary")),
        name="ffn",
    )(*args)


def _mm_kernel(a_ref, w_ref, o_ref):
    o_ref[...] = _dot(a_ref[...], w_ref[...]).astype(o_ref.dtype)


def _in_proj(u, w, layer):
    rows = u.shape[0]
    tm, tn = _row_tile(rows, 2048), 512
    return pl.pallas_call(
        _mm_kernel,
        grid=(rows // tm, N_PROJ // tn),
        in_specs=[pl.BlockSpec((tm, D), lambda i, j: (i, 0)),
                  pl.BlockSpec((None, D, tn), lambda i, j: (layer, 0, j))],
        out_specs=pl.BlockSpec((tm, tn), lambda i, j: (i, j)),
        out_shape=jax.ShapeDtypeStruct((rows, N_PROJ), F32),
        compiler_params=_cparams(("arbitrary", "arbitrary")),
        name="in_proj",
    )(u, w)


def _mix_kernel(b0_ref, b1_ref, b2_ref, wb_ref, g0_ref, g1_ref, g2_ref, wo_ref, h_ref, gate_ref,
                ng_ref, nshift_ref, nscale_ref, o_ref, un_ref):
    acc = jax.nn.sigmoid(g0_ref[...]) * _dot(b0_ref[...], wb_ref[0])
    b1 = b1_ref[...].reshape(b0_ref.shape)
    acc = acc + jax.nn.sigmoid(g1_ref[...]) * _dot(b1, wb_ref[1])
    acc = acc + jax.nn.sigmoid(g2_ref[...]) * _dot(b2_ref[...], wb_ref[2])
    h_new = h_ref[...] + gate_ref[...] * _dot(acc.astype(BF16), wo_ref[...])
    o_ref[...] = h_new
    y = _rms(h_new, ng_ref[...])
    un_ref[...] = (y * (1.0 + nscale_ref[...]) + nshift_ref[...]).astype(un_ref.dtype)


def _mix(branches, proj, layer, w_branch, w_out, h, rows, m4, n_lat, n_batch, n_g):
    tm = 256
    nlt = n_lat // tm
    row = lambda i: (i, 0)
    once = pl.Buffered(1)
    cpt = tm // SSD_CHUNK
    per = SEQ // tm
    _, groups = _ssd_streams(n_batch)

    def ssd_map(i):
        b = jnp.where(i < nlt, i // per, i - nlt)
        pblk = jnp.where(i < nlt, CTX // tm + i % per, 0)
        return (b % groups, pblk, b // groups, 0, 0)

    return pl.pallas_call(
        _mix_kernel,
        grid=(rows // tm,),
        in_specs=[pl.BlockSpec((tm, HALF), row), pl.BlockSpec((None, cpt, None, SSD_CHUNK, HALF), ssd_map),
                  pl.BlockSpec((tm, HALF), row)]
        + [pl.BlockSpec((None, 3, HALF, D), lambda i: (layer, 0, 0, 0), pipeline_mode=once)]
        + [pl.BlockSpec((tm, D), functools.partial(lambda i, k: (i, k), k=k)) for k in range(3)]
        + [pl.BlockSpec((None, D, D), lambda i: (layer, 0, 0), pipeline_mode=once),
           pl.BlockSpec((tm, D), row),
           _mod_spec(tm, nlt, n_batch, 5),
           pl.BlockSpec((1, D), lambda i: (0, 0)),
           _mod_spec(tm, nlt, n_batch, 6), _mod_spec(tm, nlt, n_batch, 7)],
        out_specs=[pl.BlockSpec((tm, D), row), pl.BlockSpec((tm, D), row)],
        out_shape=[jax.ShapeDtypeStruct((rows, D), F32), jax.ShapeDtypeStruct((rows, D), BF16)],
        compiler_params=_cparams(("arbitrary",)),
        name="mix",
    )(*branches, w_branch, proj, proj, proj, w_out, h, m4, n_g.reshape(1, D), m4, m4)


def _rope128(y, cos, sin):
    lane = lax.broadcasted_iota(jnp.int32, y.shape, 1)
    partner = jnp.where((lane % 32) < 16, pltpu.roll(y, 112, 1), pltpu.roll(y, 16, 1))
    return y * cos + partner * sin


def _mla_prep_kernel(qn_ref, qr_ref, ckv_ref, misc_ref, cos_ref, sin_ref, wuk_ref, wuv_ref,
                     gkv_ref, gqn_ref, gqr_ref, gkn_ref, gkr_ref, q_ref, k_ref, v_ref, *, scale):
    cos, sin = cos_ref[...], sin_ref[...]
    lane = lax.broadcasted_iota(jnp.int32, cos.shape, 1)
    low = lane < MLA_ROPE

    ckv = _rms(ckv_ref[...], gkv_ref[...]).astype(BF16)
    k_nope = _dot(ckv, wuk_ref[...])
    v_ref[...] = _dot(ckv, wuv_ref[...]).astype(v_ref.dtype)

    misc = misc_ref[...]
    kr_sq = jnp.where(low, misc * misc, 0.0)
    kr_rot = jnp.where(low, _rope128(misc * gkr_ref[...], cos, sin), 0.0)
    kr_rot_hi = pltpu.roll(kr_rot, MLA_ROPE, 1)

    gqn, gqr, gkn = gqn_ref[...], gqr_ref[...], gkn_ref[...]
    for pair in range(MLA_HEADS // 2):
        qr = qr_ref[:, pair * 128:(pair + 1) * 128]
        qr2 = qr * qr
        rs = []
        for sub in (0, 1):
            hd = 2 * pair + sub
            qn = qn_ref[:, hd * 128:(hd + 1) * 128]
            q_sq = qn * qn + jnp.where(low if sub == 0 else ~low, qr2, 0.0)
            r = lax.rsqrt(jnp.sum(q_sq, axis=-1, keepdims=True) * (1.0 / MLA_QK) + EPS)
            rs.append(r)
            q_ref[:, hd * MLA_PAD:hd * MLA_PAD + 128] = (qn * r * gqn * scale).astype(q_ref.dtype)
            kn = k_nope[:, hd * 128:(hd + 1) * 128]
            rk = lax.rsqrt(jnp.sum(kn * kn + kr_sq, axis=-1, keepdims=True) * (1.0 / MLA_QK) + EPS)
            k_ref[:, hd * MLA_PAD:hd * MLA_PAD + 128] = (kn * rk * gkn).astype(k_ref.dtype)
            k_ref[:, hd * MLA_PAD + 128:(hd + 1) * MLA_PAD] = ((kr_rot if sub == 0 else kr_rot_hi) * rk).astype(k_ref.dtype)
        q_rot = _rope128(qr * jnp.where(low, rs[0], rs[1]) * gqr, cos, sin) * scale
        q_ref[:, (2 * pair) * MLA_PAD + 128:(2 * pair + 1) * MLA_PAD] = jnp.where(low, q_rot, 0.0).astype(q_ref.dtype)
        q_ref[:, (2 * pair + 1) * MLA_PAD + 128:(2 * pair + 2) * MLA_PAD] = jnp.where(low, 0.0, q_rot).astype(q_ref.dtype)


def _mla_prep(proj, cos_t, sin_t, layer, w_uk, w_uv, g_kv, g_q, g_k, n_lat):
    rows = proj.shape[0]
    tm = 256
    nlt = n_lat // tm
    per = SEQ // tm

    def tab_map(i):
        return (jnp.where(i < nlt, i % per, per), 0)

    def row(v):
        return v.reshape(1, -1)

    def rope_gain(g):
        return jnp.concatenate([g[MLA_NOPE:], g[MLA_NOPE:]]).reshape(1, 128)

    const = lambda i: (0, 0)
    return pl.pallas_call(
        functools.partial(_mla_prep_kernel, scale=MLA_QK ** -0.5 * math.log2(math.e)),
        grid=(rows // tm,),
        in_specs=[
            pl.BlockSpec((tm, HALF), lambda i: (i, OFF_QNOPE // HALF)),
            pl.BlockSpec((tm, 512), lambda i: (i, OFF_QROPE // 512)),
            pl.BlockSpec((tm, MLA_RANK), lambda i: (i, OFF_CKV // MLA_RANK)),
            pl.BlockSpec((tm, 128), lambda i: (i, OFF_MISC // 128)),
            pl.BlockSpec((tm, 128), tab_map),
            pl.BlockSpec((tm, 128), tab_map),
            pl.BlockSpec((None, MLA_RANK, HALF), lambda i: (layer, 0, 0)),
            pl.BlockSpec((None, MLA_RANK, HALF), lambda i: (layer, 0, 0)),
            pl.BlockSpec((1, MLA_RANK), const),
            pl.BlockSpec((1, 128), const),
            pl.BlockSpec((1, 128), const),
            pl.BlockSpec((1, 128), const),
            pl.BlockSpec((1, 128), const),
        ],
        out_specs=[
            pl.BlockSpec((tm, MLA_HEADS * MLA_PAD), lambda i: (i, 0)),
            pl.BlockSpec((tm, MLA_HEADS * MLA_PAD), lambda i: (i, 0)),
            pl.BlockSpec((tm, HALF), lambda i: (i, 0)),
        ],
        out_shape=[
            jax.ShapeDtypeStruct((rows, MLA_HEADS * MLA_PAD), BF16),
            jax.ShapeDtypeStruct((rows, MLA_HEADS * MLA_PAD), BF16),
            jax.ShapeDtypeStruct((rows, HALF), BF16),
        ],
        compiler_params=_cparams(("arbitrary",)),
        name="mla_prep",
    )(proj, proj, proj, proj, cos_t, sin_t, w_uk, w_uv, row(g_kv),
      row(g_q[:MLA_NOPE]), rope_gain(g_q), row(g_k[:MLA_NOPE]), rope_gain(g_k))


LOG2E = math.log2(math.e)


def _softmax2_pv(s1, s2, v1, v2):
    m = jnp.maximum(jnp.max(s1, axis=-1, keepdims=True), jnp.max(s2, axis=-1, keepdims=True))
    p1 = jnp.exp2(s1 - m)
    p2 = jnp.exp2(s2 - m)
    denom = jnp.sum(p1, axis=-1, keepdims=True) + jnp.sum(p2, axis=-1, keepdims=True)
    return (_dot(p1.astype(BF16), v1) + _dot(p2.astype(BF16), v2)) / denom


def _softmax_pv(s, v):
    p = jnp.exp2(s - jnp.max(s, axis=-1, keepdims=True))
    return _dot(p.astype(BF16), v) / jnp.sum(p, axis=-1, keepdims=True)


def _mla_attn_kernel(q_ref, kl_ref, kc_ref, vl_ref, vc_ref, o_ref, *, sub):
    kl, kc, vl, vc = kl_ref[...], kc_ref[...], vl_ref[...], vc_ref[...]
    rows = q_ref.shape[0] // sub
    for t in range(sub):
        q = q_ref[t * rows:(t + 1) * rows, :]
        o = _softmax2_pv(_dot_nt(q, kl), _dot_nt(q, kc), vl, vc)
        o_ref[t * rows:(t + 1) * rows, :] = o.astype(o_ref.dtype)


def _mla_attn(q, k, v, n_batch):
    tq = 1024
    nq = SEQ // tq
    lat_blocks = n_batch * SEQ // CTX
    return pl.pallas_call(
        functools.partial(_mla_attn_kernel, sub=4),
        grid=(n_batch, MLA_HEADS, nq),
        in_specs=[
            pl.BlockSpec((tq, MLA_PAD), lambda b, h, i: (b * nq + i, h)),
            pl.BlockSpec((SEQ, MLA_PAD), lambda b, h, i: (b, h)),
            pl.BlockSpec((CTX, MLA_PAD), lambda b, h, i: (lat_blocks + b, h)),
            pl.BlockSpec((SEQ, MLA_V), lambda b, h, i: (b, h)),
            pl.BlockSpec((CTX, MLA_V), lambda b, h, i: (lat_blocks + b, h)),
        ],
        out_specs=pl.BlockSpec((tq, MLA_V), lambda b, h, i: (b * nq + i, h)),
        out_shape=jax.ShapeDtypeStruct((n_batch * SEQ, HALF), BF16),
        compiler_params=_cparams(("arbitrary", "arbitrary", "arbitrary")),
        name="mla_attn",
    )(q, k, k, v, v)


def _ctx_attn_kernel(q_ref, k_ref, v_ref, o_ref):
    o_ref[...] = _softmax_pv(_dot_nt(q_ref[...], k_ref[...]), v_ref[...]).astype(o_ref.dtype)


def _mla_ctx_attn(q, k, v, n_batch):
    lat_blocks = n_batch * SEQ // CTX
    return pl.pallas_call(
        _ctx_attn_kernel,
        grid=(n_batch, MLA_HEADS),
        in_specs=[
            pl.BlockSpec((CTX, MLA_PAD), lambda b, h: (lat_blocks + b, h)),
            pl.BlockSpec((CTX, MLA_PAD), lambda b, h: (lat_blocks + b, h)),
            pl.BlockSpec((CTX, MLA_V), lambda b, h: (lat_blocks + b, h)),
        ],
        out_specs=pl.BlockSpec((CTX, MLA_V), lambda b, h: (b, h)),
        out_shape=jax.ShapeDtypeStruct((n_batch * CTX, HALF), BF16),
        compiler_params=_cparams(("arbitrary", "arbitrary")),
        name="mla_ctx_attn",
    )(q, k, v)


def _na_key_row0(rg):
    return int(np.clip(NA_QROWS * rg - NA_WIN_ROWS // 2, 0, GRID_H - NA_KROWS))


def _na_band0(n):
    return int(np.clip(NA_QCOLS * n - NA_WIN_COLS // 2, 0, GRID_W - NA_KCOLS))


def _na_offsets(q0, k0, nq, nk, win, extent):
    qpos = q0 + np.arange(nq)[:, None]
    kpos = k0 + np.arange(nk)[None, :]
    start = np.clip(qpos - win // 2, 0, extent - win)
    valid = (kpos >= start) & (kpos < start + win)
    return np.where(valid, kpos - qpos + win - 1, 2 * win - 1)


def _na_block_classes():
    rows = [_na_offsets(NA_QROWS * g, _na_key_row0(g), NA_QROWS, NA_KROWS, NA_WIN_ROWS, GRID_H) for g in range(NA_NRG)]
    cols = [_na_offsets(NA_QCOLS * n, _na_band0(n), NA_QCOLS, NA_KCOLS, NA_WIN_COLS, GRID_W) for n in range(NA_NCB)]

    def classes(pats):
        reps, ids = [], []
        for p in pats:
            for k, r in enumerate(reps):
                if np.array_equal(p, r):
                    ids.append(k)
                    break
            else:
                ids.append(len(reps))
                reps.append(p)
        return np.stack(reps), ids

    return classes(rows), classes(cols)


def _na_bias_table(rpb):
    (r_off, _), (c_off, _) = _na_block_classes()
    ext = jnp.pad(rpb * LOG2E, ((0, 0), (0, 1), (0, 1)), constant_values=NEG_BIG)
    r_sel = jnp.asarray(r_off[..., None] == np.arange(2 * NA_WIN_ROWS), F32)
    c_sel = jnp.asarray(c_off[..., None] == np.arange(2 * NA_WIN_COLS), F32)
    bias = jnp.einsum('gika,hab,njcb->hgnijkc', r_sel, ext, c_sel, precision=lax.Precision.HIGHEST)
    return bias.reshape(NA_HEADS, r_off.shape[0] * c_off.shape[0], NA_QROWS * NA_QCOLS, NA_KROWS * NA_KCOLS)


def _na_kernel(q_ref, k_ref, v_ref, qc_ref, kc_ref, vc_ref, gq_ref, gk_ref, bias_ref, o_ref, oc_ref,
               qn_s, kn_s, *, scale):
    gq, gk = gq_ref[...], gk_ref[...]
    qn_s[...] = _rms(q_ref[...], gq) * scale
    kn_s[...] = _rms(k_ref[...], gk)
    kc = _rms(kc_ref[...], gk).astype(BF16)
    vc = vc_ref[...].astype(BF16)
    (_, row_cls), (col_reps, col_cls) = _na_block_classes()
    for rg in range(NA_NRG):
        kr0 = _na_key_row0(rg)
        for n in range(NA_NCB):
            b0 = _na_band0(n)
            q_rows = [(NA_QROWS * rg + i) * GRID_W + NA_QCOLS * n for i in range(NA_QROWS)]
            k_rows = [(kr0 + kr) * GRID_W + b0 for kr in range(NA_KROWS)]
            qb = jnp.concatenate([qn_s[pl.ds(t, NA_QCOLS), :] for t in q_rows], axis=0).astype(BF16)
            kb = jnp.concatenate([kn_s[pl.ds(t, NA_KCOLS), :] for t in k_rows], axis=0).astype(BF16)
            vb = jnp.concatenate([v_ref[pl.ds(t, NA_KCOLS), :] for t in k_rows], axis=0).astype(BF16)
            s_win = _dot_nt(qb, kb) + bias_ref[row_cls[rg] * len(col_reps) + col_cls[n]]
            o = _softmax2_pv(s_win, _dot_nt(qb, kc), vb, vc).astype(o_ref.dtype)
            for i, t in enumerate(q_rows):
                o_ref[pl.ds(t, NA_QCOLS), :] = o[i * NA_QCOLS:(i + 1) * NA_QCOLS]
    qc = (_rms(qc_ref[...], gq) * scale).astype(BF16)
    oc_ref[...] = _softmax_pv(_dot_nt(qc, kc), vc).astype(oc_ref.dtype)


def _na_attn(proj, g_q, g_k, bias, n_batch):
    base = OFF_NA // NA_DH
    lat_blocks = n_batch * SEQ // CTX

    def lat(part):
        return pl.BlockSpec((SEQ, NA_DH), lambda h, b: (b, base + part * NA_HEADS + h))

    def cx(part):
        return pl.BlockSpec((CTX, NA_DH), lambda h, b: (lat_blocks + b, base + part * NA_HEADS + h))

    const = lambda h, b: (0, 0)
    nblk, nq, nk = bias.shape[1:]
    return pl.pallas_call(
        functools.partial(_na_kernel, scale=NA_DH ** -0.5 * LOG2E),
        grid=(NA_HEADS, n_batch),
        in_specs=[lat(0), lat(1), lat(2), cx(0), cx(1), cx(2),
                  pl.BlockSpec((1, NA_DH), const), pl.BlockSpec((1, NA_DH), const),
                  pl.BlockSpec((None, nblk, nq, nk), lambda h, b: (h, 0, 0, 0))],
        out_specs=[pl.BlockSpec((SEQ, NA_DH), lambda h, b: (b, h)),
                   pl.BlockSpec((CTX, NA_DH), lambda h, b: (b, h))],
        out_shape=[jax.ShapeDtypeStruct((n_batch * SEQ, HALF), BF16),
                   jax.ShapeDtypeStruct((n_batch * CTX, HALF), BF16)],
        scratch_shapes=[pltpu.VMEM((SEQ, NA_DH), F32), pltpu.VMEM((SEQ, NA_DH), F32)],
        compiler_params=_cparams(("arbitrary", "arbitrary")),
        name="na_attn",
    )(proj, proj, proj, proj, proj, proj, g_q.reshape(1, NA_DH), g_k.reshape(1, NA_DH), bias)


CONV_TM = 256
CONV_HALO = 8


def _conv_kernel(prev_ref, cur_ref, next_ref, w_ref, b_ref, o_ref, pad_s, *, n_lat_tiles):
    i = pl.program_id(0)
    per = SEQ // CONV_TM
    is_ctx = i >= n_lat_tiles
    at_start = is_ctx | (i % per == 0)
    at_end = is_ctx | (i % per == per - 1)
    pad_s[0:CONV_HALO, :] = jnp.where(at_start, 0.0, prev_ref[...])
    pad_s[CONV_HALO:CONV_HALO + CONV_TM, :] = cur_ref[...]
    pad_s[CONV_HALO + CONV_TM:, :] = jnp.where(at_end, 0.0, next_ref[...])
    acc = jnp.broadcast_to(b_ref[...], cur_ref.shape)
    for k in range(SSD_CONV):
        acc = acc + w_ref[k:k + 1, :] * pad_s[pl.ds(CONV_HALO - SSD_CONV // 2 + k, CONV_TM), :]
    o_ref[...] = _silu(acc)


def _ssd_conv(proj, conv_w, conv_b, n_lat):
    rows = proj.shape[0]
    tc = 512
    nlt = n_lat // CONV_TM
    cb0 = OFF_XBC // tc
    hb = CONV_TM // CONV_HALO
    last = rows // CONV_HALO - 1
    return pl.pallas_call(
        functools.partial(_conv_kernel, n_lat_tiles=nlt),
        grid=(rows // CONV_TM, SSD_XBC // tc),
        in_specs=[
            pl.BlockSpec((CONV_HALO, tc), lambda i, j: (jnp.maximum(i * hb - 1, 0), cb0 + j)),
            pl.BlockSpec((CONV_TM, tc), lambda i, j: (i, cb0 + j)),
            pl.BlockSpec((CONV_HALO, tc), lambda i, j: (jnp.minimum((i + 1) * hb, last), cb0 + j)),
            pl.BlockSpec((SSD_CONV, tc), lambda i, j: (0, j)),
            pl.BlockSpec((1, tc), lambda i, j: (0, j)),
        ],
        out_specs=pl.BlockSpec((CONV_TM, tc), lambda i, j: (i, j)),
        out_shape=jax.ShapeDtypeStruct((rows, SSD_XBC), F32),
        scratch_shapes=[pltpu.VMEM((CONV_TM + 2 * CONV_HALO, tc), F32)],
        compiler_params=_cparams(("arbitrary", "arbitrary")),
        name="ssd_conv",
    )(proj, proj, proj, conv_w, conv_b.reshape(1, SSD_XBC))


def _ssd_scan_kernel(*refs, reverse, final, streams):
    brow_ref, bcol_ref, acol_ref, e1_ref = refs[:4]
    pos, per = 4, 3
    if final:
        d_ref, g_ref, yprev_ref = refs[4:7]
        pos, per = 7, 4
    y_ref, state_s = refs[pos + per * streams:]

    @pl.when(pl.program_id(1) == 0)
    def _():
        state_s[...] = jnp.zeros_like(state_s)

    for s in range(streams):
        r = refs[pos + per * s:pos + per * (s + 1)]
        if final:
            rest = (yprev_ref.at[s], r[3], d_ref, g_ref, y_ref.at[s], state_s.at[s])
        else:
            rest = (y_ref.at[s], state_s.at[s])
        _ssd_chunk(r[0], r[1], r[2], brow_ref, bcol_ref, acol_ref, e1_ref, *rest, reverse=reverse, final=final)


def _ssd_chunk(xbc_ref, misc_ref, dtt_ref, brow_ref, bcol_ref, acol_ref, e1_ref, *rest, reverse, final):
    if final:
        yprev_ref, z_ref, d_ref, g_ref, y_ref, state_s = rest
    else:
        y_ref, state_s = rest
    L = SSD_CHUNK

    li = lax.broadcasted_iota(jnp.int32, (L, L), 0)
    si = lax.broadcasted_iota(jnp.int32, (L, L), 1)
    lane = lax.broadcasted_iota(jnp.int32, (L, 128), 1)
    if reverse:
        cum_r = jnp.where(li >= si, 1.0, 0.0).astype(BF16)
        causal = li <= si
    else:
        cum_r = jnp.where(li <= si, 1.0, 0.0).astype(BF16)
        causal = li >= si

    dt_exp = _dot_sel_right(_softplus(misc_ref[...] + brow_ref[...]), e1_ref[...])
    dt_row = _softplus(dtt_ref[...] + bcol_ref[...])
    ac_row = _dot_sel_right(dt_row * (-jnp.exp(acol_ref[...])), cum_r)
    ac_col = [jnp.broadcast_to(ac_row[hd:hd + 1, :], (L, L)).T for hd in range(SSD_HEADS)]
    ac_exp = jnp.concatenate([jnp.where(lane < SSD_P, ac_col[2 * p], ac_col[2 * p + 1])
                              for p in range(SSD_HEADS // 2)], axis=1)

    end = 0 if reverse else L - 1
    tot_exp = ac_exp[end:end + 1, :]
    xbc = xbc_ref[...]
    xdt = xbc[:, :HALF] * dt_exp
    xw = (xdt * jnp.exp(tot_exp - ac_exp)).astype(BF16)
    xdt_b = xdt.astype(BF16)
    eac = jnp.exp(ac_exp)
    chunk_decay = jnp.exp(tot_exp)
    gw = SSD_HEADS // SSD_G * SSD_P
    for g in range(SSD_G):
        gs = slice(g * gw, (g + 1) * gw)
        b_t = xbc[:, HALF + g * SSD_N:HALF + (g + 1) * SSD_N].T.astype(BF16)
        c_g = xbc[:, HALF + SSD_G * SSD_N + g * SSD_N:HALF + SSD_G * SSD_N + (g + 1) * SSD_N].astype(BF16)
        cb = _dot(c_g, b_t)
        st = state_s[:, gs]
        y_off = _dot(c_g, st.astype(BF16)) * eac[:, gs]
        state_s[:, gs] = st * chunk_decay[:, gs] + _dot(b_t, xw[:, gs])
        y_pairs = []
        for pair in range(2):
            ha = 4 * g + 2 * pair
            xp = xdt_b[:, ha * SSD_P:ha * SSD_P + 128]
            ys = []
            for hd in (ha, ha + 1):
                seg = ac_col[hd] - ac_row[hd:hd + 1, :]
                m = (cb * jnp.exp(jnp.where(causal, seg, -jnp.inf))).astype(BF16)
                ys.append(_dot(m, xp))
            y_pairs.append(jnp.where(lane < SSD_P, ys[0], ys[1]))
        y_g = jnp.concatenate(y_pairs, axis=1) + y_off
        if final:
            y_all = yprev_ref[:, gs] + y_g + d_ref[:, gs] * xbc[:, gs]
            y_ref[:, gs] = _rms(y_all * _silu(z_ref[:, gs]), g_ref[:, gs]).astype(y_ref.dtype)
        else:
            y_ref[:, gs] = y_g


def _ssd_streams(n_batch):
    streams = 2 if n_batch % 2 == 0 else 1
    return streams, n_batch // streams


def _ssd_scan(xbc_act, proj, dt_t, dt_bias, a_log, direction, n_batch, final=None):
    reverse = direction == 1
    nlc, ncc = SEQ // SSD_CHUNK, CTX // SSD_CHUNK
    lat_chunks = n_batch * nlc
    streams, groups = _ssd_streams(n_batch)

    def pos(j):
        return jnp.where(j < ncc, ncc - 1 - j, 2 * ncc + nlc - 1 - j) if reverse else j

    def blk(s):
        def f(g, j):
            b, p = s * groups + g, pos(j)
            return jnp.where(p < ncc, lat_chunks + b * ncc + p, b * nlc + p - ncc)
        return f

    lane0 = MISC_DT + SSD_HEADS * direction
    b_row = jnp.zeros((1, 128), F32).at[0, lane0:lane0 + SSD_HEADS].set(dt_bias[direction])
    b_col, a_col = dt_bias[direction].reshape(SSD_HEADS, 1), a_log[direction].reshape(SSD_HEADS, 1)
    e1 = jnp.asarray((np.arange(128)[:, None] - lane0) == (np.arange(HALF)[None] // SSD_P), BF16)
    const = lambda g, j: (0, 0)
    y_spec = pl.BlockSpec((None, None, streams, SSD_CHUNK, HALF), lambda g, j: (g, pos(j), 0, 0, 0))
    in_specs = [pl.BlockSpec((1, 128), const), pl.BlockSpec((SSD_HEADS, 1), const),
                pl.BlockSpec((SSD_HEADS, 1), const), pl.BlockSpec((128, HALF), const)]
    args = [b_row, b_col, a_col, e1]
    if final is not None:
        y_prev, d_skip, norm_g = final
        in_specs += [pl.BlockSpec((1, HALF), const), pl.BlockSpec((1, HALF), const), y_spec]
        args += [jnp.repeat(d_skip, SSD_P).reshape(1, HALF), norm_g.reshape(1, HALF), y_prev]
    for s in range(streams):
        f = blk(s)
        in_specs += [pl.BlockSpec((SSD_CHUNK, SSD_XBC), lambda g, j, f=f: (f(g, j), 0)),
                     pl.BlockSpec((SSD_CHUNK, 128), lambda g, j, f=f: (f(g, j), OFF_MISC // 128)),
                     pl.BlockSpec((SSD_HEADS, SSD_CHUNK), lambda g, j, f=f: (direction, f(g, j)))]
        args += [xbc_act, proj, dt_t]
        if final is not None:
            in_specs.append(pl.BlockSpec((SSD_CHUNK, HALF), lambda g, j, f=f: (f(g, j), OFF_Z // HALF)))
            args.append(proj)
    return pl.pallas_call(
        functools.partial(_ssd_scan_kernel, reverse=reverse, final=final is not None, streams=streams),
        grid=(groups, ncc + nlc),
        in_specs=in_specs,
        out_specs=y_spec,
        out_shape=jax.ShapeDtypeStruct((groups, ncc + nlc, streams, SSD_CHUNK, HALF), F32 if final is None else BF16),
        scratch_shapes=[pltpu.VMEM((streams, SSD_N, HALF), F32)],
        compiler_params=_cparams(("arbitrary", "arbitrary")),
        name="ssd_scan_bwd" if reverse else "ssd_scan_fwd",
    )(*args)


def _permute_w_in(w):
    w = w.astype(BF16)
    o = np.cumsum((0, MLA_HEADS * MLA_QK, MLA_RANK, MLA_ROPE, HALF, SSD_XBC, 2 * SSD_HEADS, 3 * HALF, 3 * D))
    q_nope = [w[..., hd * MLA_QK:hd * MLA_QK + MLA_NOPE] for hd in range(MLA_HEADS)]
    q_rope = [w[..., hd * MLA_QK + MLA_NOPE:(hd + 1) * MLA_QK] for hd in range(MLA_HEADS)]
    parts = [w[..., o[7]:o[8]], w[..., o[6]:o[7]], w[..., o[4]:o[5]], *q_nope, w[..., o[3]:o[4]], *q_rope,
             w[..., o[1]:o[2]], w[..., o[2]:o[3]], w[..., o[5]:o[6]]]
    used = sum(p.shape[-1] for p in parts)
    parts.append(jnp.zeros(w.shape[:-1] + (N_PROJ - used,), BF16))
    return jnp.concatenate(parts, axis=-1)


def _rope_tables():
    t = jnp.arange(SEQ)
    pos = jnp.stack([t // GRID_W, t % GRID_W], axis=-1).astype(F32)
    n_freq = MLA_ROPE // 4
    inv_freq = ROPE_THETA ** (-jnp.arange(n_freq, dtype=F32) / n_freq)
    ang = pos[:, :, None] * inv_freq
    cos, sin = jnp.cos(ang), jnp.sin(ang)
    c64 = jnp.concatenate([cos[:, 0], cos[:, 0], cos[:, 1], cos[:, 1]], axis=-1)
    s64 = jnp.concatenate([-sin[:, 0], sin[:, 0], -sin[:, 1], sin[:, 1]], axis=-1)
    ident = 256
    cos_t = jnp.concatenate([jnp.tile(c64, (1, 2)), jnp.ones((ident, 128), F32)], axis=0)
    sin_t = jnp.concatenate([jnp.tile(s64, (1, 2)), jnp.zeros((ident, 128), F32)], axis=0)
    return cos_t, sin_t


def kernel(x, c, ctx, c_ctx, w_ada, b_ada, norm_g, ffn1_w_gate, ffn1_w_up, ffn1_w_down, ffn2_w_gate, ffn2_w_up, ffn2_w_down, w_in, mla_kv_norm_g, mla_w_uk, mla_w_uv, mla_q_norm_g, mla_k_norm_g, ssd_conv_w, ssd_conv_b, ssd_a_log, ssd_dt_bias, ssd_d, ssd_norm_g, na_q_norm_g, na_k_norm_g, na_rpb, w_branch, w_out):
    nb = x.shape[0]
    assert x.shape[1:] == (SEQ, D) and ctx.shape[1:] == (CTX, D) and nb < 16
    n_lat, n_ctx = nb * SEQ, nb * CTX
    n_tok = n_lat + n_ctx
    depth = w_ada.shape[0]

    cond =jnp.concatenate([c, c_ctx[None], jnp.zeros((16 - nb - 1, D), F32)], axis=0)
    m_all = _adaln(cond, w_ada, b_ada)
    cos_t, sin_t = _rope_tables()

    bf = lambda w: w.astype(BF16)
    ffn1 = (bf(ffn1_w_gate), bf(ffn1_w_up), bf(ffn1_w_down))
    ffn2 = (bf(ffn2_w_gate), bf(ffn2_w_up), bf(ffn2_w_down))
    w_in_p, w_uk, w_uv, w_br, w_o = _permute_w_in(w_in), bf(mla_w_uk), bf(mla_w_uv), bf(w_branch), bf(w_out)

    m4s = [m_all[l].reshape(16, N_MOD, 1, D) for l in range(depth)]
    h, u = _normmod(x.reshape(n_lat, D), ctx.reshape(n_ctx, D), norm_g[0, 0], m4s[0], base=0, n_batch=nb)
    for l in range(depth):
        need_ctx = l < depth - 1
        rows_out = n_tok if need_ctx else n_lat
        m4 = m4s[l]
        mod = dict(m4=m4, n_lat=n_lat, n_batch=nb)

        h, u = _ffn(u, h, n_tok, l, *ffn1, base=0, nxt=(norm_g[l, 1], m4, 3), **mod)
        proj = _in_proj(u, w_in_p, l)

        q, k, v = _mla_prep(proj, cos_t, sin_t, l, w_uk, w_uv, mla_kv_norm_g[l],
                            mla_q_norm_g[l], mla_k_norm_g[l], n_lat)
        mla = _mla_attn(q, k, v, nb)

        na, na_c = _na_attn(proj, na_q_norm_g[l], na_k_norm_g[l], _na_bias_table(na_rpb[l]), nb)

        xbc_act = _ssd_conv(proj, ssd_conv_w[l], ssd_conv_b[l], n_lat)
        dt_t = proj[:, OFF_MISC + MISC_DT:OFF_MISC + MISC_DT + 2 * SSD_HEADS].T
        y_f = _ssd_scan(xbc_act, proj, dt_t, ssd_dt_bias[l], ssd_a_log[l], 0, nb)
        ssd = _ssd_scan(xbc_act, proj, dt_t, ssd_dt_bias[l], ssd_a_log[l], 1, nb,
                        final=(y_f, ssd_d[l], ssd_norm_g[l]))

        if need_ctx:
            mla = jnp.concatenate([mla, _mla_ctx_attn(q, k, v, nb)], axis=0)
            na = jnp.concatenate([na, na_c], axis=0)
        h, u = _mix((mla, ssd, na), proj, l, w_br, w_o, h, rows_out, n_g=norm_g[l, 2], **mod)

        if need_ctx:
            h, u = _ffn(u, h, rows_out, l, *ffn2, base=6, nxt=(norm_g[l + 1, 0], m4s[l + 1], 0), **mod)
        else:
            h = _ffn(u, h, rows_out, l, *ffn2, base=6, **mod)

    return h[:n_lat].reshape(nb, SEQ, D)
```

```python
import functools
import math

import numpy as np
import jax
import jax.numpy as jnp
from jax import lax
from jax.experimental import pallas as pl
from jax.experimental.pallas import tpu as pltpu

F32 = jnp.float32
BF16 = jnp.bfloat16

D = 2048
SEQ = 2048
CTX = 256
DEPTH = 2
GRID_W = 64
GRID_H = SEQ // GRID_W
EPS = 1e-6
ROPE_THETA = 10000.0
N_MOD = 9
D_FF = 5632
HALF = D // 2

MLA_NOPE, MLA_ROPE, MLA_V = 128, 64, 128
MLA_QK = MLA_NOPE + MLA_ROPE
MLA_HEADS = 8
MLA_RANK = 512
MLA_PAD = 256

SSD_HEADS, SSD_P, SSD_G, SSD_N, SSD_CONV, SSD_CHUNK = 16, 64, 4, 128, 5, 128
SSD_XBC = HALF + 2 * SSD_G * SSD_N

NA_HEADS, NA_DH = 8, 128
NA_WIN_ROWS, NA_WIN_COLS = 8, 16
NA_QROWS, NA_QCOLS = 8, 16
NA_KROWS, NA_KCOLS = NA_QROWS + NA_WIN_ROWS - 1, NA_QCOLS + NA_WIN_COLS
NA_NRG, NA_NCB = GRID_H // NA_QROWS, GRID_W // NA_QCOLS
NEG_BIG = -1e30

OFF_GATE = 0
OFF_NA = 3 * D
OFF_XBC = OFF_NA + 3 * HALF
OFF_QNOPE = OFF_XBC + SSD_XBC
OFF_Z = OFF_QNOPE + HALF
OFF_QROPE = OFF_Z + HALF
OFF_CKV = OFF_QROPE + MLA_HEADS * MLA_ROPE
OFF_MISC = OFF_CKV + MLA_RANK
N_PROJ = 29 * 512
MISC_DT = MLA_ROPE

VMEM_LIMIT = 56 * 1024 * 1024


def _cparams(sem):
    return pltpu.CompilerParams(dimension_semantics=sem, vmem_limit_bytes=VMEM_LIMIT)


def _row_tile(rows, preferred):
    tm = preferred
    while rows % tm:
        tm //= 2
    return tm


def _dot(a, b):
    return jnp.dot(a, b, preferred_element_type=F32)


def _dot_nt(a, b):
    return lax.dot_general(a, b, (((1,), (1,)), ((), ())), preferred_element_type=F32)


def _split3(x):
    hi = x.astype(BF16)
    r1 = x - hi.astype(F32)
    mid = r1.astype(BF16)
    lo = (r1 - mid.astype(F32)).astype(BF16)
    return hi, mid, lo


def _dot_sel_right(x, m01):
    hi, mid, lo = _split3(x)
    return _dot(hi, m01) + _dot(mid, m01) + _dot(lo, m01)


def _dot_sel_left(m01, x):
    hi, mid, lo = _split3(x)
    return _dot(m01, hi) + _dot(m01, mid) + _dot(m01, lo)


def _silu(x):
    return x * jax.nn.sigmoid(x)


def _softplus(x):
    return jnp.maximum(x, 0.0) + jnp.log1p(jnp.exp(-jnp.abs(x)))


def _rms(x, g):
    return x * lax.rsqrt(jnp.mean(x * x, axis=-1, keepdims=True) + EPS) * g


def _ada_kernel(c_ref, w_ref, b_ref, o_ref):
    s = _silu(c_ref[...]).astype(BF16)
    o_ref[...] = _dot(s, w_ref[...].astype(BF16)) + b_ref[...]


def _adaln(cond, w_ada, b_ada):
    depth = w_ada.shape[0]
    n = N_MOD * D
    tn = 1024
    return pl.pallas_call(
        _ada_kernel,
        grid=(depth, n // tn),
        in_specs=[
            pl.BlockSpec((16, D), lambda l, j: (0, 0)),
            pl.BlockSpec((None, D, tn), lambda l, j: (l, 0, j)),
            pl.BlockSpec((None, 1, tn), lambda l, j: (l, 0, j)),
        ],
        out_specs=pl.BlockSpec((None, 16, tn), lambda l, j: (l, 0, j)),
        out_shape=jax.ShapeDtypeStruct((depth, 16, n), F32),
        compiler_params=_cparams(("arbitrary", "arbitrary")),
        name="adaln",
    )(cond, w_ada, b_ada.reshape(depth, 1, n))


def _mod_spec(tm, n_lat_tiles, n_batch, k):
    per = SEQ // tm

    def imap(i, *_):
        return (jnp.where(i < n_lat_tiles, i // per, n_batch), k, 0, 0)

    return pl.BlockSpec((None, None, 1, D), imap)


def _normmod_kernel(x_ref, c_ref, g_ref, shift_ref, scale_ref, h_ref, o_ref, *, n_lat_tiles):
    is_lat = pl.program_id(0) < n_lat_tiles
    h = jnp.where(is_lat, x_ref[...], c_ref[...])
    h_ref[...] = h
    y = _rms(h, g_ref[...])
    o_ref[...] = (y * (1.0 + scale_ref[...]) + shift_ref[...]).astype(o_ref.dtype)


def _normmod(x2d, ctx2d, g, m4, base, n_batch):
    tm = 256
    n_lat, n_ctx = x2d.shape[0], ctx2d.shape[0]
    nlt = n_lat // tm
    rows = n_lat + n_ctx
    row = pl.BlockSpec((tm, D), lambda i: (i, 0))
    return pl.pallas_call(
        functools.partial(_normmod_kernel, n_lat_tiles=nlt),
        grid=(rows // tm,),
        in_specs=[
            pl.BlockSpec((tm, D), lambda i: (jnp.minimum(i, nlt - 1), 0)),
            pl.BlockSpec((tm, D), lambda i: (jnp.maximum(i - nlt, 0), 0)),
            pl.BlockSpec((1, D), lambda i: (0, 0)),
            _mod_spec(tm, nlt, n_batch, base),
            _mod_spec(tm, nlt, n_batch, base + 1),
        ],
        out_specs=[row, row],
        out_shape=[jax.ShapeDtypeStruct((rows, D), F32), jax.ShapeDtypeStruct((rows, D), BF16)],
        compiler_params=_cparams(("arbitrary",)),
        name="normmod",
    )(x2d, ctx2d, g.reshape(1, D), m4, m4)


def _ffn_kernel(u_ref, wg_ref, wu_ref, wd_ref, h_ref, gate_ref, *rest, nf, with_next):
    if with_next:
        ng_ref, nshift_ref, nscale_ref, o_ref, un_ref, acc_ref = rest
    else:
        (o_ref,) = rest
        acc_ref = o_ref
    f = pl.program_id(1)

    @pl.when(f == 0)
    def _():
        acc_ref[...] = jnp.zeros_like(acc_ref)

    u = u_ref[...]
    g = _dot(u, wg_ref[...])
    up = _dot(u, wu_ref[...])
    act = (_silu(g) * up).astype(BF16)
    acc_ref[...] += _dot(act, wd_ref[...])

    @pl.when(f == nf - 1)
    def _():
        h_new = h_ref[...] + (0.5 * gate_ref[...]) * acc_ref[...]
        o_ref[...] = h_new
        if with_next:
            y = _rms(h_new, ng_ref[...])
            un_ref[...] = (y * (1.0 + nscale_ref[...]) + nshift_ref[...]).astype(un_ref.dtype)


def _ffn(u, h, rows, layer, wg, wu, wd, m4, base, n_lat, n_batch, nxt=None):
    tm, tf = _row_tile(rows, 512 if nxt is not None else 1024), 512
    nf = D_FF // tf
    nlt = n_lat // tm
    row_spec = pl.BlockSpec((tm, D), lambda i, f: (i, 0))
    in_specs = [
        row_spec,
        pl.BlockSpec((None, D, tf), lambda i, f: (layer, 0, f)),
        pl.BlockSpec((None, D, tf), lambda i, f: (layer, 0, f)),
        pl.BlockSpec((None, tf, D), lambda i, f: (layer, f, 0)),
        row_spec if nxt is not None else pl.BlockSpec((tm, D), lambda i, f: (i, 0), pipeline_mode=pl.Buffered(1)),
        _mod_spec(tm, nlt, n_batch, base + 2),
    ]
    args = [u, wg, wu, wd, h, m4]
    out_specs, out_shape = row_spec, jax.ShapeDtypeStruct((rows, D), F32)
    if nxt is not None:
        n_g, n_m4, n_base = nxt
        in_specs += [pl.BlockSpec((1, D), lambda i, f: (0, 0)),
                     _mod_spec(tm, nlt, n_batch, n_base), _mod_spec(tm, nlt, n_batch, n_base + 1)]
        args += [n_g.reshape(1, D), n_m4, n_m4]
        out_specs = [row_spec, row_spec]
        out_shape = [out_shape, jax.ShapeDtypeStruct((rows, D), BF16)]
    return pl.pallas_call(
        functools.partial(_ffn_kernel, nf=nf, with_next=nxt is not None),
        grid=(rows // tm, nf),
        in_specs=in_specs,
        out_specs=out_specs,
        out_shape=out_shape,
        scratch_shapes=[pltpu.VMEM((tm, D), F32)] if nxt is not None else [],
        compiler_params=_cparams(("arbitrary", "arbitrary")),
        name="ffn",
    )(*args)


def _mm_kernel(a_ref, w_ref, o_ref):
    o_ref[...] = _dot(a_ref[...], w_ref[...]).astype(o_ref.dtype)


def _in_proj(u, w, layer):
    rows = u.shape[0]
    tm, tn = _row_tile(rows, 2048), 512
    return pl.pallas_call(
        _mm_kernel,
        grid=(rows // tm, N_PROJ // tn),
        in_specs=[pl.BlockSpec((tm, D), lambda i, j: (i, 0)),
                  pl.BlockSpec((None, D, tn), lambda i, j: (layer, 0, j))],
        out_specs=pl.BlockSpec((tm, tn), lambda i, j: (i, j)),
        out_shape=jax.ShapeDtypeStruct((rows, N_PROJ), F32),
        compiler_params=_cparams(("arbitrary", "arbitrary")),
        name="in_proj",
    )(u, w)


def _mix_kernel(b0_ref, b1_ref, b2_ref, wb_ref, g0_ref, g1_ref, g2_ref, wo_ref, h_ref, gate_ref,
                ng_ref, nshift_ref, nscale_ref, o_ref, un_ref):
    acc = jax.nn.sigmoid(g0_ref[...]) * _dot(b0_ref[...], wb_ref[0])
    b1 = b1_ref[...].reshape(b0_ref.shape)
    acc = acc + jax.nn.sigmoid(g1_ref[...]) * _dot(b1, wb_ref[1])
    acc = acc + jax.nn.sigmoid(g2_ref[...]) * _dot(b2_ref[...], wb_ref[2])
    h_new = h_ref[...] + gate_ref[...] * _dot(acc.astype(BF16), wo_ref[...])
    o_ref[...] = h_new
    y = _rms(h_new, ng_ref[...])
    un_ref[...] = (y * (1.0 + nscale_ref[...]) + nshift_ref[...]).astype(un_ref.dtype)


def _mix(branches, proj, layer, w_branch, w_out, h, rows, m4, n_lat, n_batch, n_g):
    tm = 256
    nlt = n_lat // tm
    row = lambda i: (i, 0)
    once = pl.Buffered(1)
    cpt = tm // SSD_CHUNK
    per = SEQ // tm
    _, groups = _ssd_streams(n_batch)

    def ssd_map(i):
        b = jnp.where(i < nlt, i // per, i - nlt)
        pblk = jnp.where(i < nlt, CTX // tm + i % per, 0)
        return (b % groups, pblk, b // groups, 0, 0)

    return pl.pallas_call(
        _mix_kernel,
        grid=(rows // tm,),
        in_specs=[pl.BlockSpec((tm, HALF), row), pl.BlockSpec((None, cpt, None, SSD_CHUNK, HALF), ssd_map),
                  pl.BlockSpec((tm, HALF), row)]
        + [pl.BlockSpec((None, 3, HALF, D), lambda i: (layer, 0, 0, 0), pipeline_mode=once)]
        + [pl.BlockSpec((tm, D), functools.partial(lambda i, k: (i, k), k=k)) for k in range(3)]
        + [pl.BlockSpec((None, D, D), lambda i: (layer, 0, 0), pipeline_mode=once),
           pl.BlockSpec((tm, D), row),
           _mod_spec(tm, nlt, n_batch, 5),
           pl.BlockSpec((1, D), lambda i: (0, 0)),
           _mod_spec(tm, nlt, n_batch, 6), _mod_spec(tm, nlt, n_batch, 7)],
        out_specs=[pl.BlockSpec((tm, D), row), pl.BlockSpec((tm, D), row)],
        out_shape=[jax.ShapeDtypeStruct((rows, D), F32), jax.ShapeDtypeStruct((rows, D), BF16)],
        compiler_params=_cparams(("arbitrary",)),
        name="mix",
    )(*branches, w_branch, proj, proj, proj, w_out, h, m4, n_g.reshape(1, D), m4, m4)


def _rope128(y, cos, sin):
    lane = lax.broadcasted_iota(jnp.int32, y.shape, 1)
    partner = jnp.where((lane % 32) < 16, pltpu.roll(y, 112, 1), pltpu.roll(y, 16, 1))
    return y * cos + partner * sin


def _mla_prep_kernel(qn_ref, qr_ref, ckv_ref, misc_ref, cos_ref, sin_ref, wuk_ref, wuv_ref,
                     gkv_ref, gqn_ref, gqr_ref, gkn_ref, gkr_ref, q_ref, k_ref, v_ref, *, scale):
    cos, sin = cos_ref[...], sin_ref[...]
    lane = lax.broadcasted_iota(jnp.int32, cos.shape, 1)
    low = lane < MLA_ROPE

    ckv = _rms(ckv_ref[...], gkv_ref[...]).astype(BF16)
    k_nope = _dot(ckv, wuk_ref[...])
    v_ref[...] = _dot(ckv, wuv_ref[...]).astype(v_ref.dtype)

    misc = misc_ref[...]
    kr_sq = jnp.where(low, misc * misc, 0.0)
    kr_rot = jnp.where(low, _rope128(misc * gkr_ref[...], cos, sin), 0.0)
    kr_rot_hi = pltpu.roll(kr_rot, MLA_ROPE, 1)

    gqn, gqr, gkn = gqn_ref[...], gqr_ref[...], gkn_ref[...]
    for pair in range(MLA_HEADS // 2):
        qr = qr_ref[:, pair * 128:(pair + 1) * 128]
        qr2 = qr * qr
        rs = []
        for sub in (0, 1):
            hd = 2 * pair + sub
            qn = qn_ref[:, hd * 128:(hd + 1) * 128]
            q_sq = qn * qn + jnp.where(low if sub == 0 else ~low, qr2, 0.0)
            r = lax.rsqrt(jnp.sum(q_sq, axis=-1, keepdims=True) * (1.0 / MLA_QK) + EPS)
            rs.append(r)
            q_ref[:, hd * MLA_PAD:hd * MLA_PAD + 128] = (qn * r * gqn * scale).astype(q_ref.dtype)
            kn = k_nope[:, hd * 128:(hd + 1) * 128]
            rk = lax.rsqrt(jnp.sum(kn * kn + kr_sq, axis=-1, keepdims=True) * (1.0 / MLA_QK) + EPS)
            k_ref[:, hd * MLA_PAD:hd * MLA_PAD + 128] = (kn * rk * gkn).astype(k_ref.dtype)
            k_ref[:, hd * MLA_PAD + 128:(hd + 1) * MLA_PAD] = ((kr_rot if sub == 0 else kr_rot_hi) * rk).astype(k_ref.dtype)
        q_rot = _rope128(qr * jnp.where(low, rs[0], rs[1]) * gqr, cos, sin) * scale
        q_ref[:, (2 * pair) * MLA_PAD + 128:(2 * pair + 1) * MLA_PAD] = jnp.where(low, q_rot, 0.0).astype(q_ref.dtype)
        q_ref[:, (2 * pair + 1) * MLA_PAD + 128:(2 * pair + 2) * MLA_PAD] = jnp.where(low, 0.0, q_rot).astype(q_ref.dtype)


def _mla_prep(proj, cos_t, sin_t, layer, w_uk, w_uv, g_kv, g_q, g_k, n_lat):
    rows = proj.shape[0]
    tm = 256
    nlt = n_lat // tm
    per = SEQ // tm

    def tab_map(i):
        return (jnp.where(i < nlt, i % per, per), 0)

    def row(v):
        return v.reshape(1, -1)

    def rope_gain(g):
        return jnp.concatenate([g[MLA_NOPE:], g[MLA_NOPE:]]).reshape(1, 128)

    const = lambda i: (0, 0)
    return pl.pallas_call(
        functools.partial(_mla_prep_kernel, scale=MLA_QK ** -0.5 * math.log2(math.e)),
        grid=(rows // tm,),
        in_specs=[
            pl.BlockSpec((tm, HALF), lambda i: (i, OFF_QNOPE // HALF)),
            pl.BlockSpec((tm, 512), lambda i: (i, OFF_QROPE // 512)),
            pl.BlockSpec((tm, MLA_RANK), lambda i: (i, OFF_CKV // MLA_RANK)),
            pl.BlockSpec((tm, 128), lambda i: (i, OFF_MISC // 128)),
            pl.BlockSpec((tm, 128), tab_map),
            pl.BlockSpec((tm, 128), tab_map),
            pl.BlockSpec((None, MLA_RANK, HALF), lambda i: (layer, 0, 0)),
            pl.BlockSpec((None, MLA_RANK, HALF), lambda i: (layer, 0, 0)),
            pl.BlockSpec((1, MLA_RANK), const),
            pl.BlockSpec((1, 128), const),
            pl.BlockSpec((1, 128), const),
            pl.BlockSpec((1, 128), const),
            pl.BlockSpec((1, 128), const),
        ],
        out_specs=[
            pl.BlockSpec((tm, MLA_HEADS * MLA_PAD), lambda i: (i, 0)),
            pl.BlockSpec((tm, MLA_HEADS * MLA_PAD), lambda i: (i, 0)),
            pl.BlockSpec((tm, HALF), lambda i: (i, 0)),
        ],
        out_shape=[
            jax.ShapeDtypeStruct((rows, MLA_HEADS * MLA_PAD), BF16),
            jax.ShapeDtypeStruct((rows, MLA_HEADS * MLA_PAD), BF16),
            jax.ShapeDtypeStruct((rows, HALF), BF16),
        ],
        compiler_params=_cparams(("arbitrary",)),
        name="mla_prep",
    )(proj, proj, proj, proj, cos_t, sin_t, w_uk, w_uv, row(g_kv),
      row(g_q[:MLA_NOPE]), rope_gain(g_q), row(g_k[:MLA_NOPE]), rope_gain(g_k))


LOG2E = math.log2(math.e)


def _softmax2_pv(s1, s2, v1, v2):
    m = jnp.maximum(jnp.max(s1, axis=-1, keepdims=True), jnp.max(s2, axis=-1, keepdims=True))
    p1 = jnp.exp2(s1 - m)
    p2 = jnp.exp2(s2 - m)
    denom = jnp.sum(p1, axis=-1, keepdims=True) + jnp.sum(p2, axis=-1, keepdims=True)
    return (_dot(p1.astype(BF16), v1) + _dot(p2.astype(BF16), v2)) / denom


def _softmax_pv(s, v):
    p = jnp.exp2(s - jnp.max(s, axis=-1, keepdims=True))
    return _dot(p.astype(BF16), v) / jnp.sum(p, axis=-1, keepdims=True)


def _mla_attn_kernel(q_ref, kl_ref, kc_ref, vl_ref, vc_ref, o_ref, *, sub):
    kl, kc, vl, vc = kl_ref[...], kc_ref[...], vl_ref[...], vc_ref[...]
    rows = q_ref.shape[0] // sub
    for t in range(sub):
        q = q_ref[t * rows:(t + 1) * rows, :]
        o = _softmax2_pv(_dot_nt(q, kl), _dot_nt(q, kc), vl, vc)
        o_ref[t * rows:(t + 1) * rows, :] = o.astype(o_ref.dtype)


def _mla_attn(q, k, v, n_batch):
    tq = 1024
    nq = SEQ // tq
    lat_blocks = n_batch * SEQ // CTX
    return pl.pallas_call(
        functools.partial(_mla_attn_kernel, sub=4),
        grid=(n_batch, MLA_HEADS, nq),
        in_specs=[
            pl.BlockSpec((tq, MLA_PAD), lambda b, h, i: (b * nq + i, h)),
            pl.BlockSpec((SEQ, MLA_PAD), lambda b, h, i: (b, h)),
            pl.BlockSpec((CTX, MLA_PAD), lambda b, h, i: (lat_blocks + b, h)),
            pl.BlockSpec((SEQ, MLA_V), lambda b, h, i: (b, h)),
            pl.BlockSpec((CTX, MLA_V), lambda b, h, i: (lat_blocks + b, h)),
        ],
        out_specs=pl.BlockSpec((tq, MLA_V), lambda b, h, i: (b * nq + i, h)),
        out_shape=jax.ShapeDtypeStruct((n_batch * SEQ, HALF), BF16),
        compiler_params=_cparams(("arbitrary", "arbitrary", "arbitrary")),
        name="mla_attn",
    )(q, k, k, v, v)


def _ctx_attn_kernel(q_ref, k_ref, v_ref, o_ref):
    o_ref[...] = _softmax_pv(_dot_nt(q_ref[...], k_ref[...]), v_ref[...]).astype(o_ref.dtype)


def _mla_ctx_attn(q, k, v, n_batch):
    lat_blocks = n_batch * SEQ // CTX
    return pl.pallas_call(
        _ctx_attn_kernel,
        grid=(n_batch, MLA_HEADS),
        in_specs=[
            pl.BlockSpec((CTX, MLA_PAD), lambda b, h: (lat_blocks + b, h)),
            pl.BlockSpec((CTX, MLA_PAD), lambda b, h: (lat_blocks + b, h)),
            pl.BlockSpec((CTX, MLA_V), lambda b, h: (lat_blocks + b, h)),
        ],
        out_specs=pl.BlockSpec((CTX, MLA_V), lambda b, h: (b, h)),
        out_shape=jax.ShapeDtypeStruct((n_batch * CTX, HALF), BF16),
        compiler_params=_cparams(("arbitrary", "arbitrary")),
        name="mla_ctx_attn",
    )(q, k, v)


def _na_key_row0(rg):
    return int(np.clip(NA_QROWS * rg - NA_WIN_ROWS // 2, 0, GRID_H - NA_KROWS))


def _na_band0(n):
    return int(np.clip(NA_QCOLS * n - NA_WIN_COLS // 2, 0, GRID_W - NA_KCOLS))


def _na_offsets(q0, k0, nq, nk, win, extent):
    qpos = q0 + np.arange(nq)[:, None]
    kpos = k0 + np.arange(nk)[None, :]
    start = np.clip(qpos - win // 2, 0, extent - win)
    valid = (kpos >= start) & (kpos < start + win)
    return np.where(valid, kpos - qpos + win - 1, 2 * win - 1)


def _na_block_classes():
    rows = [_na_offsets(NA_QROWS * g, _na_key_row0(g), NA_QROWS, NA_KROWS, NA_WIN_ROWS, GRID_H) for g in range(NA_NRG)]
    cols = [_na_offsets(NA_QCOLS * n, _na_band0(n), NA_QCOLS, NA_KCOLS, NA_WIN_COLS, GRID_W) for n in range(NA_NCB)]

    def classes(pats):
        reps, ids = [], []
        for p in pats:
            for k, r in enumerate(reps):
                if np.array_equal(p, r):
                    ids.append(k)
                    break
            else:
                ids.append(len(reps))
                reps.append(p)
        return np.stack(reps), ids

    return classes(rows), classes(cols)


def _na_bias_table(rpb):
    (r_off, _), (c_off, _) = _na_block_classes()
    ext = jnp.pad(rpb * LOG2E, ((0, 0), (0, 1), (0, 1)), constant_values=NEG_BIG)
    r_sel = jnp.asarray(r_off[..., None] == np.arange(2 * NA_WIN_ROWS), F32)
    c_sel = jnp.asarray(c_off[..., None] == np.arange(2 * NA_WIN_COLS), F32)
    bias = jnp.einsum('gika,hab,njcb->hgnijkc', r_sel, ext, c_sel, precision=lax.Precision.HIGHEST)
    return bias.reshape(NA_HEADS, r_off.shape[0] * c_off.shape[0], NA_QROWS * NA_QCOLS, NA_KROWS * NA_KCOLS)


def _na_kernel(q_ref, k_ref, v_ref, qc_ref, kc_ref, vc_ref, gq_ref, gk_ref, bias_ref, o_ref, oc_ref,
               qn_s, kn_s, *, scale):
    gq, gk = gq_ref[...], gk_ref[...]
    qn_s[...] = _rms(q_ref[...], gq) * scale
    kn_s[...] = _rms(k_ref[...], gk)
    kc = _rms(kc_ref[...], gk).astype(BF16)
    vc = vc_ref[...].astype(BF16)
    (_, row_cls), (col_reps, col_cls) = _na_block_classes()
    for rg in range(NA_NRG):
        kr0 = _na_key_row0(rg)
        for n in range(NA_NCB):
            b0 = _na_band0(n)
            q_rows = [(NA_QROWS * rg + i) * GRID_W + NA_QCOLS * n for i in range(NA_QROWS)]
            k_rows = [(kr0 + kr) * GRID_W + b0 for kr in range(NA_KROWS)]
            qb = jnp.concatenate([qn_s[pl.ds(t, NA_QCOLS), :] for t in q_rows], axis=0).astype(BF16)
            kb = jnp.concatenate([kn_s[pl.ds(t, NA_KCOLS), :] for t in k_rows], axis=0).astype(BF16)
            vb = jnp.concatenate([v_ref[pl.ds(t, NA_KCOLS), :] for t in k_rows], axis=0).astype(BF16)
            s_win = _dot_nt(qb, kb) + bias_ref[row_cls[rg] * len(col_reps) + col_cls[n]]
            o = _softmax2_pv(s_win, _dot_nt(qb, kc), vb, vc).astype(o_ref.dtype)
            for i, t in enumerate(q_rows):
                o_ref[pl.ds(t, NA_QCOLS), :] = o[i * NA_QCOLS:(i + 1) * NA_QCOLS]
    qc = (_rms(qc_ref[...], gq) * scale).astype(BF16)
    oc_ref[...] = _softmax_pv(_dot_nt(qc, kc), vc).astype(oc_ref.dtype)


def _na_attn(proj, g_q, g_k, bias, n_batch):
    base = OFF_NA // NA_DH
    lat_blocks = n_batch * SEQ // CTX

    def lat(part):
        return pl.BlockSpec((SEQ, NA_DH), lambda h, b: (b, base + part * NA_HEADS + h))

    def cx(part):
        return pl.BlockSpec((CTX, NA_DH), lambda h, b: (lat_blocks + b, base + part * NA_HEADS + h))

    const = lambda h, b: (0, 0)
    nblk, nq, nk = bias.shape[1:]
    return pl.pallas_call(
        functools.partial(_na_kernel, scale=NA_DH ** -0.5 * LOG2E),
        grid=(NA_HEADS, n_batch),
        in_specs=[lat(0), lat(1), lat(2), cx(0), cx(1), cx(2),
                  pl.BlockSpec((1, NA_DH), const), pl.BlockSpec((1, NA_DH), const),
                  pl.BlockSpec((None, nblk, nq, nk), lambda h, b: (h, 0, 0, 0))],
        out_specs=[pl.BlockSpec((SEQ, NA_DH), lambda h, b: (b, h)),
                   pl.BlockSpec((CTX, NA_DH), lambda h, b: (b, h))],
        out_shape=[jax.ShapeDtypeStruct((n_batch * SEQ, HALF), BF16),
                   jax.ShapeDtypeStruct((n_batch * CTX, HALF), BF16)],
        scratch_shapes=[pltpu.VMEM((SEQ, NA_DH), F32), pltpu.VMEM((SEQ, NA_DH), F32)],
        compiler_params=_cparams(("arbitrary", "arbitrary")),
        name="na_attn",
    )(proj, proj, proj, proj, proj, proj, g_q.reshape(1, NA_DH), g_k.reshape(1, NA_DH), bias)


CONV_TM = 256
CONV_HALO = 8


def _conv_kernel(prev_ref, cur_ref, next_ref, w_ref, b_ref, o_ref, pad_s, *, n_lat_tiles):
    i = pl.program_id(0)
    per = SEQ // CONV_TM
    is_ctx = i >= n_lat_tiles
    at_start = is_ctx | (i % per == 0)
    at_end = is_ctx | (i % per == per - 1)
    pad_s[0:CONV_HALO, :] = jnp.where(at_start, 0.0, prev_ref[...])
    pad_s[CONV_HALO:CONV_HALO + CONV_TM, :] = cur_ref[...]
    pad_s[CONV_HALO + CONV_TM:, :] = jnp.where(at_end, 0.0, next_ref[...])
    acc = jnp.broadcast_to(b_ref[...], cur_ref.shape)
    for k in range(SSD_CONV):
        acc = acc + w_ref[k:k + 1, :] * pad_s[pl.ds(CONV_HALO - SSD_CONV // 2 + k, CONV_TM), :]
    o_ref[...] = _silu(acc)


def _ssd_conv(proj, conv_w, conv_b, n_lat):
    rows = proj.shape[0]
    tc = 1024
    nlt = n_lat // CONV_TM
    cb0 = OFF_XBC // tc
    hb = CONV_TM // CONV_HALO
    last = rows // CONV_HALO - 1
    return pl.pallas_call(
        functools.partial(_conv_kernel, n_lat_tiles=nlt),
        grid=(rows // CONV_TM, SSD_XBC // tc),
        in_specs=[
            pl.BlockSpec((CONV_HALO, tc), lambda i, j: (jnp.maximum(i * hb - 1, 0), cb0 + j)),
            pl.BlockSpec((CONV_TM, tc), lambda i, j: (i, cb0 + j)),
            pl.BlockSpec((CONV_HALO, tc), lambda i, j: (jnp.minimum((i + 1) * hb, last), cb0 + j)),
            pl.BlockSpec((SSD_CONV, tc), lambda i, j: (0, j)),
            pl.BlockSpec((1, tc), lambda i, j: (0, j)),
        ],
        out_specs=pl.BlockSpec((CONV_TM, tc), lambda i, j: (i, j)),
        out_shape=jax.ShapeDtypeStruct((rows, SSD_XBC), F32),
        scratch_shapes=[pltpu.VMEM((CONV_TM + 2 * CONV_HALO, tc), F32)],
        compiler_params=_cparams(("arbitrary", "arbitrary")),
        name="ssd_conv",
    )(proj, proj, proj, conv_w, conv_b.reshape(1, SSD_XBC))


def _ssd_scan_kernel(*refs, reverse, final, streams):
    brow_ref, bcol_ref, acol_ref, e1_ref = refs[:4]
    pos, per = 4, 3
    if final:
        d_ref, g_ref, yprev_ref = refs[4:7]
        pos, per = 7, 4
    y_ref, state_s = refs[pos + per * streams:]

    @pl.when(pl.program_id(1) == 0)
    def _():
        state_s[...] = jnp.zeros_like(state_s)

    for s in range(streams):
        r = refs[pos + per * s:pos + per * (s + 1)]
        if final:
            rest = (yprev_ref.at[s], r[3], d_ref, g_ref, y_ref.at[s], state_s.at[s])
        else:
            rest = (y_ref.at[s], state_s.at[s])
        _ssd_chunk(r[0], r[1], r[2], brow_ref, bcol_ref, acol_ref, e1_ref, *rest, reverse=reverse, final=final)


def _ssd_chunk(xbc_ref, misc_ref, dtt_ref, brow_ref, bcol_ref, acol_ref, e1_ref, *rest, reverse, final):
    if final:
        yprev_ref, z_ref, d_ref, g_ref, y_ref, state_s = rest
    else:
        y_ref, state_s = rest
    L = SSD_CHUNK

    li = lax.broadcasted_iota(jnp.int32, (L, L), 0)
    si = lax.broadcasted_iota(jnp.int32, (L, L), 1)
    lane = lax.broadcasted_iota(jnp.int32, (L, 128), 1)
    if reverse:
        cum_r = jnp.where(li >= si, 1.0, 0.0).astype(BF16)
        causal = li <= si
    else:
        cum_r = jnp.where(li <= si, 1.0, 0.0).astype(BF16)
        causal = li >= si

    dt_exp = _dot_sel_right(_softplus(misc_ref[...] + brow_ref[...]), e1_ref[...])
    dt_row = _softplus(dtt_ref[...] + bcol_ref[...])
    ac_row = _dot_sel_right(dt_row * (-jnp.exp(acol_ref[...])), cum_r)
    ac_col = [jnp.broadcast_to(ac_row[hd:hd + 1, :], (L, L)).T for hd in range(SSD_HEADS)]
    ac_exp = jnp.concatenate([jnp.where(lane < SSD_P, ac_col[2 * p], ac_col[2 * p + 1])
                              for p in range(SSD_HEADS // 2)], axis=1)

    end = 0 if reverse else L - 1
    tot_exp = ac_exp[end:end + 1, :]
    xbc = xbc_ref[...]
    xdt = xbc[:, :HALF] * dt_exp
    xw = (xdt * jnp.exp(tot_exp - ac_exp)).astype(BF16)
    xdt_b = xdt.astype(BF16)
    eac = jnp.exp(ac_exp)
    chunk_decay = jnp.exp(tot_exp)
    gw = SSD_HEADS // SSD_G * SSD_P
    for g in range(SSD_G):
        gs = slice(g * gw, (g + 1) * gw)
        b_t = xbc[:, HALF + g * SSD_N:HALF + (g + 1) * SSD_N].T.astype(BF16)
        c_g = xbc[:, HALF + SSD_G * SSD_N + g * SSD_N:HALF + SSD_G * SSD_N + (g + 1) * SSD_N].astype(BF16)
        cb = _dot(c_g, b_t)
        st = state_s[:, gs]
        y_off = _dot(c_g, st.astype(BF16)) * eac[:, gs]
        state_s[:, gs] = st * chunk_decay[:, gs] + _dot(b_t, xw[:, gs])
        y_pairs = []
        for pair in range(2):
            ha = 4 * g + 2 * pair
            xp = xdt_b[:, ha * SSD_P:ha * SSD_P + 128]
            ys = []
            for hd in (ha, ha + 1):
                seg = ac_col[hd] - ac_row[hd:hd + 1, :]
                m = (cb * jnp.exp(jnp.where(causal, seg, -jnp.inf))).astype(BF16)
                ys.append(_dot(m, xp))
            y_pairs.append(jnp.where(lane < SSD_P, ys[0], ys[1]))
        y_g = jnp.concatenate(y_pairs, axis=1) + y_off
        if final:
            y_all = yprev_ref[:, gs] + y_g + d_ref[:, gs] * xbc[:, gs]
            y_ref[:, gs] = _rms(y_all * _silu(z_ref[:, gs]), g_ref[:, gs]).astype(y_ref.dtype)
        else:
            y_ref[:, gs] = y_g


def _ssd_streams(n_batch):
    streams = 2 if n_batch % 2 == 0 else 1
    return streams, n_batch // streams


def _ssd_scan(xbc_act, proj, dt_t, dt_bias, a_log, direction, n_batch, final=None):
    reverse = direction == 1
    nlc, ncc = SEQ // SSD_CHUNK, CTX // SSD_CHUNK
    lat_chunks = n_batch * nlc
    streams, groups = _ssd_streams(n_batch)

    def pos(j):
        return jnp.where(j < ncc, ncc - 1 - j, 2 * ncc + nlc - 1 - j) if reverse else j

    def blk(s):
        def f(g, j):
            b, p = s * groups + g, pos(j)
            return jnp.where(p < ncc, lat_chunks + b * ncc + p, b * nlc + p - ncc)
        return f

    lane0 = MISC_DT + SSD_HEADS * direction
    b_row = jnp.zeros((1, 128), F32).at[0, lane0:lane0 + SSD_HEADS].set(dt_bias[direction])
    b_col, a_col = dt_bias[direction].reshape(SSD_HEADS, 1), a_log[direction].reshape(SSD_HEADS, 1)
    e1 = jnp.asarray((np.arange(128)[:, None] - lane0) == (np.arange(HALF)[None] // SSD_P), BF16)
    const = lambda g, j: (0, 0)
    y_spec = pl.BlockSpec((None, None, streams, SSD_CHUNK, HALF), lambda g, j: (g, pos(j), 0, 0, 0))
    in_specs = [pl.BlockSpec((1, 128), const), pl.BlockSpec((SSD_HEADS, 1), const),
                pl.BlockSpec((SSD_HEADS, 1), const), pl.BlockSpec((128, HALF), const)]
    args = [b_row, b_col, a_col, e1]
    if final is not None:
        y_prev, d_skip, norm_g = final
        in_specs += [pl.BlockSpec((1, HALF), const), pl.BlockSpec((1, HALF), const), y_spec]
        args += [jnp.repeat(d_skip, SSD_P).reshape(1, HALF), norm_g.reshape(1, HALF), y_prev]
    for s in range(streams):
        f = blk(s)
        in_specs += [pl.BlockSpec((SSD_CHUNK, SSD_XBC), lambda g, j, f=f: (f(g, j), 0)),
                     pl.BlockSpec((SSD_CHUNK, 128), lambda g, j, f=f: (f(g, j), OFF_MISC // 128)),
                     pl.BlockSpec((SSD_HEADS, SSD_CHUNK), lambda g, j, f=f: (direction, f(g, j)))]
        args += [xbc_act, proj, dt_t]
        if final is not None:
            in_specs.append(pl.BlockSpec((SSD_CHUNK, HALF), lambda g, j, f=f: (f(g, j), OFF_Z // HALF)))
            args.append(proj)
    return pl.pallas_call(
        functools.partial(_ssd_scan_kernel, reverse=reverse, final=final is not None, streams=streams),
        grid=(groups, ncc + nlc),
        in_specs=in_specs,
        out_specs=y_spec,
        out_shape=jax.ShapeDtypeStruct((groups, ncc + nlc, streams, SSD_CHUNK, HALF), F32 if final is None else BF16),
        scratch_shapes=[pltpu.VMEM((streams, SSD_N, HALF), F32)],
        compiler_params=_cparams(("arbitrary", "arbitrary")),
        name="ssd_scan_bwd" if reverse else "ssd_scan_fwd",
    )(*args)


def _permute_w_in(w):
    w = w.astype(BF16)
    o = np.cumsum((0, MLA_HEADS * MLA_QK, MLA_RANK, MLA_ROPE, HALF, SSD_XBC, 2 * SSD_HEADS, 3 * HALF, 3 * D))
    q_nope = [w[..., hd * MLA_QK:hd * MLA_QK + MLA_NOPE] for hd in range(MLA_HEADS)]
    q_rope = [w[..., hd * MLA_QK + MLA_NOPE:(hd + 1) * MLA_QK] for hd in range(MLA_HEADS)]
    parts = [w[..., o[7]:o[8]], w[..., o[6]:o[7]], w[..., o[4]:o[5]], *q_nope, w[..., o[3]:o[4]], *q_rope,
             w[..., o[1]:o[2]], w[..., o[2]:o[3]], w[..., o[5]:o[6]]]
    used = sum(p.shape[-1] for p in parts)
    parts.append(jnp.zeros(w.shape[:-1] + (N_PROJ - used,), BF16))
    return jnp.concatenate(parts, axis=-1)


def _rope_tables():
    t = jnp.arange(SEQ)
    pos = jnp.stack([t // GRID_W, t % GRID_W], axis=-1).astype(F32)
    n_freq = MLA_ROPE // 4
    inv_freq = ROPE_THETA ** (-jnp.arange(n_freq, dtype=F32) / n_freq)
    ang = pos[:, :, None] * inv_freq
    cos, sin = jnp.cos(ang), jnp.sin(ang)
    c64 = jnp.concatenate([cos[:, 0], cos[:, 0], cos[:, 1], cos[:, 1]], axis=-1)
    s64 = jnp.concatenate([-sin[:, 0], sin[:, 0], -sin[:, 1], sin[:, 1]], axis=-1)
    ident = 256
    cos_t = jnp.concatenate([jnp.tile(c64, (1, 2)), jnp.ones((ident, 128), F32)], axis=0)
    sin_t = jnp.concatenate([jnp.tile(s64, (1, 2)), jnp.zeros((ident, 128), F32)], axis=0)
    return cos_t, sin_t


def kernel(x, c, ctx, c_ctx, w_ada, b_ada, norm_g, ffn1_w_gate, ffn1_w_up, ffn1_w_down, ffn2_w_gate, ffn2_w_up, ffn2_w_down, w_in, mla_kv_norm_g, mla_w_uk, mla_w_uv, mla_q_norm_g, mla_k_norm_g, ssd_conv_w, ssd_conv_b, ssd_a_log, ssd_dt_bias, ssd_d, ssd_norm_g, na_q_norm_g, na_k_norm_g, na_rpb, w_branch, w_out):
    nb = x.shape[0]
    assert x.shape[1:] == (SEQ, D) and ctx.shape[1:] == (CTX, D) and nb < 16
    n_lat, n_ctx = nb * SEQ, nb * CTX
    n_tok = n_lat + n_ctx
    depth = w_ada.shape[0]

    cond =jnp.concatenate([c, c_ctx[None], jnp.zeros((16 - nb - 1, D), F32)], axis=0)
    m_all = _adaln(cond, w_ada, b_ada)
    cos_t, sin_t = _rope_tables()

    bf = lambda w: w.astype(BF16)
    ffn1 = (bf(ffn1_w_gate), bf(ffn1_w_up), bf(ffn1_w_down))
    ffn2 = (bf(ffn2_w_gate), bf(ffn2_w_up), bf(ffn2_w_down))
    w_in_p, w_uk, w_uv, w_br, w_o = _permute_w_in(w_in), bf(mla_w_uk), bf(mla_w_uv), bf(w_branch), bf(w_out)

    m4s = [m_all[l].reshape(16, N_MOD, 1, D) for l in range(depth)]
    h, u = _normmod(x.reshape(n_lat, D), ctx.reshape(n_ctx, D), norm_g[0, 0], m4s[0], base=0, n_batch=nb)
    for l in range(depth):
        need_ctx = l < depth - 1
        rows_out = n_tok if need_ctx else n_lat
        m4 = m4s[l]
        mod = dict(m4=m4, n_lat=n_lat, n_batch=nb)

        h, u = _ffn(u, h, n_tok, l, *ffn1, base=0, nxt=(norm_g[l, 1], m4, 3), **mod)
        proj = _in_proj(u, w_in_p, l)

        q, k, v = _mla_prep(proj, cos_t, sin_t, l, w_uk, w_uv, mla_kv_norm_g[l],
                            mla_q_norm_g[l], mla_k_norm_g[l], n_lat)
        mla = _mla_attn(q, k, v, nb)

        na, na_c = _na_attn(proj, na_q_norm_g[l], na_k_norm_g[l], _na_bias_table(na_rpb[l]), nb)

        xbc_act = _ssd_conv(proj, ssd_conv_w[l], ssd_conv_b[l], n_lat)
        dt_t = proj[:, OFF_MISC + MISC_DT:OFF_MISC + MISC_DT + 2 * SSD_HEADS].T
        y_f = _ssd_scan(xbc_act, proj, dt_t, ssd_dt_bias[l], ssd_a_log[l], 0, nb)
        ssd = _ssd_scan(xbc_act, proj, dt_t, ssd_dt_bias[l], ssd_a_log[l], 1, nb,
                        final=(y_f, ssd_d[l], ssd_norm_g[l]))

        if need_ctx:
            mla = jnp.concatenate([mla, _mla_ctx_attn(q, k, v, nb)], axis=0)
            na = jnp.concatenate([na, na_c], axis=0)
        h, u = _mix((mla, ssd, na), proj, l, w_br, w_o, h, rows_out, n_g=norm_g[l, 2], **mod)

        if need_ctx:
            h, u = _ffn(u, h, rows_out, l, *ffn2, base=6, nxt=(norm_g[l + 1, 0], m4s[l + 1], 0), **mod)
        else:
            h = _ffn(u, h, rows_out, l, *ffn2, base=6, **mod)

    return h[:n_lat].reshape(nb, SEQ, D)
```

```python
import functools
import math

import numpy as np
import jax
import jax.numpy as jnp
from jax import lax
from jax.experimental import pallas as pl
from jax.experimental.pallas import tpu as pltpu

F32 = jnp.float32
BF16 = jnp.bfloat16

D = 2048
SEQ = 2048
CTX = 256
DEPTH = 2
GRID_W = 64
GRID_H = SEQ // GRID_W
EPS = 1e-6
ROPE_THETA = 10000.0
N_MOD = 9
D_FF = 5632
HALF = D // 2

MLA_NOPE, MLA_ROPE, MLA_V = 128, 64, 128
MLA_QK = MLA_NOPE + MLA_ROPE
MLA_HEADS = 8
MLA_RANK = 512
MLA_PAD = 256

SSD_HEADS, SSD_P, SSD_G, SSD_N, SSD_CONV, SSD_CHUNK = 16, 64, 4, 128, 5, 128
SSD_XBC = HALF + 2 * SSD_G * SSD_N

NA_HEADS, NA_DH = 8, 128
NA_WIN_ROWS, NA_WIN_COLS = 8, 16
NA_QROWS, NA_QCOLS = 8, 16
NA_KROWS, NA_KCOLS = NA_QROWS + NA_WIN_ROWS - 1, NA_QCOLS + NA_WIN_COLS
NA_NRG, NA_NCB = GRID_H // NA_QROWS, GRID_W // NA_QCOLS
NEG_BIG = -1e30

OFF_GATE = 0
OFF_NA = 3 * D
OFF_XBC = OFF_NA + 3 * HALF
OFF_QNOPE = OFF_XBC + SSD_XBC
OFF_Z = OFF_QNOPE + HALF
OFF_QROPE = OFF_Z + HALF
OFF_CKV = OFF_QROPE + MLA_HEADS * MLA_ROPE
OFF_MISC = OFF_CKV + MLA_RANK
N_PROJ = 29 * 512
MISC_DT = MLA_ROPE

VMEM_LIMIT = 56 * 1024 * 1024


def _cparams(sem):
    return pltpu.CompilerParams(dimension_semantics=sem, vmem_limit_bytes=VMEM_LIMIT)


def _row_tile(rows, preferred):
    tm = preferred
    while rows % tm:
        tm //= 2
    return tm


def _dot(a, b):
    return jnp.dot(a, b, preferred_element_type=F32)


def _dot_nt(a, b):
    return lax.dot_general(a, b, (((1,), (1,)), ((), ())), preferred_element_type=F32)


def _split3(x):
    hi = x.astype(BF16)
    r1 = x - hi.astype(F32)
    mid = r1.astype(BF16)
    lo = (r1 - mid.astype(F32)).astype(BF16)
    return hi, mid, lo


def _dot_sel_right(x, m01):
    hi, mid, lo = _split3(x)
    return _dot(hi, m01) + _dot(mid, m01) + _dot(lo, m01)


def _dot_sel_left(m01, x):
    hi, mid, lo = _split3(x)
    return _dot(m01, hi) + _dot(m01, mid) + _dot(m01, lo)


def _silu(x):
    return x * jax.nn.sigmoid(x)


def _softplus(x):
    return jnp.maximum(x, 0.0) + jnp.log1p(jnp.exp(-jnp.abs(x)))


def _rms(x, g):
    return x * lax.rsqrt(jnp.mean(x * x, axis=-1, keepdims=True) + EPS) * g


def _ada_kernel(c_ref, w_ref, b_ref, o_ref):
    s = _silu(c_ref[...]).astype(BF16)
    o_ref[...] = _dot(s, w_ref[...].astype(BF16)) + b_ref[...]


def _adaln(cond, w_ada, b_ada):
    depth = w_ada.shape[0]
    n = N_MOD * D
    tn = 1024
    return pl.pallas_call(
        _ada_kernel,
        grid=(depth, n // tn),
        in_specs=[
            pl.BlockSpec((16, D), lambda l, j: (0, 0)),
            pl.BlockSpec((None, D, tn), lambda l, j: (l, 0, j)),
            pl.BlockSpec((None, 1, tn), lambda l, j: (l, 0, j)),
        ],
        out_specs=pl.BlockSpec((None, 16, tn), lambda l, j: (l, 0, j)),
        out_shape=jax.ShapeDtypeStruct((depth, 16, n), F32),
        compiler_params=_cparams(("arbitrary", "arbitrary")),
        name="adaln",
    )(cond, w_ada, b_ada.reshape(depth, 1, n))


def _mod_spec(tm, n_lat_tiles, n_batch, k):
    per = SEQ // tm

    def imap(i, *_):
        return (jnp.where(i < n_lat_tiles, i // per, n_batch), k, 0, 0)

    return pl.BlockSpec((None, None, 1, D), imap)


def _normmod_kernel(x_ref, c_ref, g_ref, shift_ref, scale_ref, h_ref, o_ref, *, n_lat_tiles):
    is_lat = pl.program_id(0) < n_lat_tiles
    h = jnp.where(is_lat, x_ref[...], c_ref[...])
    h_ref[...] = h
    y = _rms(h, g_ref[...])
    o_ref[...] = (y * (1.0 + scale_ref[...]) + shift_ref[...]).astype(o_ref.dtype)


def _normmod(x2d, ctx2d, g, m4, base, n_batch):
    tm = 256
    n_lat, n_ctx = x2d.shape[0], ctx2d.shape[0]
    nlt = n_lat // tm
    rows = n_lat + n_ctx
    row = pl.BlockSpec((tm, D), lambda i: (i, 0))
    return pl.pallas_call(
        functools.partial(_normmod_kernel, n_lat_tiles=nlt),
        grid=(rows // tm,),
        in_specs=[
            pl.BlockSpec((tm, D), lambda i: (jnp.minimum(i, nlt - 1), 0)),
            pl.BlockSpec((tm, D), lambda i: (jnp.maximum(i - nlt, 0), 0)),
            pl.BlockSpec((1, D), lambda i: (0, 0)),
            _mod_spec(tm, nlt, n_batch, base),
            _mod_spec(tm, nlt, n_batch, base + 1),
        ],
        out_specs=[row, row],
        out_shape=[jax.ShapeDtypeStruct((rows, D), F32), jax.ShapeDtypeStruct((rows, D), BF16)],
        compiler_params=_cparams(("arbitrary",)),
        name="normmod",
    )(x2d, ctx2d, g.reshape(1, D), m4, m4)


def _ffn_kernel(u_ref, wg_ref, wu_ref, wd_ref, h_ref, gate_ref, *rest, nf, with_next):
    if with_next:
        ng_ref, nshift_ref, nscale_ref, o_ref, un_ref, acc_ref = rest
    else:
        o_ref, acc_ref = rest
    f = pl.program_id(1)

    @pl.when(f == 0)
    def _():
        acc_ref[...] = jnp.zeros_like(acc_ref)

    u = u_ref[...]
    g = _dot(u, wg_ref[...])
    up = _dot(u, wu_ref[...])
    act = (_silu(g) * up).astype(BF16)
    acc_ref[...] += _dot(act, wd_ref[...])

    @pl.when(f == nf - 1)
    def _():
        h_new = h_ref[...] + (0.5 * gate_ref[...]) * acc_ref[...]
        o_ref[...] = h_new
        if with_next:
            y = _rms(h_new, ng_ref[...])
            un_ref[...] = (y * (1.0 + nscale_ref[...]) + nshift_ref[...]).astype(un_ref.dtype)


def _ffn(u, h, rows, layer, wg, wu, wd, m4, base, n_lat, n_batch, nxt=None):
    tm, tf = _row_tile(rows, 512), 512
    nf = D_FF // tf
    nlt = n_lat // tm
    row_spec = pl.BlockSpec((tm, D), lambda i, f: (i, 0))
    in_specs = [
        row_spec,
        pl.BlockSpec((None, D, tf), lambda i, f: (layer, 0, f)),
        pl.BlockSpec((None, D, tf), lambda i, f: (layer, 0, f)),
        pl.BlockSpec((None, tf, D), lambda i, f: (layer, f, 0)),
        row_spec,
        _mod_spec(tm, nlt, n_batch, base + 2),
    ]
    args = [u, wg, wu, wd, h, m4]
    out_specs, out_shape = row_spec, jax.ShapeDtypeStruct((rows, D), F32)
    if nxt is not None:
        n_g, n_m4, n_base = nxt
        in_specs += [pl.BlockSpec((1, D), lambda i, f: (0, 0)),
                     _mod_spec(tm, nlt, n_batch, n_base), _mod_spec(tm, nlt, n_batch, n_base + 1)]
        args += [n_g.reshape(1, D), n_m4, n_m4]
        out_specs = [row_spec, row_spec]
        out_shape = [out_shape, jax.ShapeDtypeStruct((rows, D), BF16)]
    return pl.pallas_call(
        functools.partial(_ffn_kernel, nf=nf, with_next=nxt is not None),
        grid=(rows // tm, nf),
        in_specs=in_specs,
        out_specs=out_specs,
        out_shape=out_shape,
        scratch_shapes=[pltpu.VMEM((tm, D), F32)],
        compiler_params=_cparams(("arbitrary", "arbitrary")),
        name="ffn",
    )(*args)


def _mm_kernel(a_ref, w_ref, o_ref):
    o_ref[...] = _dot(a_ref[...], w_ref[...]).astype(o_ref.dtype)


def _in_proj(u, w, layer):
    rows = u.shape[0]
    tm, tn = _row_tile(rows, 2048), 512
    return pl.pallas_call(
        _mm_kernel,
        grid=(rows // tm, N_PROJ // tn),
        in_specs=[pl.BlockSpec((tm, D), lambda i, j: (i, 0)),
                  pl.BlockSpec((None, D, tn), lambda i, j: (layer, 0, j))],
        out_specs=pl.BlockSpec((tm, tn), lambda i, j: (i, j)),
        out_shape=jax.ShapeDtypeStruct((rows, N_PROJ), F32),
        compiler_params=_cparams(("arbitrary", "arbitrary")),
        name="in_proj",
    )(u, w)


def _mix_kernel(b0_ref, b0c_ref, b1_ref, b2_ref, b2c_ref, wb_ref, g0_ref, g1_ref, g2_ref, wo_ref, h_ref, gate_ref,
                ng_ref, nshift_ref, nscale_ref, o_ref, un_ref, *, n_lat_tiles):
    is_lat = pl.program_id(0) < n_lat_tiles
    b0 = jnp.where(is_lat, b0_ref[...], b0c_ref[...])
    b2 = jnp.where(is_lat, b2_ref[...], b2c_ref[...])
    acc = jax.nn.sigmoid(g0_ref[...]) * _dot(b0, wb_ref[0])
    b1 = b1_ref[...].reshape(b0.shape)
    acc = acc + jax.nn.sigmoid(g1_ref[...]) * _dot(b1, wb_ref[1])
    acc = acc + jax.nn.sigmoid(g2_ref[...]) * _dot(b2, wb_ref[2])
    h_new = h_ref[...] + gate_ref[...] * _dot(acc.astype(BF16), wo_ref[...])
    o_ref[...] = h_new
    y = _rms(h_new, ng_ref[...])
    un_ref[...] = (y * (1.0 + nscale_ref[...]) + nshift_ref[...]).astype(un_ref.dtype)


def _mix(branches, proj, layer, w_branch, w_out, h, rows, m4, n_lat, n_batch, n_g):
    tm = 256
    nlt = n_lat // tm
    row = lambda i: (i, 0)
    once = pl.Buffered(1)
    cpt = tm // SSD_CHUNK
    per = SEQ // tm
    _, groups = _ssd_streams(n_batch)

    def ssd_map(i):
        b = jnp.where(i < nlt, i // per, i - nlt)
        pblk = jnp.where(i < nlt, CTX // tm + i % per, 0)
        return (b % groups, pblk, b // groups, 0, 0)

    (mla, mla_c), ssd, (na, na_c) = branches
    lat_rows = pl.BlockSpec((tm, HALF), lambda i: (jnp.minimum(i, nlt - 1), 0))
    ctx_rows = pl.BlockSpec((tm, HALF), lambda i: (jnp.maximum(i - nlt, 0), 0))
    return pl.pallas_call(
        functools.partial(_mix_kernel, n_lat_tiles=nlt),
        grid=(rows // tm,),
        in_specs=[lat_rows, ctx_rows, pl.BlockSpec((None, cpt, None, SSD_CHUNK, HALF), ssd_map), lat_rows, ctx_rows]
        + [pl.BlockSpec((None, 3, HALF, D), lambda i: (layer, 0, 0, 0), pipeline_mode=once)]
        + [pl.BlockSpec((tm, D), functools.partial(lambda i, k: (i, k), k=k)) for k in range(3)]
        + [pl.BlockSpec((None, D, D), lambda i: (layer, 0, 0), pipeline_mode=once),
           pl.BlockSpec((tm, D), row),
           _mod_spec(tm, nlt, n_batch, 5),
           pl.BlockSpec((1, D), lambda i: (0, 0)),
           _mod_spec(tm, nlt, n_batch, 6), _mod_spec(tm, nlt, n_batch, 7)],
        out_specs=[pl.BlockSpec((tm, D), row), pl.BlockSpec((tm, D), row)],
        out_shape=[jax.ShapeDtypeStruct((rows, D), F32), jax.ShapeDtypeStruct((rows, D), BF16)],
        compiler_params=_cparams(("arbitrary",)),
        name="mix",
    )(mla, mla_c, ssd, na, na_c, w_branch, proj, proj, proj, w_out, h, m4, n_g.reshape(1, D), m4, m4)


def _rope128(y, cos, sin):
    lane = lax.broadcasted_iota(jnp.int32, y.shape, 1)
    partner = jnp.where((lane % 32) < 16, pltpu.roll(y, 112, 1), pltpu.roll(y, 16, 1))
    return y * cos + partner * sin


def _mla_prep_kernel(qn_ref, qr_ref, ckv_ref, misc_ref, cos_ref, sin_ref, wuk_ref, wuv_ref,
                     gkv_ref, gqn_ref, gqr_ref, gkn_ref, gkr_ref, q_ref, k_ref, v_ref, *, scale):
    cos, sin = cos_ref[...], sin_ref[...]
    lane = lax.broadcasted_iota(jnp.int32, cos.shape, 1)
    low = lane < MLA_ROPE

    ckv = _rms(ckv_ref[...], gkv_ref[...]).astype(BF16)
    k_nope = _dot(ckv, wuk_ref[...])
    v_ref[...] = _dot(ckv, wuv_ref[...]).astype(v_ref.dtype)

    misc = misc_ref[...]
    kr_sq = jnp.where(low, misc * misc, 0.0)
    kr_rot = jnp.where(low, _rope128(misc * gkr_ref[...], cos, sin), 0.0)
    kr_rot_hi = pltpu.roll(kr_rot, MLA_ROPE, 1)

    gqn, gqr, gkn = gqn_ref[...], gqr_ref[...], gkn_ref[...]
    for pair in range(MLA_HEADS // 2):
        qr = qr_ref[:, pair * 128:(pair + 1) * 128]
        qr2 = qr * qr
        rs = []
        for sub in (0, 1):
            hd = 2 * pair + sub
            qn = qn_ref[:, hd * 128:(hd + 1) * 128]
            q_sq = qn * qn + jnp.where(low if sub == 0 else ~low, qr2, 0.0)
            r = lax.rsqrt(jnp.sum(q_sq, axis=-1, keepdims=True) * (1.0 / MLA_QK) + EPS)
            rs.append(r)
            q_ref[:, hd * MLA_PAD:hd * MLA_PAD + 128] = (qn * r * gqn * scale).astype(q_ref.dtype)
            kn = k_nope[:, hd * 128:(hd + 1) * 128]
            rk = lax.rsqrt(jnp.sum(kn * kn + kr_sq, axis=-1, keepdims=True) * (1.0 / MLA_QK) + EPS)
            k_ref[:, hd * MLA_PAD:hd * MLA_PAD + 128] = (kn * rk * gkn).astype(k_ref.dtype)
            k_ref[:, hd * MLA_PAD + 128:(hd + 1) * MLA_PAD] = ((kr_rot if sub == 0 else kr_rot_hi) * rk).astype(k_ref.dtype)
        q_rot = _rope128(qr * jnp.where(low, rs[0], rs[1]) * gqr, cos, sin) * scale
        q_ref[:, (2 * pair) * MLA_PAD + 128:(2 * pair + 1) * MLA_PAD] = jnp.where(low, q_rot, 0.0).astype(q_ref.dtype)
        q_ref[:, (2 * pair + 1) * MLA_PAD + 128:(2 * pair + 2) * MLA_PAD] = jnp.where(low, 0.0, q_rot).astype(q_ref.dtype)


def _mla_prep(proj, cos_t, sin_t, layer, w_uk, w_uv, g_kv, g_q, g_k, n_lat):
    rows = proj.shape[0]
    tm = 256
    nlt = n_lat // tm
    per = SEQ // tm

    def tab_map(i):
        return (jnp.where(i < nlt, i % per, per), 0)

    def row(v):
        return v.reshape(1, -1)

    def rope_gain(g):
        return jnp.concatenate([g[MLA_NOPE:], g[MLA_NOPE:]]).reshape(1, 128)

    const = lambda i: (0, 0)
    return pl.pallas_call(
        functools.partial(_mla_prep_kernel, scale=MLA_QK ** -0.5 * math.log2(math.e)),
        grid=(rows // tm,),
        in_specs=[
            pl.BlockSpec((tm, HALF), lambda i: (i, OFF_QNOPE // HALF)),
            pl.BlockSpec((tm, 512), lambda i: (i, OFF_QROPE // 512)),
            pl.BlockSpec((tm, MLA_RANK), lambda i: (i, OFF_CKV // MLA_RANK)),
            pl.BlockSpec((tm, 128), lambda i: (i, OFF_MISC // 128)),
            pl.BlockSpec((tm, 128), tab_map),
            pl.BlockSpec((tm, 128), tab_map),
            pl.BlockSpec((None, MLA_RANK, HALF), lambda i: (layer, 0, 0)),
            pl.BlockSpec((None, MLA_RANK, HALF), lambda i: (layer, 0, 0)),
            pl.BlockSpec((1, MLA_RANK), const),
            pl.BlockSpec((1, 128), const),
            pl.BlockSpec((1, 128), const),
            pl.BlockSpec((1, 128), const),
            pl.BlockSpec((1, 128), const),
        ],
        out_specs=[
            pl.BlockSpec((tm, MLA_HEADS * MLA_PAD), lambda i: (i, 0)),
            pl.BlockSpec((tm, MLA_HEADS * MLA_PAD), lambda i: (i, 0)),
            pl.BlockSpec((tm, HALF), lambda i: (i, 0)),
        ],
        out_shape=[
            jax.ShapeDtypeStruct((rows, MLA_HEADS * MLA_PAD), BF16),
            jax.ShapeDtypeStruct((rows, MLA_HEADS * MLA_PAD), BF16),
            jax.ShapeDtypeStruct((rows, HALF), BF16),
        ],
        compiler_params=_cparams(("arbitrary",)),
        name="mla_prep",
    )(proj, proj, proj, proj, cos_t, sin_t, w_uk, w_uv, row(g_kv),
      row(g_q[:MLA_NOPE]), rope_gain(g_q), row(g_k[:MLA_NOPE]), rope_gain(g_k))


LOG2E = math.log2(math.e)


def _softmax2_pv(s1, s2, v1, v2):
    m = jnp.maximum(jnp.max(s1, axis=-1, keepdims=True), jnp.max(s2, axis=-1, keepdims=True))
    p1 = jnp.exp2(s1 - m)
    p2 = jnp.exp2(s2 - m)
    denom = jnp.sum(p1, axis=-1, keepdims=True) + jnp.sum(p2, axis=-1, keepdims=True)
    return (_dot(p1.astype(BF16), v1) + _dot(p2.astype(BF16), v2)) / denom


def _softmax_pv(s, v):
    p = jnp.exp2(s - jnp.max(s, axis=-1, keepdims=True))
    return _dot(p.astype(BF16), v) / jnp.sum(p, axis=-1, keepdims=True)


def _mla_attn_kernel(q_ref, kl_ref, kc_ref, vl_ref, vc_ref, o_ref, *, sub):
    kl, kc, vl, vc = kl_ref[...], kc_ref[...], vl_ref[...], vc_ref[...]
    rows = q_ref.shape[0] // sub
    for t in range(sub):
        q = q_ref[t * rows:(t + 1) * rows, :]
        o = _softmax2_pv(_dot_nt(q, kl), _dot_nt(q, kc), vl, vc)
        o_ref[t * rows:(t + 1) * rows, :] = o.astype(o_ref.dtype)


def _mla_attn(q, k, v, n_batch):
    tq = 1024
    nq = SEQ // tq
    lat_blocks = n_batch * SEQ // CTX
    return pl.pallas_call(
        functools.partial(_mla_attn_kernel, sub=4),
        grid=(n_batch, MLA_HEADS, nq),
        in_specs=[
            pl.BlockSpec((tq, MLA_PAD), lambda b, h, i: (b * nq + i, h)),
            pl.BlockSpec((SEQ, MLA_PAD), lambda b, h, i: (b, h)),
            pl.BlockSpec((CTX, MLA_PAD), lambda b, h, i: (lat_blocks + b, h)),
            pl.BlockSpec((SEQ, MLA_V), lambda b, h, i: (b, h)),
            pl.BlockSpec((CTX, MLA_V), lambda b, h, i: (lat_blocks + b, h)),
        ],
        out_specs=pl.BlockSpec((tq, MLA_V), lambda b, h, i: (b * nq + i, h)),
        out_shape=jax.ShapeDtypeStruct((n_batch * SEQ, HALF), BF16),
        compiler_params=_cparams(("arbitrary", "arbitrary", "arbitrary")),
        name="mla_attn",
    )(q, k, k, v, v)


def _ctx_attn_kernel(q_ref, k_ref, v_ref, o_ref):
    o_ref[...] = _softmax_pv(_dot_nt(q_ref[...], k_ref[...]), v_ref[...]).astype(o_ref.dtype)


def _mla_ctx_attn(q, k, v, n_batch):
    lat_blocks = n_batch * SEQ // CTX
    return pl.pallas_call(
        _ctx_attn_kernel,
        grid=(n_batch, MLA_HEADS),
        in_specs=[
            pl.BlockSpec((CTX, MLA_PAD), lambda b, h: (lat_blocks + b, h)),
            pl.BlockSpec((CTX, MLA_PAD), lambda b, h: (lat_blocks + b, h)),
            pl.BlockSpec((CTX, MLA_V), lambda b, h: (lat_blocks + b, h)),
        ],
        out_specs=pl.BlockSpec((CTX, MLA_V), lambda b, h: (b, h)),
        out_shape=jax.ShapeDtypeStruct((n_batch * CTX, HALF), BF16),
        compiler_params=_cparams(("arbitrary", "arbitrary")),
        name="mla_ctx_attn",
    )(q, k, v)


def _na_key_row0(rg):
    return int(np.clip(NA_QROWS * rg - NA_WIN_ROWS // 2, 0, GRID_H - NA_KROWS))


def _na_band0(n):
    return int(np.clip(NA_QCOLS * n - NA_WIN_COLS // 2, 0, GRID_W - NA_KCOLS))


def _na_offsets(q0, k0, nq, nk, win, extent):
    qpos = q0 + np.arange(nq)[:, None]
    kpos = k0 + np.arange(nk)[None, :]
    start = np.clip(qpos - win // 2, 0, extent - win)
    valid = (kpos >= start) & (kpos < start + win)
    return np.where(valid, kpos - qpos + win - 1, 2 * win - 1)


def _na_block_classes():
    rows = [_na_offsets(NA_QROWS * g, _na_key_row0(g), NA_QROWS, NA_KROWS, NA_WIN_ROWS, GRID_H) for g in range(NA_NRG)]
    cols = [_na_offsets(NA_QCOLS * n, _na_band0(n), NA_QCOLS, NA_KCOLS, NA_WIN_COLS, GRID_W) for n in range(NA_NCB)]

    def classes(pats):
        reps, ids = [], []
        for p in pats:
            for k, r in enumerate(reps):
                if np.array_equal(p, r):
                    ids.append(k)
                    break
            else:
                ids.append(len(reps))
                reps.append(p)
        return np.stack(reps), ids

    return classes(rows), classes(cols)


def _na_bias_table(rpb):
    (r_off, _), (c_off, _) = _na_block_classes()
    ext = jnp.pad(rpb * LOG2E, ((0, 0), (0, 1), (0, 1)), constant_values=NEG_BIG)
    r_sel = jnp.asarray(r_off[..., None] == np.arange(2 * NA_WIN_ROWS), F32)
    c_sel = jnp.asarray(c_off[..., None] == np.arange(2 * NA_WIN_COLS), F32)
    bias = jnp.einsum('gika,hab,njcb->hgnijkc', r_sel, ext, c_sel, precision=lax.Precision.HIGHEST)
    return bias.reshape(NA_HEADS, r_off.shape[0] * c_off.shape[0], NA_QROWS * NA_QCOLS, NA_KROWS * NA_KCOLS)


def _na_kernel(q_ref, k_ref, v_ref, qc_ref, kc_ref, vc_ref, gq_ref, gk_ref, bias_ref, o_ref, oc_ref,
               qn_s, kn_s, *, scale):
    gq, gk = gq_ref[...], gk_ref[...]
    qn_s[...] = _rms(q_ref[...], gq) * scale
    kn_s[...] = _rms(k_ref[...], gk)
    kc = _rms(kc_ref[...], gk).astype(BF16)
    vc = vc_ref[...].astype(BF16)
    (_, row_cls), (col_reps, col_cls) = _na_block_classes()
    for rg in range(NA_NRG):
        kr0 = _na_key_row0(rg)
        for n in range(NA_NCB):
            b0 = _na_band0(n)
            q_rows = [(NA_QROWS * rg + i) * GRID_W + NA_QCOLS * n for i in range(NA_QROWS)]
            k_rows = [(kr0 + kr) * GRID_W + b0 for kr in range(NA_KROWS)]
            qb = jnp.concatenate([qn_s[pl.ds(t, NA_QCOLS), :] for t in q_rows], axis=0).astype(BF16)
            kb = jnp.concatenate([kn_s[pl.ds(t, NA_KCOLS), :] for t in k_rows], axis=0).astype(BF16)
            vb = jnp.concatenate([v_ref[pl.ds(t, NA_KCOLS), :] for t in k_rows], axis=0).astype(BF16)
            s_win = _dot_nt(qb, kb) + bias_ref[row_cls[rg] * len(col_reps) + col_cls[n]]
            o = _softmax2_pv(s_win, _dot_nt(qb, kc), vb, vc).astype(o_ref.dtype)
            for i, t in enumerate(q_rows):
                o_ref[pl.ds(t, NA_QCOLS), :] = o[i * NA_QCOLS:(i + 1) * NA_QCOLS]
    qc = (_rms(qc_ref[...], gq) * scale).astype(BF16)
    oc_ref[...] = _softmax_pv(_dot_nt(qc, kc), vc).astype(oc_ref.dtype)


def _na_attn(proj, g_q, g_k, bias, n_batch):
    base = OFF_NA // NA_DH
    lat_blocks = n_batch * SEQ // CTX

    def lat(part):
        return pl.BlockSpec((SEQ, NA_DH), lambda h, b: (b, base + part * NA_HEADS + h))

    def cx(part):
        return pl.BlockSpec((CTX, NA_DH), lambda h, b: (lat_blocks + b, base + part * NA_HEADS + h))

    const = lambda h, b: (0, 0)
    nblk, nq, nk = bias.shape[1:]
    return pl.pallas_call(
        functools.partial(_na_kernel, scale=NA_DH ** -0.5 * LOG2E),
        grid=(NA_HEADS, n_batch),
        in_specs=[lat(0), lat(1), lat(2), cx(0), cx(1), cx(2),
                  pl.BlockSpec((1, NA_DH), const), pl.BlockSpec((1, NA_DH), const),
                  pl.BlockSpec((None, nblk, nq, nk), lambda h, b: (h, 0, 0, 0))],
        out_specs=[pl.BlockSpec((SEQ, NA_DH), lambda h, b: (b, h)),
                   pl.BlockSpec((CTX, NA_DH), lambda h, b: (b, h))],
        out_shape=[jax.ShapeDtypeStruct((n_batch * SEQ, HALF), BF16),
                   jax.ShapeDtypeStruct((n_batch * CTX, HALF), BF16)],
        scratch_shapes=[pltpu.VMEM((SEQ, NA_DH), F32), pltpu.VMEM((SEQ, NA_DH), F32)],
        compiler_params=_cparams(("arbitrary", "arbitrary")),
        name="na_attn",
    )(proj, proj, proj, proj, proj, proj, g_q.reshape(1, NA_DH), g_k.reshape(1, NA_DH), bias)


CONV_TM = 256
CONV_HALO = 8


def _conv_kernel(prev_ref, cur_ref, next_ref, w_ref, b_ref, o_ref, pad_s, *, n_lat_tiles):
    i = pl.program_id(0)
    per = SEQ // CONV_TM
    is_ctx = i >= n_lat_tiles
    at_start = is_ctx | (i % per == 0)
    at_end = is_ctx | (i % per == per - 1)
    pad_s[0:CONV_HALO, :] = jnp.where(at_start, 0.0, prev_ref[...])
    pad_s[CONV_HALO:CONV_HALO + CONV_TM, :] = cur_ref[...]
    pad_s[CONV_HALO + CONV_TM:, :] = jnp.where(at_end, 0.0, next_ref[...])
    acc = jnp.broadcast_to(b_ref[...], cur_ref.shape)
    for k in range(SSD_CONV):
        acc = acc + w_ref[k:k + 1, :] * pad_s[pl.ds(CONV_HALO - SSD_CONV // 2 + k, CONV_TM), :]
    o_ref[...] = _silu(acc)


def _ssd_conv(proj, conv_w, conv_b, n_lat):
    rows = proj.shape[0]
    tc = 1024
    assert OFF_XBC % tc == 0 and SSD_XBC % tc == 0
    nlt = n_lat // CONV_TM
    cb0 = OFF_XBC // tc
    hb = CONV_TM // CONV_HALO
    last = rows // CONV_HALO - 1
    return pl.pallas_call(
        functools.partial(_conv_kernel, n_lat_tiles=nlt),
        grid=(rows // CONV_TM, SSD_XBC // tc),
        in_specs=[
            pl.BlockSpec((CONV_HALO, tc), lambda i, j: (jnp.maximum(i * hb - 1, 0), cb0 + j)),
            pl.BlockSpec((CONV_TM, tc), lambda i, j: (i, cb0 + j)),
            pl.BlockSpec((CONV_HALO, tc), lambda i, j: (jnp.minimum((i + 1) * hb, last), cb0 + j)),
            pl.BlockSpec((SSD_CONV, tc), lambda i, j: (0, j)),
            pl.BlockSpec((1, tc), lambda i, j: (0, j)),
        ],
        out_specs=pl.BlockSpec((CONV_TM, tc), lambda i, j: (i, j)),
        out_shape=jax.ShapeDtypeStruct((rows, SSD_XBC), F32),
        scratch_shapes=[pltpu.VMEM((CONV_TM + 2 * CONV_HALO, tc), F32)],
        compiler_params=_cparams(("arbitrary", "arbitrary")),
        name="ssd_conv",
    )(proj, proj, proj, conv_w, conv_b.reshape(1, SSD_XBC))


def _ssd_scan_kernel(*refs, reverse, final, streams):
    brow_ref, bcol_ref, acol_ref, e1_ref = refs[:4]
    pos, per = 4, 3
    if final:
        d_ref, g_ref, yprev_ref = refs[4:7]
        pos, per = 7, 4
    y_ref, state_s = refs[pos + per * streams:]

    @pl.when(pl.program_id(1) == 0)
    def _():
        state_s[...] = jnp.zeros_like(state_s)

    for s in range(streams):
        r = refs[pos + per * s:pos + per * (s + 1)]
        if final:
            rest = (yprev_ref.at[s], r[3], d_ref, g_ref, y_ref.at[s], state_s.at[s])
        else:
            rest = (y_ref.at[s], state_s.at[s])
        _ssd_chunk(r[0], r[1], r[2], brow_ref, bcol_ref, acol_ref, e1_ref, *rest, reverse=reverse, final=final)


def _ssd_chunk(xbc_ref, misc_ref, dtt_ref, brow_ref, bcol_ref, acol_ref, e1_ref, *rest, reverse, final):
    if final:
        yprev_ref, z_ref, d_ref, g_ref, y_ref, state_s = rest
    else:
        y_ref, state_s = rest
    L = SSD_CHUNK

    li = lax.broadcasted_iota(jnp.int32, (L, L), 0)
    si = lax.broadcasted_iota(jnp.int32, (L, L), 1)
    lane = lax.broadcasted_iota(jnp.int32, (L, 128), 1)
    if reverse:
        cum_r = jnp.where(li >= si, 1.0, 0.0).astype(BF16)
        causal = li <= si
    else:
        cum_r = jnp.where(li <= si, 1.0, 0.0).astype(BF16)
        causal = li >= si

    dt_exp = _dot_sel_right(_softplus(misc_ref[...] + brow_ref[...]), e1_ref[...])
    dt_row = _softplus(dtt_ref[...] + bcol_ref[...])
    ac_row = _dot_sel_right(dt_row * (-jnp.exp(acol_ref[...])), cum_r)
    ac_col = [jnp.broadcast_to(ac_row[hd:hd + 1, :], (L, L)).T for hd in range(SSD_HEADS)]
    ac_exp = jnp.concatenate([jnp.where(lane < SSD_P, ac_col[2 * p], ac_col[2 * p + 1])
                              for p in range(SSD_HEADS // 2)], axis=1)

    end = 0 if reverse else L - 1
    tot_exp = ac_exp[end:end + 1, :]
    xbc = xbc_ref[...]
    xdt = xbc[:, :HALF] * dt_exp
    xw = (xdt * jnp.exp(tot_exp - ac_exp)).astype(BF16)
    xdt_b = xdt.astype(BF16)
    eac = jnp.exp(ac_exp)
    chunk_decay = jnp.exp(tot_exp)
    gw = SSD_HEADS // SSD_G * SSD_P
    for g in range(SSD_G):
        gs = slice(g * gw, (g + 1) * gw)
        b_t = xbc[:, HALF + g * SSD_N:HALF + (g + 1) * SSD_N].T.astype(BF16)
        c_g = xbc[:, HALF + SSD_G * SSD_N + g * SSD_N:HALF + SSD_G * SSD_N + (g + 1) * SSD_N].astype(BF16)
        cb = _dot(c_g, b_t)
        st = state_s[:, gs]
        y_off = _dot(c_g, st.astype(BF16)) * eac[:, gs]
        state_s[:, gs] = st * chunk_decay[:, gs] + _dot(b_t, xw[:, gs])
        y_pairs = []
        for pair in range(2):
            ha = 4 * g + 2 * pair
            xp = xdt_b[:, ha * SSD_P:ha * SSD_P + 128]
            ys = []
            for hd in (ha, ha + 1):
                seg = ac_col[hd] - ac_row[hd:hd + 1, :]
                m = (cb * jnp.exp(jnp.where(causal, seg, -jnp.inf))).astype(BF16)
                ys.append(_dot(m, xp))
            y_pairs.append(jnp.where(lane < SSD_P, ys[0], ys[1]))
        y_g = jnp.concatenate(y_pairs, axis=1) + y_off
        if final:
            y_all = yprev_ref[:, gs] + y_g + d_ref[:, gs] * xbc[:, gs]
            y_ref[:, gs] = _rms(y_all * _silu(z_ref[:, gs]), g_ref[:, gs]).astype(y_ref.dtype)
        else:
            y_ref[:, gs] = y_g


def _ssd_streams(n_batch):
    streams = 2 if n_batch % 2 == 0 else 1
    return streams, n_batch // streams


def _ssd_scan(xbc_act, proj, dt_t, dt_bias, a_log, direction, n_batch, final=None):
    reverse = direction == 1
    nlc, ncc = SEQ // SSD_CHUNK, CTX // SSD_CHUNK
    lat_chunks = n_batch * nlc
    streams, groups = _ssd_streams(n_batch)

    def pos(j):
        return jnp.where(j < ncc, ncc - 1 - j, 2 * ncc + nlc - 1 - j) if reverse else j

    def blk(s):
        def f(g, j):
            b, p = s * groups + g, pos(j)
            return jnp.where(p < ncc, lat_chunks + b * ncc + p, b * nlc + p - ncc)
        return f

    lane0 = MISC_DT + SSD_HEADS * direction
    b_row = jnp.zeros((1, 128), F32).at[0, lane0:lane0 + SSD_HEADS].set(dt_bias[direction])
    b_col, a_col = dt_bias[direction].reshape(SSD_HEADS, 1), a_log[direction].reshape(SSD_HEADS, 1)
    e1 = jnp.asarray((np.arange(128)[:, None] - lane0) == (np.arange(HALF)[None] // SSD_P), BF16)
    const = lambda g, j: (0, 0)
    y_spec = pl.BlockSpec((None, None, streams, SSD_CHUNK, HALF), lambda g, j: (g, pos(j), 0, 0, 0))
    in_specs = [pl.BlockSpec((1, 128), const), pl.BlockSpec((SSD_HEADS, 1), const),
                pl.BlockSpec((SSD_HEADS, 1), const), pl.BlockSpec((128, HALF), const)]
    args = [b_row, b_col, a_col, e1]
    if final is not None:
        y_prev, d_skip, norm_g = final
        in_specs += [pl.BlockSpec((1, HALF), const), pl.BlockSpec((1, HALF), const), y_spec]
        args += [jnp.repeat(d_skip, SSD_P).reshape(1, HALF), norm_g.reshape(1, HALF), y_prev]
    for s in range(streams):
        f = blk(s)
        in_specs += [pl.BlockSpec((SSD_CHUNK, SSD_XBC), lambda g, j, f=f: (f(g, j), 0)),
                     pl.BlockSpec((SSD_CHUNK, 128), lambda g, j, f=f: (f(g, j), OFF_MISC // 128)),
                     pl.BlockSpec((SSD_HEADS, SSD_CHUNK), lambda g, j, f=f: (direction, f(g, j)))]
        args += [xbc_act, proj, dt_t]
        if final is not None:
            in_specs.append(pl.BlockSpec((SSD_CHUNK, HALF), lambda g, j, f=f: (f(g, j), OFF_Z // HALF)))
            args.append(proj)
    return pl.pallas_call(
        functools.partial(_ssd_scan_kernel, reverse=reverse, final=final is not None, streams=streams),
        grid=(groups, ncc + nlc),
        in_specs=in_specs,
        out_specs=y_spec,
        out_shape=jax.ShapeDtypeStruct((groups, ncc + nlc, streams, SSD_CHUNK, HALF), F32 if final is None else BF16),
        scratch_shapes=[pltpu.VMEM((streams, SSD_N, HALF), F32)],
        compiler_params=_cparams(("arbitrary", "arbitrary")),
        name="ssd_scan_bwd" if reverse else "ssd_scan_fwd",
    )(*args)


def _w_in_pieces():
    o = np.cumsum((0, MLA_HEADS * MLA_QK, MLA_RANK, MLA_ROPE, HALF, SSD_XBC, 2 * SSD_HEADS, 3 * HALF, 3 * D))
    seg = lambda k: (int(o[k]), int(o[k + 1] - o[k]))
    q_nope = [(hd * MLA_QK, MLA_NOPE) for hd in range(MLA_HEADS)]
    q_rope = [(hd * MLA_QK + MLA_NOPE, MLA_ROPE) for hd in range(MLA_HEADS)]
    return [seg(7), seg(6), seg(4), *q_nope, seg(3), *q_rope, seg(1), seg(2), seg(5)]


def _permute_kernel(w_ref, o_ref):
    w = w_ref[...]
    parts = [w[:, a:a + n] for a, n in _w_in_pieces()]
    used = sum(n for _, n in _w_in_pieces())
    parts.append(jnp.zeros((w.shape[0], N_PROJ - used), w.dtype))
    o_ref[...] = jnp.concatenate(parts, axis=1).astype(o_ref.dtype)


def _permute_w_in(w):
    depth, _, n_in = w.shape
    tr = 64
    return pl.pallas_call(
        _permute_kernel,
        grid=(depth, D // tr),
        in_specs=[pl.BlockSpec((None, tr, n_in), lambda l, i: (l, i, 0))],
        out_specs=pl.BlockSpec((None, tr, N_PROJ), lambda l, i: (l, i, 0)),
        out_shape=jax.ShapeDtypeStruct((depth, D, N_PROJ), BF16),
        compiler_params=_cparams(("arbitrary", "arbitrary")),
        name="permute_w_in",
    )(w)


def _rope_tables():
    t = jnp.arange(SEQ)
    pos = jnp.stack([t // GRID_W, t % GRID_W], axis=-1).astype(F32)
    n_freq = MLA_ROPE // 4
    inv_freq = ROPE_THETA ** (-jnp.arange(n_freq, dtype=F32) / n_freq)
    ang = pos[:, :, None] * inv_freq
    cos, sin = jnp.cos(ang), jnp.sin(ang)
    c64 = jnp.concatenate([cos[:, 0], cos[:, 0], cos[:, 1], cos[:, 1]], axis=-1)
    s64 = jnp.concatenate([-sin[:, 0], sin[:, 0], -sin[:, 1], sin[:, 1]], axis=-1)
    ident = 256
    cos_t = jnp.concatenate([jnp.tile(c64, (1, 2)), jnp.ones((ident, 128), F32)], axis=0)
    sin_t = jnp.concatenate([jnp.tile(s64, (1, 2)), jnp.zeros((ident, 128), F32)], axis=0)
    return cos_t, sin_t


def kernel(x, c, ctx, c_ctx, w_ada, b_ada, norm_g, ffn1_w_gate, ffn1_w_up, ffn1_w_down, ffn2_w_gate, ffn2_w_up, ffn2_w_down, w_in, mla_kv_norm_g, mla_w_uk, mla_w_uv, mla_q_norm_g, mla_k_norm_g, ssd_conv_w, ssd_conv_b, ssd_a_log, ssd_dt_bias, ssd_d, ssd_norm_g, na_q_norm_g, na_k_norm_g, na_rpb, w_branch, w_out):
    nb = x.shape[0]
    assert x.shape[1:] == (SEQ, D) and ctx.shape[1:] == (CTX, D) and nb < 16
    n_lat, n_ctx = nb * SEQ, nb * CTX
    n_tok = n_lat + n_ctx
    depth = w_ada.shape[0]

    cond =jnp.concatenate([c, c_ctx[None], jnp.zeros((16 - nb - 1, D), F32)], axis=0)
    m_all = _adaln(cond, w_ada, b_ada)
    cos_t, sin_t = _rope_tables()

    bf = lambda w: w.astype(BF16)
    ffn1 = (bf(ffn1_w_gate), bf(ffn1_w_up), bf(ffn1_w_down))
    ffn2 = (bf(ffn2_w_gate), bf(ffn2_w_up), bf(ffn2_w_down))
    w_in_p, w_uk, w_uv, w_br, w_o = _permute_w_in(w_in), bf(mla_w_uk), bf(mla_w_uv), bf(w_branch), bf(w_out)

    m4s = [m_all[l].reshape(16, N_MOD, 1, D) for l in range(depth)]
    h, u = _normmod(x.reshape(n_lat, D), ctx.reshape(n_ctx, D), norm_g[0, 0], m4s[0], base=0, n_batch=nb)
    for l in range(depth):
        need_ctx = l < depth - 1
        rows_out = n_tok if need_ctx else n_lat
        m4 = m4s[l]
        mod = dict(m4=m4, n_lat=n_lat, n_batch=nb)

        h, u = _ffn(u, h, n_tok, l, *ffn1, base=0, nxt=(norm_g[l, 1], m4, 3), **mod)
        proj = _in_proj(u, w_in_p, l)

        q, k, v = _mla_prep(proj, cos_t, sin_t, l, w_uk, w_uv, mla_kv_norm_g[l],
                            mla_q_norm_g[l], mla_k_norm_g[l], n_lat)
        mla = _mla_attn(q, k, v, nb)

        na, na_c = _na_attn(proj, na_q_norm_g[l], na_k_norm_g[l], _na_bias_table(na_rpb[l]), nb)

        xbc_act = _ssd_conv(proj, ssd_conv_w[l], ssd_conv_b[l], n_lat)
        dt_t = proj[:, OFF_MISC + MISC_DT:OFF_MISC + MISC_DT + 2 * SSD_HEADS].T
        y_f = _ssd_scan(xbc_act, proj, dt_t, ssd_dt_bias[l], ssd_a_log[l], 0, nb)
        ssd = _ssd_scan(xbc_act, proj, dt_t, ssd_dt_bias[l], ssd_a_log[l], 1, nb,
                        final=(y_f, ssd_d[l], ssd_norm_g[l]))

        mla_c = _mla_ctx_attn(q, k, v, nb) if need_ctx else na_c
        h, u = _mix(((mla, mla_c), ssd, (na, na_c)), proj, l, w_br, w_o, h, rows_out, n_g=norm_g[l, 2], **mod)

        if need_ctx:
            h, u = _ffn(u, h, rows_out, l, *ffn2, base=6, nxt=(norm_g[l + 1, 0], m4s[l + 1], 0), **mod)
        else:
            h = _ffn(u, h, rows_out, l, *ffn2, base=6, **mod)

    return h[:n_lat].reshape(nb, SEQ, D)
```

```python
import functools
import math

import numpy as np
import jax
import jax.numpy as jnp
from jax import lax
from jax.experimental import pallas as pl
from jax.experimental.pallas import tpu as pltpu

F32 = jnp.float32
BF16 = jnp.bfloat16

D = 2048
SEQ = 2048
CTX = 256
DEPTH = 2
GRID_W = 64
GRID_H = SEQ // GRID_W
EPS = 1e-6
ROPE_THETA = 10000.0
N_MOD = 9
D_FF = 5632
HALF = D // 2

MLA_NOPE, MLA_ROPE, MLA_V = 128, 64, 128
MLA_QK = MLA_NOPE + MLA_ROPE
MLA_HEADS = 8
MLA_RANK = 512
MLA_PAD = 256

SSD_HEADS, SSD_P, SSD_G, SSD_N, SSD_CONV, SSD_CHUNK = 16, 64, 4, 128, 5, 128
SSD_XBC = HALF + 2 * SSD_G * SSD_N

NA_HEADS, NA_DH = 8, 128
NA_WIN_ROWS, NA_WIN_COLS = 8, 16
NA_QROWS, NA_QCOLS = 8, 16
NA_KROWS, NA_KCOLS = NA_QROWS + NA_WIN_ROWS - 1, NA_QCOLS + NA_WIN_COLS
NA_NRG, NA_NCB = GRID_H // NA_QROWS, GRID_W // NA_QCOLS
NEG_BIG = -1e30

OFF_GATE = 0
OFF_NA = 3 * D
OFF_XBC = OFF_NA + 3 * HALF
OFF_QNOPE = OFF_XBC + SSD_XBC
OFF_Z = OFF_QNOPE + HALF
OFF_QROPE = OFF_Z + HALF
OFF_CKV = OFF_QROPE + MLA_HEADS * MLA_ROPE
OFF_MISC = OFF_CKV + MLA_RANK
N_PROJ = 29 * 512
MISC_DT = MLA_ROPE

VMEM_LIMIT = 56 * 1024 * 1024


def _cparams(sem):
    return pltpu.CompilerParams(dimension_semantics=sem, vmem_limit_bytes=VMEM_LIMIT)


def _row_tile(rows, preferred):
    tm = preferred
    while rows % tm:
        tm //= 2
    return tm


def _dot(a, b):
    return jnp.dot(a, b, preferred_element_type=F32)


def _dot_nt(a, b):
    return lax.dot_general(a, b, (((1,), (1,)), ((), ())), preferred_element_type=F32)


def _split3(x):
    hi = x.astype(BF16)
    r1 = x - hi.astype(F32)
    mid = r1.astype(BF16)
    lo = (r1 - mid.astype(F32)).astype(BF16)
    return hi, mid, lo


def _dot_sel_right(x, m01):
    hi, mid, lo = _split3(x)
    return _dot(hi, m01) + _dot(mid, m01) + _dot(lo, m01)


def _dot_sel_left(m01, x):
    hi, mid, lo = _split3(x)
    return _dot(m01, hi) + _dot(m01, mid) + _dot(m01, lo)


def _silu(x):
    return x * jax.nn.sigmoid(x)


def _softplus(x):
    return jnp.maximum(x, 0.0) + jnp.log1p(jnp.exp(-jnp.abs(x)))


def _rms(x, g):
    return x * lax.rsqrt(jnp.mean(x * x, axis=-1, keepdims=True) + EPS) * g


def _ada_kernel(c_ref, w_ref, b_ref, o_ref):
    s = _silu(c_ref[...]).astype(BF16)
    o_ref[...] = _dot(s, w_ref[...].astype(BF16)) + b_ref[...]


def _adaln(cond, w_ada, b_ada):
    depth = w_ada.shape[0]
    n = N_MOD * D
    tn = 1024
    return pl.pallas_call(
        _ada_kernel,
        grid=(depth, n // tn),
        in_specs=[
            pl.BlockSpec((16, D), lambda l, j: (0, 0)),
            pl.BlockSpec((None, D, tn), lambda l, j: (l, 0, j)),
            pl.BlockSpec((None, 1, tn), lambda l, j: (l, 0, j)),
        ],
        out_specs=pl.BlockSpec((None, 16, tn), lambda l, j: (l, 0, j)),
        out_shape=jax.ShapeDtypeStruct((depth, 16, n), F32),
        compiler_params=_cparams(("arbitrary", "arbitrary")),
        name="adaln",
    )(cond, w_ada, b_ada.reshape(depth, 1, n))


def _mod_spec(tm, n_lat_tiles, n_batch, k):
    per = SEQ // tm

    def imap(i, *_):
        return (jnp.where(i < n_lat_tiles, i // per, n_batch), k, 0, 0)

    return pl.BlockSpec((None, None, 1, D), imap)


def _normmod_kernel(x_ref, c_ref, g_ref, shift_ref, scale_ref, h_ref, o_ref, *, n_lat_tiles):
    is_lat = pl.program_id(0) < n_lat_tiles
    h = jnp.where(is_lat, x_ref[...], c_ref[...])
    h_ref[...] = h
    y = _rms(h, g_ref[...])
    o_ref[...] = (y * (1.0 + scale_ref[...]) + shift_ref[...]).astype(o_ref.dtype)


def _normmod(x2d, ctx2d, g, m4, base, n_batch):
    tm = 256
    n_lat, n_ctx = x2d.shape[0], ctx2d.shape[0]
    nlt = n_lat // tm
    rows = n_lat + n_ctx
    row = pl.BlockSpec((tm, D), lambda i: (i, 0))
    return pl.pallas_call(
        functools.partial(_normmod_kernel, n_lat_tiles=nlt),
        grid=(rows // tm,),
        in_specs=[
            pl.BlockSpec((tm, D), lambda i: (jnp.minimum(i, nlt - 1), 0)),
            pl.BlockSpec((tm, D), lambda i: (jnp.maximum(i - nlt, 0), 0)),
            pl.BlockSpec((1, D), lambda i: (0, 0)),
            _mod_spec(tm, nlt, n_batch, base),
            _mod_spec(tm, nlt, n_batch, base + 1),
        ],
        out_specs=[row, row],
        out_shape=[jax.ShapeDtypeStruct((rows, D), F32), jax.ShapeDtypeStruct((rows, D), BF16)],
        compiler_params=_cparams(("arbitrary",)),
        name="normmod",
    )(x2d, ctx2d, g.reshape(1, D), m4, m4)


def _ffn_kernel(u_ref, wg_ref, wu_ref, wd_ref, h_ref, gate_ref, *rest, nf, with_next):
    if with_next:
        ng_ref, nshift_ref, nscale_ref, o_ref, un_ref, acc_ref = rest
    else:
        o_ref, acc_ref = rest
    f = pl.program_id(1)

    @pl.when(f == 0)
    def _():
        acc_ref[...] = jnp.zeros_like(acc_ref)

    u = u_ref[...]
    g = _dot(u, wg_ref[...])
    up = _dot(u, wu_ref[...])
    act = (_silu(g) * up).astype(BF16)
    acc_ref[...] += _dot(act, wd_ref[...])

    @pl.when(f == nf - 1)
    def _():
        h_new = h_ref[...] + (0.5 * gate_ref[...]) * acc_ref[...]
        o_ref[...] = h_new
        if with_next:
            y = _rms(h_new, ng_ref[...])
            un_ref[...] = (y * (1.0 + nscale_ref[...]) + nshift_ref[...]).astype(un_ref.dtype)


def _ffn(u, h, rows, layer, wg, wu, wd, m4, base, n_lat, n_batch, nxt=None):
    tm, tf = _row_tile(rows, 512), 512
    nf = D_FF // tf
    nlt = n_lat // tm
    row_spec = pl.BlockSpec((tm, D), lambda i, f: (i, 0))
    in_specs = [
        row_spec,
        pl.BlockSpec((None, D, tf), lambda i, f: (layer, 0, f)),
        pl.BlockSpec((None, D, tf), lambda i, f: (layer, 0, f)),
        pl.BlockSpec((None, tf, D), lambda i, f: (layer, f, 0)),
        row_spec,
        _mod_spec(tm, nlt, n_batch, base + 2),
    ]
    args = [u, wg, wu, wd, h, m4]
    out_specs, out_shape = row_spec, jax.ShapeDtypeStruct((rows, D), F32)
    if nxt is not None:
        n_g, n_m4, n_base = nxt
        in_specs += [pl.BlockSpec((1, D), lambda i, f: (0, 0)),
                     _mod_spec(tm, nlt, n_batch, n_base), _mod_spec(tm, nlt, n_batch, n_base + 1)]
        args += [n_g.reshape(1, D), n_m4, n_m4]
        out_specs = [row_spec, row_spec]
        out_shape = [out_shape, jax.ShapeDtypeStruct((rows, D), BF16)]
    return pl.pallas_call(
        functools.partial(_ffn_kernel, nf=nf, with_next=nxt is not None),
        grid=(rows // tm, nf),
        in_specs=in_specs,
        out_specs=out_specs,
        out_shape=out_shape,
        scratch_shapes=[pltpu.VMEM((tm, D), F32)],
        compiler_params=_cparams(("arbitrary", "arbitrary")),
        name="ffn",
    )(*args)


def _mm_kernel(a_ref, w_ref, o_ref):
    o_ref[...] = _dot(a_ref[...], w_ref[...]).astype(o_ref.dtype)


def _in_proj(u, w, layer):
    rows = u.shape[0]
    tm, tn = _row_tile(rows, 2048), 512
    return pl.pallas_call(
        _mm_kernel,
        grid=(rows // tm, N_PROJ // tn),
        in_specs=[pl.BlockSpec((tm, D), lambda i, j: (i, 0)),
                  pl.BlockSpec((None, D, tn), lambda i, j: (layer, 0, j))],
        out_specs=pl.BlockSpec((tm, tn), lambda i, j: (i, j)),
        out_shape=jax.ShapeDtypeStruct((rows, N_PROJ), F32),
        compiler_params=_cparams(("arbitrary", "arbitrary")),
        name="in_proj",
    )(u, w)


def _mix_kernel(b0_ref, b0c_ref, b1_ref, b2_ref, b2c_ref, wb_ref, g0_ref, g1_ref, g2_ref, wo_ref, h_ref, gate_ref,
                ng_ref, nshift_ref, nscale_ref, o_ref, un_ref, *, n_lat_tiles):
    is_lat = pl.program_id(0) < n_lat_tiles
    b0 = jnp.where(is_lat, b0_ref[...], b0c_ref[...])
    b2 = jnp.where(is_lat, b2_ref[...], b2c_ref[...])
    acc = jax.nn.sigmoid(g0_ref[...]) * _dot(b0, wb_ref[0])
    b1 = b1_ref[...].reshape(b0.shape)
    acc = acc + jax.nn.sigmoid(g1_ref[...]) * _dot(b1, wb_ref[1])
    acc = acc + jax.nn.sigmoid(g2_ref[...]) * _dot(b2, wb_ref[2])
    h_new = h_ref[...] + gate_ref[...] * _dot(acc.astype(BF16), wo_ref[...])
    o_ref[...] = h_new
    y = _rms(h_new, ng_ref[...])
    un_ref[...] = (y * (1.0 + nscale_ref[...]) + nshift_ref[...]).astype(un_ref.dtype)


def _mix(branches, proj, layer, w_branch, w_out, h, rows, m4, n_lat, n_batch, n_g):
    tm = 256
    nlt = n_lat // tm
    row = lambda i: (i, 0)
    once = pl.Buffered(1)
    cpt = tm // SSD_CHUNK
    per = SEQ // tm
    _, groups = _ssd_streams(n_batch)

    def ssd_map(i):
        b = jnp.where(i < nlt, i // per, i - nlt)
        pblk = jnp.where(i < nlt, CTX // tm + i % per, 0)
        return (b % groups, pblk, b // groups, 0, 0)

    (mla, mla_c), ssd, (na, na_c) = branches
    lat_rows = pl.BlockSpec((tm, HALF), lambda i: (jnp.minimum(i, nlt - 1), 0))
    ctx_rows = pl.BlockSpec((tm, HALF), lambda i: (jnp.maximum(i - nlt, 0), 0))
    return pl.pallas_call(
        functools.partial(_mix_kernel, n_lat_tiles=nlt),
        grid=(rows // tm,),
        in_specs=[lat_rows, ctx_rows, pl.BlockSpec((None, cpt, None, SSD_CHUNK, HALF), ssd_map), lat_rows, ctx_rows]
        + [pl.BlockSpec((None, 3, HALF, D), lambda i: (layer, 0, 0, 0), pipeline_mode=once)]
        + [pl.BlockSpec((tm, D), functools.partial(lambda i, k: (i, k), k=k)) for k in range(3)]
        + [pl.BlockSpec((None, D, D), lambda i: (layer, 0, 0), pipeline_mode=once),
           pl.BlockSpec((tm, D), row),
           _mod_spec(tm, nlt, n_batch, 5),
           pl.BlockSpec((1, D), lambda i: (0, 0)),
           _mod_spec(tm, nlt, n_batch, 6), _mod_spec(tm, nlt, n_batch, 7)],
        out_specs=[pl.BlockSpec((tm, D), row), pl.BlockSpec((tm, D), row)],
        out_shape=[jax.ShapeDtypeStruct((rows, D), F32), jax.ShapeDtypeStruct((rows, D), BF16)],
        compiler_params=_cparams(("arbitrary",)),
        name="mix",
    )(mla, mla_c, ssd, na, na_c, w_branch, proj, proj, proj, w_out, h, m4, n_g.reshape(1, D), m4, m4)


def _rope128(y, cos, sin):
    lane = lax.broadcasted_iota(jnp.int32, y.shape, 1)
    partner = jnp.where((lane % 32) < 16, pltpu.roll(y, 112, 1), pltpu.roll(y, 16, 1))
    return y * cos + partner * sin


def _mla_prep_kernel(qn_ref, qr_ref, ckv_ref, misc_ref, cos_ref, sin_ref, wuk_ref, wuv_ref,
                     gkv_ref, gqn_ref, gqr_ref, gkn_ref, gkr_ref, q_ref, k_ref, v_ref, *, scale):
    cos, sin = cos_ref[...], sin_ref[...]
    lane = lax.broadcasted_iota(jnp.int32, cos.shape, 1)
    low = lane < MLA_ROPE

    ckv = _rms(ckv_ref[...], gkv_ref[...]).astype(BF16)
    k_nope = _dot(ckv, wuk_ref[...])
    v_ref[...] = _dot(ckv, wuv_ref[...]).astype(v_ref.dtype)

    misc = misc_ref[...]
    kr_sq = jnp.where(low, misc * misc, 0.0)
    kr_rot = jnp.where(low, _rope128(misc * gkr_ref[...], cos, sin), 0.0)
    kr_rot_hi = pltpu.roll(kr_rot, MLA_ROPE, 1)

    gqn, gqr, gkn = gqn_ref[...], gqr_ref[...], gkn_ref[...]
    for pair in range(MLA_HEADS // 2):
        qr = qr_ref[:, pair * 128:(pair + 1) * 128]
        qr2 = qr * qr
        rs = []
        for sub in (0, 1):
            hd = 2 * pair + sub
            qn = qn_ref[:, hd * 128:(hd + 1) * 128]
            q_sq = qn * qn + jnp.where(low if sub == 0 else ~low, qr2, 0.0)
            r = lax.rsqrt(jnp.sum(q_sq, axis=-1, keepdims=True) * (1.0 / MLA_QK) + EPS)
            rs.append(r)
            q_ref[:, hd * MLA_PAD:hd * MLA_PAD + 128] = (qn * r * gqn * scale).astype(q_ref.dtype)
            kn = k_nope[:, hd * 128:(hd + 1) * 128]
            rk = lax.rsqrt(jnp.sum(kn * kn + kr_sq, axis=-1, keepdims=True) * (1.0 / MLA_QK) + EPS)
            k_ref[:, hd * MLA_PAD:hd * MLA_PAD + 128] = (kn * rk * gkn).astype(k_ref.dtype)
            k_ref[:, hd * MLA_PAD + 128:(hd + 1) * MLA_PAD] = ((kr_rot if sub == 0 else kr_rot_hi) * rk).astype(k_ref.dtype)
        q_rot = _rope128(qr * jnp.where(low, rs[0], rs[1]) * gqr, cos, sin) * scale
        q_ref[:, (2 * pair) * MLA_PAD + 128:(2 * pair + 1) * MLA_PAD] = jnp.where(low, q_rot, 0.0).astype(q_ref.dtype)
        q_ref[:, (2 * pair + 1) * MLA_PAD + 128:(2 * pair + 2) * MLA_PAD] = jnp.where(low, 0.0, q_rot).astype(q_ref.dtype)


def _mla_prep(proj, cos_t, sin_t, layer, w_uk, w_uv, g_kv, g_q, g_k, n_lat):
    rows = proj.shape[0]
    tm = 256
    nlt = n_lat // tm
    per = SEQ // tm

    def tab_map(i):
        return (jnp.where(i < nlt, i % per, per), 0)

    def row(v):
        return v.reshape(1, -1)

    def rope_gain(g):
        return jnp.concatenate([g[MLA_NOPE:], g[MLA_NOPE:]]).reshape(1, 128)

    const = lambda i: (0, 0)
    return pl.pallas_call(
        functools.partial(_mla_prep_kernel, scale=MLA_QK ** -0.5 * math.log2(math.e)),
        grid=(rows // tm,),
        in_specs=[
            pl.BlockSpec((tm, HALF), lambda i: (i, OFF_QNOPE // HALF)),
            pl.BlockSpec((tm, 512), lambda i: (i, OFF_QROPE // 512)),
            pl.BlockSpec((tm, MLA_RANK), lambda i: (i, OFF_CKV // MLA_RANK)),
            pl.BlockSpec((tm, 128), lambda i: (i, OFF_MISC // 128)),
            pl.BlockSpec((tm, 128), tab_map),
            pl.BlockSpec((tm, 128), tab_map),
            pl.BlockSpec((None, MLA_RANK, HALF), lambda i: (layer, 0, 0)),
            pl.BlockSpec((None, MLA_RANK, HALF), lambda i: (layer, 0, 0)),
            pl.BlockSpec((1, MLA_RANK), const),
            pl.BlockSpec((1, 128), const),
            pl.BlockSpec((1, 128), const),
            pl.BlockSpec((1, 128), const),
            pl.BlockSpec((1, 128), const),
        ],
        out_specs=[
            pl.BlockSpec((tm, MLA_HEADS * MLA_PAD), lambda i: (i, 0)),
            pl.BlockSpec((tm, MLA_HEADS * MLA_PAD), lambda i: (i, 0)),
            pl.BlockSpec((tm, HALF), lambda i: (i, 0)),
        ],
        out_shape=[
            jax.ShapeDtypeStruct((rows, MLA_HEADS * MLA_PAD), BF16),
            jax.ShapeDtypeStruct((rows, MLA_HEADS * MLA_PAD), BF16),
            jax.ShapeDtypeStruct((rows, HALF), BF16),
        ],
        compiler_params=_cparams(("arbitrary",)),
        name="mla_prep",
    )(proj, proj, proj, proj, cos_t, sin_t, w_uk, w_uv, row(g_kv),
      row(g_q[:MLA_NOPE]), rope_gain(g_q), row(g_k[:MLA_NOPE]), rope_gain(g_k))


LOG2E = math.log2(math.e)


def _softmax2_pv(s1, s2, v1, v2):
    m = jnp.maximum(jnp.max(s1, axis=-1, keepdims=True), jnp.max(s2, axis=-1, keepdims=True))
    p1 = jnp.exp2(s1 - m)
    p2 = jnp.exp2(s2 - m)
    denom = jnp.sum(p1, axis=-1, keepdims=True) + jnp.sum(p2, axis=-1, keepdims=True)
    return (_dot(p1.astype(BF16), v1) + _dot(p2.astype(BF16), v2)) / denom


def _softmax_pv(s, v):
    p = jnp.exp2(s - jnp.max(s, axis=-1, keepdims=True))
    return _dot(p.astype(BF16), v) / jnp.sum(p, axis=-1, keepdims=True)


def _mla_attn_kernel(q_ref, kl_ref, kc_ref, vl_ref, vc_ref, o_ref, *, sub):
    kl, kc, vl, vc = kl_ref[...], kc_ref[...], vl_ref[...], vc_ref[...]
    rows = q_ref.shape[0] // sub
    for t in range(sub):
        q = q_ref[t * rows:(t + 1) * rows, :]
        o = _softmax2_pv(_dot_nt(q, kl), _dot_nt(q, kc), vl, vc)
        o_ref[t * rows:(t + 1) * rows, :] = o.astype(o_ref.dtype)


def _mla_attn(q, k, v, n_batch):
    tq = 1024
    nq = SEQ // tq
    lat_blocks = n_batch * SEQ // CTX
    return pl.pallas_call(
        functools.partial(_mla_attn_kernel, sub=4),
        grid=(n_batch, MLA_HEADS, nq),
        in_specs=[
            pl.BlockSpec((tq, MLA_PAD), lambda b, h, i: (b * nq + i, h)),
            pl.BlockSpec((SEQ, MLA_PAD), lambda b, h, i: (b, h)),
            pl.BlockSpec((CTX, MLA_PAD), lambda b, h, i: (lat_blocks + b, h)),
            pl.BlockSpec((SEQ, MLA_V), lambda b, h, i: (b, h)),
            pl.BlockSpec((CTX, MLA_V), lambda b, h, i: (lat_blocks + b, h)),
        ],
        out_specs=pl.BlockSpec((tq, MLA_V), lambda b, h, i: (b * nq + i, h)),
        out_shape=jax.ShapeDtypeStruct((n_batch * SEQ, HALF), BF16),
        compiler_params=_cparams(("arbitrary", "arbitrary", "arbitrary")),
        name="mla_attn",
    )(q, k, k, v, v)


def _ctx_attn_kernel(q_ref, k_ref, v_ref, o_ref):
    o_ref[...] = _softmax_pv(_dot_nt(q_ref[...], k_ref[...]), v_ref[...]).astype(o_ref.dtype)


def _mla_ctx_attn(q, k, v, n_batch):
    lat_blocks = n_batch * SEQ // CTX
    return pl.pallas_call(
        _ctx_attn_kernel,
        grid=(n_batch, MLA_HEADS),
        in_specs=[
            pl.BlockSpec((CTX, MLA_PAD), lambda b, h: (lat_blocks + b, h)),
            pl.BlockSpec((CTX, MLA_PAD), lambda b, h: (lat_blocks + b, h)),
            pl.BlockSpec((CTX, MLA_V), lambda b, h: (lat_blocks + b, h)),
        ],
        out_specs=pl.BlockSpec((CTX, MLA_V), lambda b, h: (b, h)),
        out_shape=jax.ShapeDtypeStruct((n_batch * CTX, HALF), BF16),
        compiler_params=_cparams(("arbitrary", "arbitrary")),
        name="mla_ctx_attn",
    )(q, k, v)


def _na_key_row0(rg):
    return int(np.clip(NA_QROWS * rg - NA_WIN_ROWS // 2, 0, GRID_H - NA_KROWS))


def _na_band0(n):
    return int(np.clip(NA_QCOLS * n - NA_WIN_COLS // 2, 0, GRID_W - NA_KCOLS))


def _na_offsets(q0, k0, nq, nk, win, extent):
    qpos = q0 + np.arange(nq)[:, None]
    kpos = k0 + np.arange(nk)[None, :]
    start = np.clip(qpos - win // 2, 0, extent - win)
    valid = (kpos >= start) & (kpos < start + win)
    return np.where(valid, kpos - qpos + win - 1, 2 * win - 1)


def _na_block_classes():
    rows = [_na_offsets(NA_QROWS * g, _na_key_row0(g), NA_QROWS, NA_KROWS, NA_WIN_ROWS, GRID_H) for g in range(NA_NRG)]
    cols = [_na_offsets(NA_QCOLS * n, _na_band0(n), NA_QCOLS, NA_KCOLS, NA_WIN_COLS, GRID_W) for n in range(NA_NCB)]

    def classes(pats):
        reps, ids = [], []
        for p in pats:
            for k, r in enumerate(reps):
                if np.array_equal(p, r):
                    ids.append(k)
                    break
            else:
                ids.append(len(reps))
                reps.append(p)
        return np.stack(reps), ids

    return classes(rows), classes(cols)


def _na_bias_table(rpb):
    (r_off, _), (c_off, _) = _na_block_classes()
    ext = jnp.pad(rpb * LOG2E, ((0, 0), (0, 1), (0, 1)), constant_values=NEG_BIG)
    r_sel = jnp.asarray(r_off[..., None] == np.arange(2 * NA_WIN_ROWS), F32)
    c_sel = jnp.asarray(c_off[..., None] == np.arange(2 * NA_WIN_COLS), F32)
    bias = jnp.einsum('gika,hab,njcb->hgnijkc', r_sel, ext, c_sel, precision=lax.Precision.HIGHEST)
    return bias.reshape(NA_HEADS, r_off.shape[0] * c_off.shape[0], NA_QROWS * NA_QCOLS, NA_KROWS * NA_KCOLS)


def _na_kernel(q_ref, k_ref, v_ref, qc_ref, kc_ref, vc_ref, gq_ref, gk_ref, bias_ref, o_ref, oc_ref,
               qn_s, kn_s, *, scale):
    gq, gk = gq_ref[...], gk_ref[...]
    qn_s[...] = _rms(q_ref[...], gq) * scale
    kn_s[...] = _rms(k_ref[...], gk)
    kc = _rms(kc_ref[...], gk).astype(BF16)
    vc = vc_ref[...].astype(BF16)
    (_, row_cls), (col_reps, col_cls) = _na_block_classes()
    for rg in range(NA_NRG):
        kr0 = _na_key_row0(rg)
        for n in range(NA_NCB):
            b0 = _na_band0(n)
            q_rows = [(NA_QROWS * rg + i) * GRID_W + NA_QCOLS * n for i in range(NA_QROWS)]
            k_rows = [(kr0 + kr) * GRID_W + b0 for kr in range(NA_KROWS)]
            qb = jnp.concatenate([qn_s[pl.ds(t, NA_QCOLS), :] for t in q_rows], axis=0).astype(BF16)
            kb = jnp.concatenate([kn_s[pl.ds(t, NA_KCOLS), :] for t in k_rows], axis=0).astype(BF16)
            vb = jnp.concatenate([v_ref[pl.ds(t, NA_KCOLS), :] for t in k_rows], axis=0).astype(BF16)
            s_win = _dot_nt(qb, kb) + bias_ref[row_cls[rg] * len(col_reps) + col_cls[n]]
            o = _softmax2_pv(s_win, _dot_nt(qb, kc), vb, vc).astype(o_ref.dtype)
            for i, t in enumerate(q_rows):
                o_ref[pl.ds(t, NA_QCOLS), :] = o[i * NA_QCOLS:(i + 1) * NA_QCOLS]
    qc = (_rms(qc_ref[...], gq) * scale).astype(BF16)
    oc_ref[...] = _softmax_pv(_dot_nt(qc, kc), vc).astype(oc_ref.dtype)


def _na_attn(proj, g_q, g_k, bias, n_batch):
    base = OFF_NA // NA_DH
    lat_blocks = n_batch * SEQ // CTX

    def lat(part):
        return pl.BlockSpec((SEQ, NA_DH), lambda h, b: (b, base + part * NA_HEADS + h))

    def cx(part):
        return pl.BlockSpec((CTX, NA_DH), lambda h, b: (lat_blocks + b, base + part * NA_HEADS + h))

    const = lambda h, b: (0, 0)
    nblk, nq, nk = bias.shape[1:]
    return pl.pallas_call(
        functools.partial(_na_kernel, scale=NA_DH ** -0.5 * LOG2E),
        grid=(NA_HEADS, n_batch),
        in_specs=[lat(0), lat(1), lat(2), cx(0), cx(1), cx(2),
                  pl.BlockSpec((1, NA_DH), const), pl.BlockSpec((1, NA_DH), const),
                  pl.BlockSpec((None, nblk, nq, nk), lambda h, b: (h, 0, 0, 0))],
        out_specs=[pl.BlockSpec((SEQ, NA_DH), lambda h, b: (b, h)),
                   pl.BlockSpec((CTX, NA_DH), lambda h, b: (b, h))],
        out_shape=[jax.ShapeDtypeStruct((n_batch * SEQ, HALF), BF16),
                   jax.ShapeDtypeStruct((n_batch * CTX, HALF), BF16)],
        scratch_shapes=[pltpu.VMEM((SEQ, NA_DH), F32), pltpu.VMEM((SEQ, NA_DH), F32)],
        compiler_params=_cparams(("arbitrary", "arbitrary")),
        name="na_attn",
    )(proj, proj, proj, proj, proj, proj, g_q.reshape(1, NA_DH), g_k.reshape(1, NA_DH), bias)


CONV_TM = 256
CONV_HALO = 8


def _conv_kernel(prev_ref, cur_ref, next_ref, w_ref, b_ref, o_ref, pad_s, *, n_lat_tiles):
    i = pl.program_id(0)
    per = SEQ // CONV_TM
    is_ctx = i >= n_lat_tiles
    at_start = is_ctx | (i % per == 0)
    at_end = is_ctx | (i % per == per - 1)
    pad_s[0:CONV_HALO, :] = jnp.where(at_start, 0.0, prev_ref[...])
    pad_s[CONV_HALO:CONV_HALO + CONV_TM, :] = cur_ref[...]
    pad_s[CONV_HALO + CONV_TM:, :] = jnp.where(at_end, 0.0, next_ref[...])
    acc = jnp.broadcast_to(b_ref[...], cur_ref.shape)
    for k in range(SSD_CONV):
        acc = acc + w_ref[k:k + 1, :] * pad_s[pl.ds(CONV_HALO - SSD_CONV // 2 + k, CONV_TM), :]
    o_ref[...] = _silu(acc)


def _ssd_conv(proj, conv_w, conv_b, n_lat):
    rows = proj.shape[0]
    tc = 1024
    assert OFF_XBC % tc == 0 and SSD_XBC % tc == 0
    nlt = n_lat // CONV_TM
    cb0 = OFF_XBC // tc
    hb = CONV_TM // CONV_HALO
    last = rows // CONV_HALO - 1
    return pl.pallas_call(
        functools.partial(_conv_kernel, n_lat_tiles=nlt),
        grid=(rows // CONV_TM, SSD_XBC // tc),
        in_specs=[
            pl.BlockSpec((CONV_HALO, tc), lambda i, j: (jnp.maximum(i * hb - 1, 0), cb0 + j)),
            pl.BlockSpec((CONV_TM, tc), lambda i, j: (i, cb0 + j)),
            pl.BlockSpec((CONV_HALO, tc), lambda i, j: (jnp.minimum((i + 1) * hb, last), cb0 + j)),
            pl.BlockSpec((SSD_CONV, tc), lambda i, j: (0, j)),
            pl.BlockSpec((1, tc), lambda i, j: (0, j)),
        ],
        out_specs=pl.BlockSpec((CONV_TM, tc), lambda i, j: (i, j)),
        out_shape=jax.ShapeDtypeStruct((rows, SSD_XBC), F32),
        scratch_shapes=[pltpu.VMEM((CONV_TM + 2 * CONV_HALO, tc), F32)],
        compiler_params=_cparams(("arbitrary", "arbitrary")),
        name="ssd_conv",
    )(proj, proj, proj, conv_w, conv_b.reshape(1, SSD_XBC))


def _ssd_scan_kernel(*refs, reverse, final, streams):
    brow_ref, bcol_ref, acol_ref, e1_ref = refs[:4]
    pos, per = 4, 3
    if final:
        d_ref, g_ref, yprev_ref = refs[4:7]
        pos, per = 7, 4
    y_ref, state_s = refs[pos + per * streams:]

    @pl.when(pl.program_id(1) == 0)
    def _():
        state_s[...] = jnp.zeros_like(state_s)

    for s in range(streams):
        r = refs[pos + per * s:pos + per * (s + 1)]
        if final:
            rest = (yprev_ref.at[s], r[3], d_ref, g_ref, y_ref.at[s], state_s.at[s])
        else:
            rest = (y_ref.at[s], state_s.at[s])
        _ssd_chunk(r[0], r[1], r[2], brow_ref, bcol_ref, acol_ref, e1_ref, *rest, reverse=reverse, final=final)


def _ssd_chunk(xbc_ref, misc_ref, dtt_ref, brow_ref, bcol_ref, acol_ref, e1_ref, *rest, reverse, final):
    if final:
        yprev_ref, z_ref, d_ref, g_ref, y_ref, state_s = rest
    else:
        y_ref, state_s = rest
    L = SSD_CHUNK

    li = lax.broadcasted_iota(jnp.int32, (L, L), 0)
    si = lax.broadcasted_iota(jnp.int32, (L, L), 1)
    lane = lax.broadcasted_iota(jnp.int32, (L, 128), 1)
    if reverse:
        cum_r = jnp.where(li >= si, 1.0, 0.0).astype(BF16)
        causal = li <= si
    else:
        cum_r = jnp.where(li <= si, 1.0, 0.0).astype(BF16)
        causal = li >= si

    dt_exp = _dot_sel_right(_softplus(misc_ref[...] + brow_ref[...]), e1_ref[...])
    dt_row = _softplus(dtt_ref[...] + bcol_ref[...])
    ac_row = _dot_sel_right(dt_row * (-jnp.exp(acol_ref[...])), cum_r)
    ac_col = [jnp.broadcast_to(ac_row[hd:hd + 1, :], (L, L)).T for hd in range(SSD_HEADS)]
    ac_exp = jnp.concatenate([jnp.where(lane < SSD_P, ac_col[2 * p], ac_col[2 * p + 1])
                              for p in range(SSD_HEADS // 2)], axis=1)

    end = 0 if reverse else L - 1
    tot_exp = ac_exp[end:end + 1, :]
    xbc = xbc_ref[...]
    xdt = xbc[:, :HALF] * dt_exp
    xw = (xdt * jnp.exp(tot_exp - ac_exp)).astype(BF16)
    xdt_b = xdt.astype(BF16)
    eac = jnp.exp(ac_exp)
    chunk_decay = jnp.exp(tot_exp)
    gw = SSD_HEADS // SSD_G * SSD_P
    for g in range(SSD_G):
        gs = slice(g * gw, (g + 1) * gw)
        b_t = xbc[:, HALF + g * SSD_N:HALF + (g + 1) * SSD_N].T.astype(BF16)
        c_g = xbc[:, HALF + SSD_G * SSD_N + g * SSD_N:HALF + SSD_G * SSD_N + (g + 1) * SSD_N].astype(BF16)
        cb = _dot(c_g, b_t)
        st = state_s[:, gs]
        y_off = _dot(c_g, st.astype(BF16)) * eac[:, gs]
        state_s[:, gs] = st * chunk_decay[:, gs] + _dot(b_t, xw[:, gs])
        y_pairs = []
        for pair in range(2):
            ha = 4 * g + 2 * pair
            xp = xdt_b[:, ha * SSD_P:ha * SSD_P + 128]
            ys = []
            for hd in (ha, ha + 1):
                seg = ac_col[hd] - ac_row[hd:hd + 1, :]
                m = (cb * jnp.exp(jnp.where(causal, seg, -jnp.inf))).astype(BF16)
                ys.append(_dot(m, xp))
            y_pairs.append(jnp.where(lane < SSD_P, ys[0], ys[1]))
        y_g = jnp.concatenate(y_pairs, axis=1) + y_off
        if final:
            y_all = yprev_ref[:, gs] + y_g + d_ref[:, gs] * xbc[:, gs]
            y_ref[:, gs] = _rms(y_all * _silu(z_ref[:, gs]), g_ref[:, gs]).astype(y_ref.dtype)
        else:
            y_ref[:, gs] = y_g


def _ssd_streams(n_batch):
    streams = 2 if n_batch % 2 == 0 else 1
    return streams, n_batch // streams


def _ssd_scan(xbc_act, proj, dt_t, dt_bias, a_log, direction, n_batch, final=None):
    reverse = direction == 1
    nlc, ncc = SEQ // SSD_CHUNK, CTX // SSD_CHUNK
    lat_chunks = n_batch * nlc
    streams, groups = _ssd_streams(n_batch)

    def pos(j):
        return jnp.where(j < ncc, ncc - 1 - j, 2 * ncc + nlc - 1 - j) if reverse else j

    def blk(s):
        def f(g, j):
            b, p = s * groups + g, pos(j)
            return jnp.where(p < ncc, lat_chunks + b * ncc + p, b * nlc + p - ncc)
        return f

    lane0 = MISC_DT + SSD_HEADS * direction
    b_row = jnp.zeros((1, 128), F32).at[0, lane0:lane0 + SSD_HEADS].set(dt_bias[direction])
    b_col, a_col = dt_bias[direction].reshape(SSD_HEADS, 1), a_log[direction].reshape(SSD_HEADS, 1)
    e1 = jnp.asarray((np.arange(128)[:, None] - lane0) == (np.arange(HALF)[None] // SSD_P), BF16)
    const = lambda g, j: (0, 0)
    y_spec = pl.BlockSpec((None, None, streams, SSD_CHUNK, HALF), lambda g, j: (g, pos(j), 0, 0, 0))
    in_specs = [pl.BlockSpec((1, 128), const), pl.BlockSpec((SSD_HEADS, 1), const),
                pl.BlockSpec((SSD_HEADS, 1), const), pl.BlockSpec((128, HALF), const)]
    args = [b_row, b_col, a_col, e1]
    if final is not None:
        y_prev, d_skip, norm_g = final
        in_specs += [pl.BlockSpec((1, HALF), const), pl.BlockSpec((1, HALF), const), y_spec]
        args += [jnp.repeat(d_skip, SSD_P).reshape(1, HALF), norm_g.reshape(1, HALF), y_prev]
    for s in range(streams):
        f = blk(s)
        in_specs += [pl.BlockSpec((SSD_CHUNK, SSD_XBC), lambda g, j, f=f: (f(g, j), 0)),
                     pl.BlockSpec((SSD_CHUNK, 128), lambda g, j, f=f: (f(g, j), OFF_MISC // 128)),
                     pl.BlockSpec((SSD_HEADS, SSD_CHUNK), lambda g, j, f=f: (direction, f(g, j)))]
        args += [xbc_act, proj, dt_t]
        if final is not None:
            in_specs.append(pl.BlockSpec((SSD_CHUNK, HALF), lambda g, j, f=f: (f(g, j), OFF_Z // HALF)))
            args.append(proj)
    return pl.pallas_call(
        functools.partial(_ssd_scan_kernel, reverse=reverse, final=final is not None, streams=streams),
        grid=(groups, ncc + nlc),
        in_specs=in_specs,
        out_specs=y_spec,
        out_shape=jax.ShapeDtypeStruct((groups, ncc + nlc, streams, SSD_CHUNK, HALF), F32 if final is None else BF16),
        scratch_shapes=[pltpu.VMEM((streams, SSD_N, HALF), F32)],
        compiler_params=_cparams(("arbitrary", "arbitrary")),
        name="ssd_scan_bwd" if reverse else "ssd_scan_fwd",
    )(*args)


def _w_in_pieces():
    o = np.cumsum((0, MLA_HEADS * MLA_QK, MLA_RANK, MLA_ROPE, HALF, SSD_XBC, 2 * SSD_HEADS, 3 * HALF, 3 * D))
    seg = lambda k: (int(o[k]), int(o[k + 1] - o[k]))
    q_nope = [(hd * MLA_QK, MLA_NOPE) for hd in range(MLA_HEADS)]
    q_rope = [(hd * MLA_QK + MLA_NOPE, MLA_ROPE) for hd in range(MLA_HEADS)]
    return [seg(7), seg(6), seg(4), *q_nope, seg(3), *q_rope, seg(1), seg(2), seg(5)]


def _permute_kernel(wt_ref, o_ref, rows_s):
    dst = 0
    for a, n in _w_in_pieces():
        rows_s[dst:dst + n, :] = wt_ref[a:a + n, :]
        dst += n
    rows_s[dst:, :] = jnp.zeros((rows_s.shape[0] - dst, rows_s.shape[1]), rows_s.dtype)
    for j in range(N_PROJ // 128):
        cols = slice(j * 128, (j + 1) * 128)
        if j * 128 < rows_s.shape[0]:
            o_ref[:, cols] = rows_s[cols, :].T.astype(o_ref.dtype)
        else:
            o_ref[:, cols] = jnp.zeros((o_ref.shape[0], 128), o_ref.dtype)


def _permute_w_in(w):
    depth, _, n_in = w.shape
    tk = 128
    used = OFF_MISC + 128
    assert sum(n for _, n in _w_in_pieces()) == n_in <= used
    return pl.pallas_call(
        _permute_kernel,
        grid=(depth, D // tk),
        in_specs=[pl.BlockSpec((None, n_in, tk), lambda l, i: (l, 0, i))],
        out_specs=pl.BlockSpec((None, tk, N_PROJ), lambda l, i: (l, i, 0)),
        out_shape=jax.ShapeDtypeStruct((depth, D, N_PROJ), BF16),
        scratch_shapes=[pltpu.VMEM((used, tk), F32)],
        compiler_params=_cparams(("arbitrary", "arbitrary")),
        name="permute_w_in",
    )(jnp.swapaxes(w, 1, 2))


def _rope_tables():
    t = jnp.arange(SEQ)
    pos = jnp.stack([t // GRID_W, t % GRID_W], axis=-1).astype(F32)
    n_freq = MLA_ROPE // 4
    inv_freq = ROPE_THETA ** (-jnp.arange(n_freq, dtype=F32) / n_freq)
    ang = pos[:, :, None] * inv_freq
    cos, sin = jnp.cos(ang), jnp.sin(ang)
    c64 = jnp.concatenate([cos[:, 0], cos[:, 0], cos[:, 1], cos[:, 1]], axis=-1)
    s64 = jnp.concatenate([-sin[:, 0], sin[:, 0], -sin[:, 1], sin[:, 1]], axis=-1)
    ident = 256
    cos_t = jnp.concatenate([jnp.tile(c64, (1, 2)), jnp.ones((ident, 128), F32)], axis=0)
    sin_t = jnp.concatenate([jnp.tile(s64, (1, 2)), jnp.zeros((ident, 128), F32)], axis=0)
    return cos_t, sin_t


def kernel(x, c, ctx, c_ctx, w_ada, b_ada, norm_g, ffn1_w_gate, ffn1_w_up, ffn1_w_down, ffn2_w_gate, ffn2_w_up, ffn2_w_down, w_in, mla_kv_norm_g, mla_w_uk, mla_w_uv, mla_q_norm_g, mla_k_norm_g, ssd_conv_w, ssd_conv_b, ssd_a_log, ssd_dt_bias, ssd_d, ssd_norm_g, na_q_norm_g, na_k_norm_g, na_rpb, w_branch, w_out):
    nb = x.shape[0]
    assert x.shape[1:] == (SEQ, D) and ctx.shape[1:] == (CTX, D) and nb < 16
    n_lat, n_ctx = nb * SEQ, nb * CTX
    n_tok = n_lat + n_ctx
    depth = w_ada.shape[0]

    cond =jnp.concatenate([c, c_ctx[None], jnp.zeros((16 - nb - 1, D), F32)], axis=0)
    m_all = _adaln(cond, w_ada, b_ada)
    cos_t, sin_t = _rope_tables()

    bf = lambda w: w.astype(BF16)
    ffn1 = (bf(ffn1_w_gate), bf(ffn1_w_up), bf(ffn1_w_down))
    ffn2 = (bf(ffn2_w_gate), bf(ffn2_w_up), bf(ffn2_w_down))
    w_in_p, w_uk, w_uv, w_br, w_o = _permute_w_in(w_in), bf(mla_w_uk), bf(mla_w_uv), bf(w_branch), bf(w_out)

    m4s = [m_all[l].reshape(16, N_MOD, 1, D) for l in range(depth)]
    h, u = _normmod(x.reshape(n_lat, D), ctx.reshape(n_ctx, D), norm_g[0, 0], m4s[0], base=0, n_batch=nb)
    for l in range(depth):
        need_ctx = l < depth - 1
        rows_out = n_tok if need_ctx else n_lat
        m4 = m4s[l]
        mod = dict(m4=m4, n_lat=n_lat, n_batch=nb)

        h, u = _ffn(u, h, n_tok, l, *ffn1, base=0, nxt=(norm_g[l, 1], m4, 3), **mod)
        proj = _in_proj(u, w_in_p, l)

        q, k, v = _mla_prep(proj, cos_t, sin_t, l, w_uk, w_uv, mla_kv_norm_g[l],
                            mla_q_norm_g[l], mla_k_norm_g[l], n_lat)
        mla = _mla_attn(q, k, v, nb)

        na, na_c = _na_attn(proj, na_q_norm_g[l], na_k_norm_g[l], _na_bias_table(na_rpb[l]), nb)

        xbc_act = _ssd_conv(proj, ssd_conv_w[l], ssd_conv_b[l], n_lat)
        dt_t = proj[:, OFF_MISC + MISC_DT:OFF_MISC + MISC_DT + 2 * SSD_HEADS].T
        y_f = _ssd_scan(xbc_act, proj, dt_t, ssd_dt_bias[l], ssd_a_log[l], 0, nb)
        ssd = _ssd_scan(xbc_act, proj, dt_t, ssd_dt_bias[l], ssd_a_log[l], 1, nb,
                        final=(y_f, ssd_d[l], ssd_norm_g[l]))

        mla_c = _mla_ctx_attn(q, k, v, nb) if need_ctx else na_c
        h, u = _mix(((mla, mla_c), ssd, (na, na_c)), proj, l, w_br, w_o, h, rows_out, n_g=norm_g[l, 2], **mod)

        if need_ctx:
            h, u = _ffn(u, h, rows_out, l, *ffn2, base=6, nxt=(norm_g[l + 1, 0], m4s[l + 1], 0), **mod)
        else:
            h = _ffn(u, h, rows_out, l, *ffn2, base=6, **mod)

    return h[:n_lat].reshape(nb, SEQ, D)
```

```python
import functools
import math

import numpy as np
import jax
import jax.numpy as jnp
from jax import lax
from jax.experimental import pallas as pl
from jax.experimental.pallas import tpu as pltpu

F32 = jnp.float32
BF16 = jnp.bfloat16

D = 2048
SEQ = 2048
CTX = 256
DEPTH = 2
GRID_W = 64
GRID_H = SEQ // GRID_W
EPS = 1e-6
ROPE_THETA = 10000.0
N_MOD = 9
D_FF = 5632
HALF = D // 2

MLA_NOPE, MLA_ROPE, MLA_V = 128, 64, 128
MLA_QK = MLA_NOPE + MLA_ROPE
MLA_HEADS = 8
MLA_RANK = 512
MLA_PAD = 256

SSD_HEADS, SSD_P, SSD_G, SSD_N, SSD_CONV, SSD_CHUNK = 16, 64, 4, 128, 5, 128
SSD_XBC = HALF + 2 * SSD_G * SSD_N

NA_HEADS, NA_DH = 8, 128
NA_WIN_ROWS, NA_WIN_COLS = 8, 16
NA_QROWS, NA_QCOLS = 8, 16
NA_KROWS, NA_KCOLS = NA_QROWS + NA_WIN_ROWS - 1, NA_QCOLS + NA_WIN_COLS
NA_NRG, NA_NCB = GRID_H // NA_QROWS, GRID_W // NA_QCOLS
NEG_BIG = -1e30

OFF_GATE = 0
OFF_NA = 3 * D
OFF_XBC = OFF_NA + 3 * HALF
OFF_QNOPE = OFF_XBC + SSD_XBC
OFF_Z = OFF_QNOPE + HALF
OFF_QROPE = OFF_Z + HALF
OFF_CKV = OFF_QROPE + MLA_HEADS * MLA_ROPE
OFF_MISC = OFF_CKV + MLA_RANK
N_PROJ = 29 * 512
MISC_DT = MLA_ROPE

VMEM_LIMIT = 56 * 1024 * 1024


def _cparams(sem):
    return pltpu.CompilerParams(dimension_semantics=sem, vmem_limit_bytes=VMEM_LIMIT)


def _row_tile(rows, preferred):
    tm = preferred
    while rows % tm:
        tm //= 2
    return tm


def _dot(a, b):
    return jnp.dot(a, b, preferred_element_type=F32)


def _dot_nt(a, b):
    return lax.dot_general(a, b, (((1,), (1,)), ((), ())), preferred_element_type=F32)


def _split3(x):
    hi = x.astype(BF16)
    r1 = x - hi.astype(F32)
    mid = r1.astype(BF16)
    lo = (r1 - mid.astype(F32)).astype(BF16)
    return hi, mid, lo


def _dot_sel_right(x, m01):
    hi, mid, lo = _split3(x)
    return _dot(hi, m01) + _dot(mid, m01) + _dot(lo, m01)


def _dot_sel_left(m01, x):
    hi, mid, lo = _split3(x)
    return _dot(m01, hi) + _dot(m01, mid) + _dot(m01, lo)


def _silu(x):
    return x * jax.nn.sigmoid(x)


def _softplus(x):
    return jnp.maximum(x, 0.0) + jnp.log1p(jnp.exp(-jnp.abs(x)))


def _rms(x, g):
    return x * lax.rsqrt(jnp.mean(x * x, axis=-1, keepdims=True) + EPS) * g


def _ada_kernel(c_ref, w_ref, b_ref, o_ref):
    s = _silu(c_ref[...]).astype(BF16)
    o_ref[...] = _dot(s, w_ref[...].astype(BF16)) + b_ref[...]


def _adaln(cond, w_ada, b_ada):
    depth = w_ada.shape[0]
    n = N_MOD * D
    tn = 1024
    return pl.pallas_call(
        _ada_kernel,
        grid=(depth, n // tn),
        in_specs=[
            pl.BlockSpec((16, D), lambda l, j: (0, 0)),
            pl.BlockSpec((None, D, tn), lambda l, j: (l, 0, j)),
            pl.BlockSpec((None, 1, tn), lambda l, j: (l, 0, j)),
        ],
        out_specs=pl.BlockSpec((None, 16, tn), lambda l, j: (l, 0, j)),
        out_shape=jax.ShapeDtypeStruct((depth, 16, n), F32),
        compiler_params=_cparams(("arbitrary", "arbitrary")),
        name="adaln",
    )(cond, w_ada, b_ada.reshape(depth, 1, n))


def _mod_spec(tm, n_lat_tiles, n_batch, k):
    per = SEQ // tm

    def imap(i, *_):
        return (jnp.where(i < n_lat_tiles, i // per, n_batch), k, 0, 0)

    return pl.BlockSpec((None, None, 1, D), imap)


def _normmod_kernel(x_ref, c_ref, g_ref, shift_ref, scale_ref, h_ref, o_ref, *, n_lat_tiles):
    is_lat = pl.program_id(0) < n_lat_tiles
    h = jnp.where(is_lat, x_ref[...], c_ref[...])
    h_ref[...] = h
    y = _rms(h, g_ref[...])
    o_ref[...] = (y * (1.0 + scale_ref[...]) + shift_ref[...]).astype(o_ref.dtype)


def _normmod(x2d, ctx2d, g, m4, base, n_batch):
    tm = 256
    n_lat, n_ctx = x2d.shape[0], ctx2d.shape[0]
    nlt = n_lat // tm
    rows = n_lat + n_ctx
    row = pl.BlockSpec((tm, D), lambda i: (i, 0))
    return pl.pallas_call(
        functools.partial(_normmod_kernel, n_lat_tiles=nlt),
        grid=(rows // tm,),
        in_specs=[
            pl.BlockSpec((tm, D), lambda i: (jnp.minimum(i, nlt - 1), 0)),
            pl.BlockSpec((tm, D), lambda i: (jnp.maximum(i - nlt, 0), 0)),
            pl.BlockSpec((1, D), lambda i: (0, 0)),
            _mod_spec(tm, nlt, n_batch, base),
            _mod_spec(tm, nlt, n_batch, base + 1),
        ],
        out_specs=[row, row],
        out_shape=[jax.ShapeDtypeStruct((rows, D), F32), jax.ShapeDtypeStruct((rows, D), BF16)],
        compiler_params=_cparams(("arbitrary",)),
        name="normmod",
    )(x2d, ctx2d, g.reshape(1, D), m4, m4)


def _ffn_kernel(u_ref, wg_ref, wu_ref, wd_ref, h_ref, gate_ref, *rest, nf, with_next):
    if with_next:
        ng_ref, nshift_ref, nscale_ref, o_ref, un_ref, acc_ref = rest
    else:
        o_ref, acc_ref = rest
    f = pl.program_id(1)

    @pl.when(f == 0)
    def _():
        acc_ref[...] = jnp.zeros_like(acc_ref)

    u = u_ref[...]
    g = _dot(u, wg_ref[...])
    up = _dot(u, wu_ref[...])
    act = (_silu(g) * up).astype(BF16)
    acc_ref[...] += _dot(act, wd_ref[...])

    @pl.when(f == nf - 1)
    def _():
        h_new = h_ref[...] + (0.5 * gate_ref[...]) * acc_ref[...]
        o_ref[...] = h_new
        if with_next:
            y = _rms(h_new, ng_ref[...])
            un_ref[...] = (y * (1.0 + nscale_ref[...]) + nshift_ref[...]).astype(un_ref.dtype)


def _ffn(u, h, rows, layer, wg, wu, wd, m4, base, n_lat, n_batch, nxt=None):
    tm, tf = _row_tile(rows, 512), 512
    nf = D_FF // tf
    nlt = n_lat // tm
    row_spec = pl.BlockSpec((tm, D), lambda i, f: (i, 0))
    in_specs = [
        row_spec,
        pl.BlockSpec((None, D, tf), lambda i, f: (layer, 0, f)),
        pl.BlockSpec((None, D, tf), lambda i, f: (layer, 0, f)),
        pl.BlockSpec((None, tf, D), lambda i, f: (layer, f, 0)),
        row_spec,
        _mod_spec(tm, nlt, n_batch, base + 2),
    ]
    args = [u, wg, wu, wd, h, m4]
    out_specs, out_shape = row_spec, jax.ShapeDtypeStruct((rows, D), F32)
    if nxt is not None:
        n_g, n_m4, n_base = nxt
        in_specs += [pl.BlockSpec((1, D), lambda i, f: (0, 0)),
                     _mod_spec(tm, nlt, n_batch, n_base), _mod_spec(tm, nlt, n_batch, n_base + 1)]
        args += [n_g.reshape(1, D), n_m4, n_m4]
        out_specs = [row_spec, row_spec]
        out_shape = [out_shape, jax.ShapeDtypeStruct((rows, D), BF16)]
    return pl.pallas_call(
        functools.partial(_ffn_kernel, nf=nf, with_next=nxt is not None),
        grid=(rows // tm, nf),
        in_specs=in_specs,
        out_specs=out_specs,
        out_shape=out_shape,
        scratch_shapes=[pltpu.VMEM((tm, D), F32)],
        compiler_params=_cparams(("arbitrary", "arbitrary")),
        name="ffn",
    )(*args)


def _mm_kernel(a_ref, w_ref, o_ref):
    o_ref[...] = _dot(a_ref[...], w_ref[...]).astype(o_ref.dtype)


def _in_proj(u, w, layer):
    rows = u.shape[0]
    tm, tn = _row_tile(rows, 2048), 512
    return pl.pallas_call(
        _mm_kernel,
        grid=(rows // tm, N_PROJ // tn),
        in_specs=[pl.BlockSpec((tm, D), lambda i, j: (i, 0)),
                  pl.BlockSpec((None, D, tn), lambda i, j: (layer, 0, j))],
        out_specs=pl.BlockSpec((tm, tn), lambda i, j: (i, j)),
        out_shape=jax.ShapeDtypeStruct((rows, N_PROJ), F32),
        compiler_params=_cparams(("arbitrary", "arbitrary")),
        name="in_proj",
    )(u, w)


def _mix_kernel(b0_ref, b0c_ref, b1_ref, b2_ref, b2c_ref, wb_ref, g0_ref, g1_ref, g2_ref, wo_ref, h_ref, gate_ref,
                ng_ref, nshift_ref, nscale_ref, o_ref, un_ref, *, n_lat_tiles):
    is_lat = pl.program_id(0) < n_lat_tiles
    b0 = jnp.where(is_lat, b0_ref[...], b0c_ref[...])
    b2 = jnp.where(is_lat, b2_ref[...], b2c_ref[...])
    acc = jax.nn.sigmoid(g0_ref[...]) * _dot(b0, wb_ref[0])
    b1 = b1_ref[...].reshape(b0.shape)
    acc = acc + jax.nn.sigmoid(g1_ref[...]) * _dot(b1, wb_ref[1])
    acc = acc + jax.nn.sigmoid(g2_ref[...]) * _dot(b2, wb_ref[2])
    h_new = h_ref[...] + gate_ref[...] * _dot(acc.astype(BF16), wo_ref[...])
    o_ref[...] = h_new
    y = _rms(h_new, ng_ref[...])
    un_ref[...] = (y * (1.0 + nscale_ref[...]) + nshift_ref[...]).astype(un_ref.dtype)


def _mix(branches, proj, layer, w_branch, w_out, h, rows, m4, n_lat, n_batch, n_g):
    tm = 256
    nlt = n_lat // tm
    row = lambda i: (i, 0)
    once = pl.Buffered(1)
    cpt = tm // SSD_CHUNK
    per = SEQ // tm
    _, groups = _ssd_streams(n_batch)

    def ssd_map(i):
        b = jnp.where(i < nlt, i // per, i - nlt)
        pblk = jnp.where(i < nlt, CTX // tm + i % per, 0)
        return (b % groups, pblk, b // groups, 0, 0)

    (mla, mla_c), ssd, (na, na_c) = branches
    lat_rows = pl.BlockSpec((tm, HALF), lambda i: (jnp.minimum(i, nlt - 1), 0))
    ctx_rows = pl.BlockSpec((tm, HALF), lambda i: (jnp.maximum(i - nlt, 0), 0))
    return pl.pallas_call(
        functools.partial(_mix_kernel, n_lat_tiles=nlt),
        grid=(rows // tm,),
        in_specs=[lat_rows, ctx_rows, pl.BlockSpec((None, cpt, None, SSD_CHUNK, HALF), ssd_map), lat_rows, ctx_rows]
        + [pl.BlockSpec((None, 3, HALF, D), lambda i: (layer, 0, 0, 0), pipeline_mode=once)]
        + [pl.BlockSpec((tm, D), functools.partial(lambda i, k: (i, k), k=k)) for k in range(3)]
        + [pl.BlockSpec((None, D, D), lambda i: (layer, 0, 0), pipeline_mode=once),
           pl.BlockSpec((tm, D), row),
           _mod_spec(tm, nlt, n_batch, 5),
           pl.BlockSpec((1, D), lambda i: (0, 0)),
           _mod_spec(tm, nlt, n_batch, 6), _mod_spec(tm, nlt, n_batch, 7)],
        out_specs=[pl.BlockSpec((tm, D), row), pl.BlockSpec((tm, D), row)],
        out_shape=[jax.ShapeDtypeStruct((rows, D), F32), jax.ShapeDtypeStruct((rows, D), BF16)],
        compiler_params=_cparams(("arbitrary",)),
        name="mix",
    )(mla, mla_c, ssd, na, na_c, w_branch, proj, proj, proj, w_out, h, m4, n_g.reshape(1, D), m4, m4)


def _rope128(y, cos, sin):
    lane = lax.broadcasted_iota(jnp.int32, y.shape, 1)
    partner = jnp.where((lane % 32) < 16, pltpu.roll(y, 112, 1), pltpu.roll(y, 16, 1))
    return y * cos + partner * sin


def _mla_prep_kernel(qn_ref, qr_ref, ckv_ref, misc_ref, cos_ref, sin_ref, wuk_ref, wuv_ref,
                     gkv_ref, gqn_ref, gqr_ref, gkn_ref, gkr_ref, q_ref, k_ref, v_ref, *, scale):
    cos, sin = cos_ref[...], sin_ref[...]
    lane = lax.broadcasted_iota(jnp.int32, cos.shape, 1)
    low = lane < MLA_ROPE

    ckv = _rms(ckv_ref[...], gkv_ref[...]).astype(BF16)
    k_nope = _dot(ckv, wuk_ref[...])
    v_ref[...] = _dot(ckv, wuv_ref[...]).astype(v_ref.dtype)

    misc = misc_ref[...]
    kr_sq = jnp.where(low, misc * misc, 0.0)
    kr_rot = jnp.where(low, _rope128(misc * gkr_ref[...], cos, sin), 0.0)
    kr_rot_hi = pltpu.roll(kr_rot, MLA_ROPE, 1)

    gqn, gqr, gkn = gqn_ref[...], gqr_ref[...], gkn_ref[...]
    for pair in range(MLA_HEADS // 2):
        qr = qr_ref[:, pair * 128:(pair + 1) * 128]
        qr2 = qr * qr
        rs = []
        for sub in (0, 1):
            hd = 2 * pair + sub
            qn = qn_ref[:, hd * 128:(hd + 1) * 128]
            q_sq = qn * qn + jnp.where(low if sub == 0 else ~low, qr2, 0.0)
            r = lax.rsqrt(jnp.sum(q_sq, axis=-1, keepdims=True) * (1.0 / MLA_QK) + EPS)
            rs.append(r)
            q_ref[:, hd * MLA_PAD:hd * MLA_PAD + 128] = (qn * r * gqn * scale).astype(q_ref.dtype)
            kn = k_nope[:, hd * 128:(hd + 1) * 128]
            rk = lax.rsqrt(jnp.sum(kn * kn + kr_sq, axis=-1, keepdims=True) * (1.0 / MLA_QK) + EPS)
            k_ref[:, hd * MLA_PAD:hd * MLA_PAD + 128] = (kn * rk * gkn).astype(k_ref.dtype)
            k_ref[:, hd * MLA_PAD + 128:(hd + 1) * MLA_PAD] = ((kr_rot if sub == 0 else kr_rot_hi) * rk).astype(k_ref.dtype)
        q_rot = _rope128(qr * jnp.where(low, rs[0], rs[1]) * gqr, cos, sin) * scale
        q_ref[:, (2 * pair) * MLA_PAD + 128:(2 * pair + 1) * MLA_PAD] = jnp.where(low, q_rot, 0.0).astype(q_ref.dtype)
        q_ref[:, (2 * pair + 1) * MLA_PAD + 128:(2 * pair + 2) * MLA_PAD] = jnp.where(low, 0.0, q_rot).astype(q_ref.dtype)


def _mla_prep(proj, cos_t, sin_t, layer, w_uk, w_uv, g_kv, g_q, g_k, n_lat):
    rows = proj.shape[0]
    tm = 256
    nlt = n_lat // tm
    per = SEQ // tm

    def tab_map(i):
        return (jnp.where(i < nlt, i % per, per), 0)

    def row(v):
        return v.reshape(1, -1)

    def rope_gain(g):
        return jnp.concatenate([g[MLA_NOPE:], g[MLA_NOPE:]]).reshape(1, 128)

    const = lambda i: (0, 0)
    return pl.pallas_call(
        functools.partial(_mla_prep_kernel, scale=MLA_QK ** -0.5 * math.log2(math.e)),
        grid=(rows // tm,),
        in_specs=[
            pl.BlockSpec((tm, HALF), lambda i: (i, OFF_QNOPE // HALF)),
            pl.BlockSpec((tm, 512), lambda i: (i, OFF_QROPE // 512)),
            pl.BlockSpec((tm, MLA_RANK), lambda i: (i, OFF_CKV // MLA_RANK)),
            pl.BlockSpec((tm, 128), lambda i: (i, OFF_MISC // 128)),
            pl.BlockSpec((tm, 128), tab_map),
            pl.BlockSpec((tm, 128), tab_map),
            pl.BlockSpec((None, MLA_RANK, HALF), lambda i: (layer, 0, 0)),
            pl.BlockSpec((None, MLA_RANK, HALF), lambda i: (layer, 0, 0)),
            pl.BlockSpec((1, MLA_RANK), const),
            pl.BlockSpec((1, 128), const),
            pl.BlockSpec((1, 128), const),
            pl.BlockSpec((1, 128), const),
            pl.BlockSpec((1, 128), const),
        ],
        out_specs=[
            pl.BlockSpec((tm, MLA_HEADS * MLA_PAD), lambda i: (i, 0)),
            pl.BlockSpec((tm, MLA_HEADS * MLA_PAD), lambda i: (i, 0)),
            pl.BlockSpec((tm, HALF), lambda i: (i, 0)),
        ],
        out_shape=[
            jax.ShapeDtypeStruct((rows, MLA_HEADS * MLA_PAD), BF16),
            jax.ShapeDtypeStruct((rows, MLA_HEADS * MLA_PAD), BF16),
            jax.ShapeDtypeStruct((rows, HALF), BF16),
        ],
        compiler_params=_cparams(("arbitrary",)),
        name="mla_prep",
    )(proj, proj, proj, proj, cos_t, sin_t, w_uk, w_uv, row(g_kv),
      row(g_q[:MLA_NOPE]), rope_gain(g_q), row(g_k[:MLA_NOPE]), rope_gain(g_k))


LOG2E = math.log2(math.e)


def _softmax2_pv(s1, s2, v1, v2):
    m = jnp.maximum(jnp.max(s1, axis=-1, keepdims=True), jnp.max(s2, axis=-1, keepdims=True))
    p1 = jnp.exp2(s1 - m)
    p2 = jnp.exp2(s2 - m)
    denom = jnp.sum(p1, axis=-1, keepdims=True) + jnp.sum(p2, axis=-1, keepdims=True)
    return (_dot(p1.astype(BF16), v1) + _dot(p2.astype(BF16), v2)) / denom


def _softmax_pv(s, v):
    p = jnp.exp2(s - jnp.max(s, axis=-1, keepdims=True))
    return _dot(p.astype(BF16), v) / jnp.sum(p, axis=-1, keepdims=True)


def _mla_attn_kernel(q_ref, kl_ref, kc_ref, vl_ref, vc_ref, o_ref, *, sub):
    kl, kc, vl, vc = kl_ref[...], kc_ref[...], vl_ref[...], vc_ref[...]
    rows = q_ref.shape[0] // sub
    for t in range(sub):
        q = q_ref[t * rows:(t + 1) * rows, :]
        o = _softmax2_pv(_dot_nt(q, kl), _dot_nt(q, kc), vl, vc)
        o_ref[t * rows:(t + 1) * rows, :] = o.astype(o_ref.dtype)


def _mla_attn(q, k, v, n_batch):
    tq = 1024
    nq = SEQ // tq
    lat_blocks = n_batch * SEQ // CTX
    return pl.pallas_call(
        functools.partial(_mla_attn_kernel, sub=4),
        grid=(n_batch, MLA_HEADS, nq),
        in_specs=[
            pl.BlockSpec((tq, MLA_PAD), lambda b, h, i: (b * nq + i, h)),
            pl.BlockSpec((SEQ, MLA_PAD), lambda b, h, i: (b, h)),
            pl.BlockSpec((CTX, MLA_PAD), lambda b, h, i: (lat_blocks + b, h)),
            pl.BlockSpec((SEQ, MLA_V), lambda b, h, i: (b, h)),
            pl.BlockSpec((CTX, MLA_V), lambda b, h, i: (lat_blocks + b, h)),
        ],
        out_specs=pl.BlockSpec((tq, MLA_V), lambda b, h, i: (b * nq + i, h)),
        out_shape=jax.ShapeDtypeStruct((n_batch * SEQ, HALF), BF16),
        compiler_params=_cparams(("arbitrary", "arbitrary", "arbitrary")),
        name="mla_attn",
    )(q, k, k, v, v)


def _ctx_attn_kernel(q_ref, k_ref, v_ref, o_ref):
    o_ref[...] = _softmax_pv(_dot_nt(q_ref[...], k_ref[...]), v_ref[...]).astype(o_ref.dtype)


def _mla_ctx_attn(q, k, v, n_batch):
    lat_blocks = n_batch * SEQ // CTX
    return pl.pallas_call(
        _ctx_attn_kernel,
        grid=(n_batch, MLA_HEADS),
        in_specs=[
            pl.BlockSpec((CTX, MLA_PAD), lambda b, h: (lat_blocks + b, h)),
            pl.BlockSpec((CTX, MLA_PAD), lambda b, h: (lat_blocks + b, h)),
            pl.BlockSpec((CTX, MLA_V), lambda b, h: (lat_blocks + b, h)),
        ],
        out_specs=pl.BlockSpec((CTX, MLA_V), lambda b, h: (b, h)),
        out_shape=jax.ShapeDtypeStruct((n_batch * CTX, HALF), BF16),
        compiler_params=_cparams(("arbitrary", "arbitrary")),
        name="mla_ctx_attn",
    )(q, k, v)


def _na_key_row0(rg):
    return int(np.clip(NA_QROWS * rg - NA_WIN_ROWS // 2, 0, GRID_H - NA_KROWS))


def _na_band0(n):
    return int(np.clip(NA_QCOLS * n - NA_WIN_COLS // 2, 0, GRID_W - NA_KCOLS))


def _na_offsets(q0, k0, nq, nk, win, extent):
    qpos = q0 + np.arange(nq)[:, None]
    kpos = k0 + np.arange(nk)[None, :]
    start = np.clip(qpos - win // 2, 0, extent - win)
    valid = (kpos >= start) & (kpos < start + win)
    return np.where(valid, kpos - qpos + win - 1, 2 * win - 1)


def _na_block_classes():
    rows = [_na_offsets(NA_QROWS * g, _na_key_row0(g), NA_QROWS, NA_KROWS, NA_WIN_ROWS, GRID_H) for g in range(NA_NRG)]
    cols = [_na_offsets(NA_QCOLS * n, _na_band0(n), NA_QCOLS, NA_KCOLS, NA_WIN_COLS, GRID_W) for n in range(NA_NCB)]

    def classes(pats):
        reps, ids = [], []
        for p in pats:
            for k, r in enumerate(reps):
                if np.array_equal(p, r):
                    ids.append(k)
                    break
            else:
                ids.append(len(reps))
                reps.append(p)
        return np.stack(reps), ids

    return classes(rows), classes(cols)


def _na_bias_table(rpb):
    (r_off, _), (c_off, _) = _na_block_classes()
    ext = jnp.pad(rpb * LOG2E, ((0, 0), (0, 1), (0, 1)), constant_values=NEG_BIG)
    r_sel = jnp.asarray(r_off[..., None] == np.arange(2 * NA_WIN_ROWS), F32)
    c_sel = jnp.asarray(c_off[..., None] == np.arange(2 * NA_WIN_COLS), F32)
    bias = jnp.einsum('gika,hab,njcb->hgnijkc', r_sel, ext, c_sel, precision=lax.Precision.HIGHEST)
    return bias.reshape(NA_HEADS, r_off.shape[0] * c_off.shape[0], NA_QROWS * NA_QCOLS, NA_KROWS * NA_KCOLS)


def _na_kernel(q_ref, k_ref, v_ref, qc_ref, kc_ref, vc_ref, gq_ref, gk_ref, bias_ref, o_ref, oc_ref,
               qn_s, kn_s, sw_s, sc_s, pw_s, pc_s, *, scale):
    gq, gk = gq_ref[...], gk_ref[...]
    qn_s[...] = _rms(q_ref[...], gq) * scale
    kn_s[...] = _rms(k_ref[...], gk)
    kc = _rms(kc_ref[...], gk).astype(BF16)
    vc = vc_ref[...].astype(BF16)
    (_, row_cls), (col_reps, col_cls) = _na_block_classes()
    blocks = [(rg, n) for rg in range(NA_NRG) for n in range(NA_NCB)]

    def q_rows(rg, n):
        return [(NA_QROWS * rg + i) * GRID_W + NA_QCOLS * n for i in range(NA_QROWS)]

    def k_rows(rg, n):
        return [(_na_key_row0(rg) + kr) * GRID_W + _na_band0(n) for kr in range(NA_KROWS)]

    for b, (rg, n) in enumerate(blocks):
        qb = jnp.concatenate([qn_s[pl.ds(t, NA_QCOLS), :] for t in q_rows(rg, n)], axis=0).astype(BF16)
        kb = jnp.concatenate([kn_s[pl.ds(t, NA_KCOLS), :] for t in k_rows(rg, n)], axis=0).astype(BF16)
        sw_s[b] = _dot_nt(qb, kb) + bias_ref[row_cls[rg] * len(col_reps) + col_cls[n]]
        sc_s[b] = _dot_nt(qb, kc)
    denoms = []
    for b in range(len(blocks)):
        s1, s2 = sw_s[b], sc_s[b]
        m = jnp.maximum(jnp.max(s1, axis=-1, keepdims=True), jnp.max(s2, axis=-1, keepdims=True))
        p1, p2 = jnp.exp2(s1 - m), jnp.exp2(s2 - m)
        denoms.append(jnp.sum(p1, axis=-1, keepdims=True) + jnp.sum(p2, axis=-1, keepdims=True))
        pw_s[b] = p1.astype(BF16)
        pc_s[b] = p2.astype(BF16)
    for b, (rg, n) in enumerate(blocks):
        vb = jnp.concatenate([v_ref[pl.ds(t, NA_KCOLS), :] for t in k_rows(rg, n)], axis=0).astype(BF16)
        o = ((_dot(pw_s[b], vb) + _dot(pc_s[b], vc)) / denoms[b]).astype(o_ref.dtype)
        for i, t in enumerate(q_rows(rg, n)):
            o_ref[pl.ds(t, NA_QCOLS), :] = o[i * NA_QCOLS:(i + 1) * NA_QCOLS]
    qc = (_rms(qc_ref[...], gq) * scale).astype(BF16)
    oc_ref[...] = _softmax_pv(_dot_nt(qc, kc), vc).astype(oc_ref.dtype)


def _na_attn(proj, g_q, g_k, bias, n_batch):
    base = OFF_NA // NA_DH
    lat_blocks = n_batch * SEQ // CTX

    def lat(part):
        return pl.BlockSpec((SEQ, NA_DH), lambda h, b: (b, base + part * NA_HEADS + h))

    def cx(part):
        return pl.BlockSpec((CTX, NA_DH), lambda h, b: (lat_blocks + b, base + part * NA_HEADS + h))

    const = lambda h, b: (0, 0)
    nblk, nq, nk = bias.shape[1:]
    return pl.pallas_call(
        functools.partial(_na_kernel, scale=NA_DH ** -0.5 * LOG2E),
        grid=(NA_HEADS, n_batch),
        in_specs=[lat(0), lat(1), lat(2), cx(0), cx(1), cx(2),
                  pl.BlockSpec((1, NA_DH), const), pl.BlockSpec((1, NA_DH), const),
                  pl.BlockSpec((None, nblk, nq, nk), lambda h, b: (h, 0, 0, 0))],
        out_specs=[pl.BlockSpec((SEQ, NA_DH), lambda h, b: (b, h)),
                   pl.BlockSpec((CTX, NA_DH), lambda h, b: (b, h))],
        out_shape=[jax.ShapeDtypeStruct((n_batch * SEQ, HALF), BF16),
                   jax.ShapeDtypeStruct((n_batch * CTX, HALF), BF16)],
        scratch_shapes=[pltpu.VMEM((SEQ, NA_DH), F32), pltpu.VMEM((SEQ, NA_DH), F32),
                        pltpu.VMEM((NA_NRG * NA_NCB, nq, nk), F32), pltpu.VMEM((NA_NRG * NA_NCB, nq, CTX), F32),
                        pltpu.VMEM((NA_NRG * NA_NCB, nq, nk), BF16), pltpu.VMEM((NA_NRG * NA_NCB, nq, CTX), BF16)],
        compiler_params=_cparams(("arbitrary", "arbitrary")),
        name="na_attn",
    )(proj, proj, proj, proj, proj, proj, g_q.reshape(1, NA_DH), g_k.reshape(1, NA_DH), bias)


CONV_TM = 256
CONV_HALO = 8


def _conv_kernel(prev_ref, cur_ref, next_ref, w_ref, b_ref, o_ref, pad_s, *, n_lat_tiles):
    i = pl.program_id(0)
    per = SEQ // CONV_TM
    is_ctx = i >= n_lat_tiles
    at_start = is_ctx | (i % per == 0)
    at_end = is_ctx | (i % per == per - 1)
    pad_s[0:CONV_HALO, :] = jnp.where(at_start, 0.0, prev_ref[...])
    pad_s[CONV_HALO:CONV_HALO + CONV_TM, :] = cur_ref[...]
    pad_s[CONV_HALO + CONV_TM:, :] = jnp.where(at_end, 0.0, next_ref[...])
    acc = jnp.broadcast_to(b_ref[...], cur_ref.shape)
    for k in range(SSD_CONV):
        acc = acc + w_ref[k:k + 1, :] * pad_s[pl.ds(CONV_HALO - SSD_CONV // 2 + k, CONV_TM), :]
    o_ref[...] = _silu(acc)


def _ssd_conv(proj, conv_w, conv_b, n_lat):
    rows = proj.shape[0]
    tc = 1024
    assert OFF_XBC % tc == 0 and SSD_XBC % tc == 0
    nlt = n_lat // CONV_TM
    cb0 = OFF_XBC // tc
    hb = CONV_TM // CONV_HALO
    last = rows // CONV_HALO - 1
    return pl.pallas_call(
        functools.partial(_conv_kernel, n_lat_tiles=nlt),
        grid=(rows // CONV_TM, SSD_XBC // tc),
        in_specs=[
            pl.BlockSpec((CONV_HALO, tc), lambda i, j: (jnp.maximum(i * hb - 1, 0), cb0 + j)),
            pl.BlockSpec((CONV_TM, tc), lambda i, j: (i, cb0 + j)),
            pl.BlockSpec((CONV_HALO, tc), lambda i, j: (jnp.minimum((i + 1) * hb, last), cb0 + j)),
            pl.BlockSpec((SSD_CONV, tc), lambda i, j: (0, j)),
            pl.BlockSpec((1, tc), lambda i, j: (0, j)),
        ],
        out_specs=pl.BlockSpec((CONV_TM, tc), lambda i, j: (i, j)),
        out_shape=jax.ShapeDtypeStruct((rows, SSD_XBC), F32),
        scratch_shapes=[pltpu.VMEM((CONV_TM + 2 * CONV_HALO, tc), F32)],
        compiler_params=_cparams(("arbitrary", "arbitrary")),
        name="ssd_conv",
    )(proj, proj, proj, conv_w, conv_b.reshape(1, SSD_XBC))


def _ssd_scan_kernel(*refs, reverse, final, streams):
    brow_ref, bcol_ref, acol_ref, e1_ref = refs[:4]
    pos, per = 4, 3
    if final:
        d_ref, g_ref, yprev_ref = refs[4:7]
        pos, per = 7, 4
    y_ref, state_s = refs[pos + per * streams:]

    @pl.when(pl.program_id(1) == 0)
    def _():
        state_s[...] = jnp.zeros_like(state_s)

    for s in range(streams):
        r = refs[pos + per * s:pos + per * (s + 1)]
        if final:
            rest = (yprev_ref.at[s], r[3], d_ref, g_ref, y_ref.at[s], state_s.at[s])
        else:
            rest = (y_ref.at[s], state_s.at[s])
        _ssd_chunk(r[0], r[1], r[2], brow_ref, bcol_ref, acol_ref, e1_ref, *rest, reverse=reverse, final=final)


def _ssd_chunk(xbc_ref, misc_ref, dtt_ref, brow_ref, bcol_ref, acol_ref, e1_ref, *rest, reverse, final):
    if final:
        yprev_ref, z_ref, d_ref, g_ref, y_ref, state_s = rest
    else:
        y_ref, state_s = rest
    L = SSD_CHUNK

    li = lax.broadcasted_iota(jnp.int32, (L, L), 0)
    si = lax.broadcasted_iota(jnp.int32, (L, L), 1)
    lane = lax.broadcasted_iota(jnp.int32, (L, 128), 1)
    if reverse:
        cum_r = jnp.where(li >= si, 1.0, 0.0).astype(BF16)
        causal = li <= si
    else:
        cum_r = jnp.where(li <= si, 1.0, 0.0).astype(BF16)
        causal = li >= si

    dt_exp = _dot_sel_right(_softplus(misc_ref[...] + brow_ref[...]), e1_ref[...])
    dt_row = _softplus(dtt_ref[...] + bcol_ref[...])
    ac_row = _dot_sel_right(dt_row * (-jnp.exp(acol_ref[...])), cum_r)
    ac_col = [jnp.broadcast_to(ac_row[hd:hd + 1, :], (L, L)).T for hd in range(SSD_HEADS)]
    ac_exp = jnp.concatenate([jnp.where(lane < SSD_P, ac_col[2 * p], ac_col[2 * p + 1])
                              for p in range(SSD_HEADS // 2)], axis=1)

    end = 0 if reverse else L - 1
    tot_exp = ac_exp[end:end + 1, :]
    xbc = xbc_ref[...]
    xdt = xbc[:, :HALF] * dt_exp
    xw = (xdt * jnp.exp(tot_exp - ac_exp)).astype(BF16)
    xdt_b = xdt.astype(BF16)
    eac = jnp.exp(ac_exp)
    chunk_decay = jnp.exp(tot_exp)
    gw = SSD_HEADS // SSD_G * SSD_P
    for g in range(SSD_G):
        gs = slice(g * gw, (g + 1) * gw)
        b_t = xbc[:, HALF + g * SSD_N:HALF + (g + 1) * SSD_N].T.astype(BF16)
        c_g = xbc[:, HALF + SSD_G * SSD_N + g * SSD_N:HALF + SSD_G * SSD_N + (g + 1) * SSD_N].astype(BF16)
        cb = _dot(c_g, b_t)
        st = state_s[:, gs]
        y_off = _dot(c_g, st.astype(BF16)) * eac[:, gs]
        state_s[:, gs] = st * chunk_decay[:, gs] + _dot(b_t, xw[:, gs])
        y_pairs = []
        for pair in range(2):
            ha = 4 * g + 2 * pair
            xp = xdt_b[:, ha * SSD_P:ha * SSD_P + 128]
            ys = []
            for hd in (ha, ha + 1):
                seg = ac_col[hd] - ac_row[hd:hd + 1, :]
                m = (cb * jnp.exp(jnp.where(causal, seg, -jnp.inf))).astype(BF16)
                ys.append(_dot(m, xp))
            y_pairs.append(jnp.where(lane < SSD_P, ys[0], ys[1]))
        y_g = jnp.concatenate(y_pairs, axis=1) + y_off
        if final:
            y_all = yprev_ref[:, gs] + y_g + d_ref[:, gs] * xbc[:, gs]
            y_ref[:, gs] = _rms(y_all * _silu(z_ref[:, gs]), g_ref[:, gs]).astype(y_ref.dtype)
        else:
            y_ref[:, gs] = y_g


def _ssd_streams(n_batch):
    streams = 2 if n_batch % 2 == 0 else 1
    return streams, n_batch // streams


def _ssd_scan(xbc_act, proj, dt_t, dt_bias, a_log, direction, n_batch, final=None):
    reverse = direction == 1
    nlc, ncc = SEQ // SSD_CHUNK, CTX // SSD_CHUNK
    lat_chunks = n_batch * nlc
    streams, groups = _ssd_streams(n_batch)

    def pos(j):
        return jnp.where(j < ncc, ncc - 1 - j, 2 * ncc + nlc - 1 - j) if reverse else j

    def blk(s):
        def f(g, j):
            b, p = s * groups + g, pos(j)
            return jnp.where(p < ncc, lat_chunks + b * ncc + p, b * nlc + p - ncc)
        return f

    lane0 = MISC_DT + SSD_HEADS * direction
    b_row = jnp.zeros((1, 128), F32).at[0, lane0:lane0 + SSD_HEADS].set(dt_bias[direction])
    b_col, a_col = dt_bias[direction].reshape(SSD_HEADS, 1), a_log[direction].reshape(SSD_HEADS, 1)
    e1 = jnp.asarray((np.arange(128)[:, None] - lane0) == (np.arange(HALF)[None] // SSD_P), BF16)
    const = lambda g, j: (0, 0)
    y_spec = pl.BlockSpec((None, None, streams, SSD_CHUNK, HALF), lambda g, j: (g, pos(j), 0, 0, 0))
    in_specs = [pl.BlockSpec((1, 128), const), pl.BlockSpec((SSD_HEADS, 1), const),
                pl.BlockSpec((SSD_HEADS, 1), const), pl.BlockSpec((128, HALF), const)]
    args = [b_row, b_col, a_col, e1]
    if final is not None:
        y_prev, d_skip, norm_g = final
        in_specs += [pl.BlockSpec((1, HALF), const), pl.BlockSpec((1, HALF), const), y_spec]
        args += [jnp.repeat(d_skip, SSD_P).reshape(1, HALF), norm_g.reshape(1, HALF), y_prev]
    for s in range(streams):
        f = blk(s)
        in_specs += [pl.BlockSpec((SSD_CHUNK, SSD_XBC), lambda g, j, f=f: (f(g, j), 0)),
                     pl.BlockSpec((SSD_CHUNK, 128), lambda g, j, f=f: (f(g, j), OFF_MISC // 128)),
                     pl.BlockSpec((SSD_HEADS, SSD_CHUNK), lambda g, j, f=f: (direction, f(g, j)))]
        args += [xbc_act, proj, dt_t]
        if final is not None:
            in_specs.append(pl.BlockSpec((SSD_CHUNK, HALF), lambda g, j, f=f: (f(g, j), OFF_Z // HALF)))
            args.append(proj)
    return pl.pallas_call(
        functools.partial(_ssd_scan_kernel, reverse=reverse, final=final is not None, streams=streams),
        grid=(groups, ncc + nlc),
        in_specs=in_specs,
        out_specs=y_spec,
        out_shape=jax.ShapeDtypeStruct((groups, ncc + nlc, streams, SSD_CHUNK, HALF), F32 if final is None else BF16),
        scratch_shapes=[pltpu.VMEM((streams, SSD_N, HALF), F32)],
        compiler_params=_cparams(("arbitrary", "arbitrary")),
        name="ssd_scan_bwd" if reverse else "ssd_scan_fwd",
    )(*args)


def _w_in_pieces():
    o = np.cumsum((0, MLA_HEADS * MLA_QK, MLA_RANK, MLA_ROPE, HALF, SSD_XBC, 2 * SSD_HEADS, 3 * HALF, 3 * D))
    seg = lambda k: (int(o[k]), int(o[k + 1] - o[k]))
    q_nope = [(hd * MLA_QK, MLA_NOPE) for hd in range(MLA_HEADS)]
    q_rope = [(hd * MLA_QK + MLA_NOPE, MLA_ROPE) for hd in range(MLA_HEADS)]
    return [seg(7), seg(6), seg(4), *q_nope, seg(3), *q_rope, seg(1), seg(2), seg(5)]


def _permute_kernel(wt_ref, o_ref, rows_s):
    dst = 0
    for a, n in _w_in_pieces():
        rows_s[dst:dst + n, :] = wt_ref[a:a + n, :]
        dst += n
    rows_s[dst:, :] = jnp.zeros((rows_s.shape[0] - dst, rows_s.shape[1]), rows_s.dtype)
    for j in range(N_PROJ // 128):
        cols = slice(j * 128, (j + 1) * 128)
        if j * 128 < rows_s.shape[0]:
            o_ref[:, cols] = rows_s[cols, :].T.astype(o_ref.dtype)
        else:
            o_ref[:, cols] = jnp.zeros((o_ref.shape[0], 128), o_ref.dtype)


def _permute_w_in(w):
    depth, _, n_in = w.shape
    tk = 128
    used = OFF_MISC + 128
    assert sum(n for _, n in _w_in_pieces()) == n_in <= used
    return pl.pallas_call(
        _permute_kernel,
        grid=(depth, D // tk),
        in_specs=[pl.BlockSpec((None, n_in, tk), lambda l, i: (l, 0, i))],
        out_specs=pl.BlockSpec((None, tk, N_PROJ), lambda l, i: (l, i, 0)),
        out_shape=jax.ShapeDtypeStruct((depth, D, N_PROJ), BF16),
        scratch_shapes=[pltpu.VMEM((used, tk), F32)],
        compiler_params=_cparams(("arbitrary", "arbitrary")),
        name="permute_w_in",
    )(jnp.swapaxes(w, 1, 2))


def _rope_tables():
    t = jnp.arange(SEQ)
    pos = jnp.stack([t // GRID_W, t % GRID_W], axis=-1).astype(F32)
    n_freq = MLA_ROPE // 4
    inv_freq = ROPE_THETA ** (-jnp.arange(n_freq, dtype=F32) / n_freq)
    ang = pos[:, :, None] * inv_freq
    cos, sin = jnp.cos(ang), jnp.sin(ang)
    c64 = jnp.concatenate([cos[:, 0], cos[:, 0], cos[:, 1], cos[:, 1]], axis=-1)
    s64 = jnp.concatenate([-sin[:, 0], sin[:, 0], -sin[:, 1], sin[:, 1]], axis=-1)
    ident = 256
    cos_t = jnp.concatenate([jnp.tile(c64, (1, 2)), jnp.ones((ident, 128), F32)], axis=0)
    sin_t = jnp.concatenate([jnp.tile(s64, (1, 2)), jnp.zeros((ident, 128), F32)], axis=0)
    return cos_t, sin_t


def kernel(x, c, ctx, c_ctx, w_ada, b_ada, norm_g, ffn1_w_gate, ffn1_w_up, ffn1_w_down, ffn2_w_gate, ffn2_w_up, ffn2_w_down, w_in, mla_kv_norm_g, mla_w_uk, mla_w_uv, mla_q_norm_g, mla_k_norm_g, ssd_conv_w, ssd_conv_b, ssd_a_log, ssd_dt_bias, ssd_d, ssd_norm_g, na_q_norm_g, na_k_norm_g, na_rpb, w_branch, w_out):
    nb = x.shape[0]
    assert x.shape[1:] == (SEQ, D) and ctx.shape[1:] == (CTX, D) and nb < 16
    n_lat, n_ctx = nb * SEQ, nb * CTX
    n_tok = n_lat + n_ctx
    depth = w_ada.shape[0]

    cond =jnp.concatenate([c, c_ctx[None], jnp.zeros((16 - nb - 1, D), F32)], axis=0)
    m_all = _adaln(cond, w_ada, b_ada)
    cos_t, sin_t = _rope_tables()

    bf = lambda w: w.astype(BF16)
    ffn1 = (bf(ffn1_w_gate), bf(ffn1_w_up), bf(ffn1_w_down))
    ffn2 = (bf(ffn2_w_gate), bf(ffn2_w_up), bf(ffn2_w_down))
    w_in_p, w_uk, w_uv, w_br, w_o = _permute_w_in(w_in), bf(mla_w_uk), bf(mla_w_uv), bf(w_branch), bf(w_out)

    m4s = [m_all[l].reshape(16, N_MOD, 1, D) for l in range(depth)]
    h, u = _normmod(x.reshape(n_lat, D), ctx.reshape(n_ctx, D), norm_g[0, 0], m4s[0], base=0, n_batch=nb)
    for l in range(depth):
        need_ctx = l < depth - 1
        rows_out = n_tok if need_ctx else n_lat
        m4 = m4s[l]
        mod = dict(m4=m4, n_lat=n_lat, n_batch=nb)

        h, u = _ffn(u, h, n_tok, l, *ffn1, base=0, nxt=(norm_g[l, 1], m4, 3), **mod)
        proj = _in_proj(u, w_in_p, l)

        q, k, v = _mla_prep(proj, cos_t, sin_t, l, w_uk, w_uv, mla_kv_norm_g[l],
                            mla_q_norm_g[l], mla_k_norm_g[l], n_lat)
        mla = _mla_attn(q, k, v, nb)

        na, na_c = _na_attn(proj, na_q_norm_g[l], na_k_norm_g[l], _na_bias_table(na_rpb[l]), nb)

        xbc_act = _ssd_conv(proj, ssd_conv_w[l], ssd_conv_b[l], n_lat)
        dt_t = proj[:, OFF_MISC + MISC_DT:OFF_MISC + MISC_DT + 2 * SSD_HEADS].T
        y_f = _ssd_scan(xbc_act, proj, dt_t, ssd_dt_bias[l], ssd_a_log[l], 0, nb)
        ssd = _ssd_scan(xbc_act, proj, dt_t, ssd_dt_bias[l], ssd_a_log[l], 1, nb,
                        final=(y_f, ssd_d[l], ssd_norm_g[l]))

        mla_c = _mla_ctx_attn(q, k, v, nb) if need_ctx else na_c
        h, u = _mix(((mla, mla_c), ssd, (na, na_c)), proj, l, w_br, w_o, h, rows_out, n_g=norm_g[l, 2], **mod)

        if need_ctx:
            h, u = _ffn(u, h, rows_out, l, *ffn2, base=6, nxt=(norm_g[l + 1, 0], m4s[l + 1], 0), **mod)
        else:
            h = _ffn(u, h, rows_out, l, *ffn2, base=6, **mod)

    return h[:n_lat].reshape(nb, SEQ, D)
```

```python
import functools
import math

import numpy as np
import jax
import jax.numpy as jnp
from jax import lax
from jax.experimental import pallas as pl
from jax.experimental.pallas import tpu as pltpu

F32 = jnp.float32
BF16 = jnp.bfloat16

D = 2048
SEQ = 2048
CTX = 256
DEPTH = 2
GRID_W = 64
GRID_H = SEQ // GRID_W
EPS = 1e-6
ROPE_THETA = 10000.0
N_MOD = 9
D_FF = 5632
HALF = D // 2

MLA_NOPE, MLA_ROPE, MLA_V = 128, 64, 128
MLA_QK = MLA_NOPE + MLA_ROPE
MLA_HEADS = 8
MLA_RANK = 512
MLA_PAD = 256

SSD_HEADS, SSD_P, SSD_G, SSD_N, SSD_CONV, SSD_CHUNK = 16, 64, 4, 128, 5, 128
SSD_XBC = HALF + 2 * SSD_G * SSD_N

NA_HEADS, NA_DH = 8, 128
NA_WIN_ROWS, NA_WIN_COLS = 8, 16
NA_QROWS, NA_QCOLS = 8, 16
NA_KROWS, NA_KCOLS = NA_QROWS + NA_WIN_ROWS - 1, NA_QCOLS + NA_WIN_COLS
NA_NRG, NA_NCB = GRID_H // NA_QROWS, GRID_W // NA_QCOLS
NEG_BIG = -1e30

OFF_GATE = 0
OFF_NA = 3 * D
OFF_XBC = OFF_NA + 3 * HALF
OFF_QNOPE = OFF_XBC + SSD_XBC
OFF_Z = OFF_QNOPE + HALF
OFF_QROPE = OFF_Z + HALF
OFF_CKV = OFF_QROPE + MLA_HEADS * MLA_ROPE
OFF_MISC = OFF_CKV + MLA_RANK
N_PROJ = 29 * 512
MISC_DT = MLA_ROPE

VMEM_LIMIT = 56 * 1024 * 1024


def _cparams(sem):
    return pltpu.CompilerParams(dimension_semantics=sem, vmem_limit_bytes=VMEM_LIMIT)


def _row_tile(rows, preferred):
    tm = preferred
    while rows % tm:
        tm //= 2
    return tm


def _dot(a, b):
    return jnp.dot(a, b, preferred_element_type=F32)


def _dot_nt(a, b):
    return lax.dot_general(a, b, (((1,), (1,)), ((), ())), preferred_element_type=F32)


def _split3(x):
    hi = x.astype(BF16)
    r1 = x - hi.astype(F32)
    mid = r1.astype(BF16)
    lo = (r1 - mid.astype(F32)).astype(BF16)
    return hi, mid, lo


def _dot_sel_right(x, m01):
    hi, mid, lo = _split3(x)
    return _dot(hi, m01) + _dot(mid, m01) + _dot(lo, m01)


def _dot_sel_left(m01, x):
    hi, mid, lo = _split3(x)
    return _dot(m01, hi) + _dot(m01, mid) + _dot(m01, lo)


def _silu(x):
    return x * jax.nn.sigmoid(x)


def _softplus(x):
    return jnp.maximum(x, 0.0) + jnp.log1p(jnp.exp(-jnp.abs(x)))


def _rms(x, g):
    return x * lax.rsqrt(jnp.mean(x * x, axis=-1, keepdims=True) + EPS) * g


def _ada_kernel(c_ref, w_ref, b_ref, o_ref):
    s = _silu(c_ref[...]).astype(BF16)
    o_ref[...] = _dot(s, w_ref[...].astype(BF16)) + b_ref[...]


def _adaln(cond, w_ada, b_ada):
    depth = w_ada.shape[0]
    n = N_MOD * D
    tn = 1024
    return pl.pallas_call(
        _ada_kernel,
        grid=(depth, n // tn),
        in_specs=[
            pl.BlockSpec((16, D), lambda l, j: (0, 0)),
            pl.BlockSpec((None, D, tn), lambda l, j: (l, 0, j)),
            pl.BlockSpec((None, 1, tn), lambda l, j: (l, 0, j)),
        ],
        out_specs=pl.BlockSpec((None, 16, tn), lambda l, j: (l, 0, j)),
        out_shape=jax.ShapeDtypeStruct((depth, 16, n), F32),
        compiler_params=_cparams(("arbitrary", "arbitrary")),
        name="adaln",
    )(cond, w_ada, b_ada.reshape(depth, 1, n))


def _mod_spec(tm, n_lat_tiles, n_batch, k):
    per = SEQ // tm

    def imap(i, *_):
        return (jnp.where(i < n_lat_tiles, i // per, n_batch), k, 0, 0)

    return pl.BlockSpec((None, None, 1, D), imap)


def _normmod_kernel(x_ref, c_ref, g_ref, shift_ref, scale_ref, h_ref, o_ref, *, n_lat_tiles):
    is_lat = pl.program_id(0) < n_lat_tiles
    h = jnp.where(is_lat, x_ref[...], c_ref[...])
    h_ref[...] = h
    y = _rms(h, g_ref[...])
    o_ref[...] = (y * (1.0 + scale_ref[...]) + shift_ref[...]).astype(o_ref.dtype)


def _normmod(x2d, ctx2d, g, m4, base, n_batch):
    tm = 256
    n_lat, n_ctx = x2d.shape[0], ctx2d.shape[0]
    nlt = n_lat // tm
    rows = n_lat + n_ctx
    row = pl.BlockSpec((tm, D), lambda i: (i, 0))
    return pl.pallas_call(
        functools.partial(_normmod_kernel, n_lat_tiles=nlt),
        grid=(rows // tm,),
        in_specs=[
            pl.BlockSpec((tm, D), lambda i: (jnp.minimum(i, nlt - 1), 0)),
            pl.BlockSpec((tm, D), lambda i: (jnp.maximum(i - nlt, 0), 0)),
            pl.BlockSpec((1, D), lambda i: (0, 0)),
            _mod_spec(tm, nlt, n_batch, base),
            _mod_spec(tm, nlt, n_batch, base + 1),
        ],
        out_specs=[row, row],
        out_shape=[jax.ShapeDtypeStruct((rows, D), F32), jax.ShapeDtypeStruct((rows, D), BF16)],
        compiler_params=_cparams(("arbitrary",)),
        name="normmod",
    )(x2d, ctx2d, g.reshape(1, D), m4, m4)


def _ffn_kernel(u_ref, wg_ref, wu_ref, wd_ref, h_ref, gate_ref, *rest, nf, with_next):
    if with_next:
        ng_ref, nshift_ref, nscale_ref, o_ref, un_ref, acc_ref = rest
    else:
        o_ref, acc_ref = rest
    f = pl.program_id(1)

    @pl.when(f == 0)
    def _():
        acc_ref[...] = jnp.zeros_like(acc_ref)

    u = u_ref[...]
    g = _dot(u, wg_ref[...])
    up = _dot(u, wu_ref[...])
    act = (_silu(g) * up).astype(BF16)
    acc_ref[...] += _dot(act, wd_ref[...])

    @pl.when(f == nf - 1)
    def _():
        h_new = h_ref[...] + (0.5 * gate_ref[...]) * acc_ref[...]
        o_ref[...] = h_new
        if with_next:
            y = _rms(h_new, ng_ref[...])
            un_ref[...] = (y * (1.0 + nscale_ref[...]) + nshift_ref[...]).astype(un_ref.dtype)


def _ffn(u, h, rows, layer, wg, wu, wd, m4, base, n_lat, n_batch, nxt=None):
    tm, tf = _row_tile(rows, 512), 512
    nf = D_FF // tf
    nlt = n_lat // tm
    row_spec = pl.BlockSpec((tm, D), lambda i, f: (i, 0))
    in_specs = [
        row_spec,
        pl.BlockSpec((None, D, tf), lambda i, f: (layer, 0, f)),
        pl.BlockSpec((None, D, tf), lambda i, f: (layer, 0, f)),
        pl.BlockSpec((None, tf, D), lambda i, f: (layer, f, 0)),
        row_spec,
        _mod_spec(tm, nlt, n_batch, base + 2),
    ]
    args = [u, wg, wu, wd, h, m4]
    out_specs, out_shape = row_spec, jax.ShapeDtypeStruct((rows, D), F32)
    if nxt is not None:
        n_g, n_m4, n_base = nxt
        in_specs += [pl.BlockSpec((1, D), lambda i, f: (0, 0)),
                     _mod_spec(tm, nlt, n_batch, n_base), _mod_spec(tm, nlt, n_batch, n_base + 1)]
        args += [n_g.reshape(1, D), n_m4, n_m4]
        out_specs = [row_spec, row_spec]
        out_shape = [out_shape, jax.ShapeDtypeStruct((rows, D), BF16)]
    return pl.pallas_call(
        functools.partial(_ffn_kernel, nf=nf, with_next=nxt is not None),
        grid=(rows // tm, nf),
        in_specs=in_specs,
        out_specs=out_specs,
        out_shape=out_shape,
        scratch_shapes=[pltpu.VMEM((tm, D), F32)],
        compiler_params=_cparams(("arbitrary", "arbitrary")),
        name="ffn",
    )(*args)


def _mm_kernel(a_ref, w_ref, o_ref):
    o_ref[...] = _dot(a_ref[...], w_ref[...]).astype(o_ref.dtype)


def _in_proj(u, w, layer):
    rows = u.shape[0]
    tm, tn = _row_tile(rows, 2048), 512
    return pl.pallas_call(
        _mm_kernel,
        grid=(rows // tm, N_PROJ // tn),
        in_specs=[pl.BlockSpec((tm, D), lambda i, j: (i, 0)),
                  pl.BlockSpec((None, D, tn), lambda i, j: (layer, 0, j))],
        out_specs=pl.BlockSpec((tm, tn), lambda i, j: (i, j)),
        out_shape=jax.ShapeDtypeStruct((rows, N_PROJ), F32),
        compiler_params=_cparams(("arbitrary", "arbitrary")),
        name="in_proj",
    )(u, w)


def _mix_kernel(b0_ref, b0c_ref, b1_ref, b2_ref, b2c_ref, wb_ref, g0_ref, g1_ref, g2_ref, wo_ref, h_ref, gate_ref,
                ng_ref, nshift_ref, nscale_ref, o_ref, un_ref, *, n_lat_tiles):
    is_lat = pl.program_id(0) < n_lat_tiles
    b0 = jnp.where(is_lat, b0_ref[...], b0c_ref[...])
    b2 = jnp.where(is_lat, b2_ref[...], b2c_ref[...])
    acc = jax.nn.sigmoid(g0_ref[...]) * _dot(b0, wb_ref[0])
    b1 = b1_ref[...].reshape(b0.shape)
    acc = acc + jax.nn.sigmoid(g1_ref[...]) * _dot(b1, wb_ref[1])
    acc = acc + jax.nn.sigmoid(g2_ref[...]) * _dot(b2, wb_ref[2])
    h_new = h_ref[...] + gate_ref[...] * _dot(acc.astype(BF16), wo_ref[...])
    o_ref[...] = h_new
    y = _rms(h_new, ng_ref[...])
    un_ref[...] = (y * (1.0 + nscale_ref[...]) + nshift_ref[...]).astype(un_ref.dtype)


def _mix(branches, proj, layer, w_branch, w_out, h, rows, m4, n_lat, n_batch, n_g):
    tm = 256
    nlt = n_lat // tm
    row = lambda i: (i, 0)
    once = pl.Buffered(1)
    cpt = tm // SSD_CHUNK
    per = SEQ // tm
    _, groups = _ssd_streams(n_batch)

    def ssd_map(i):
        b = jnp.where(i < nlt, i // per, i - nlt)
        pblk = jnp.where(i < nlt, CTX // tm + i % per, 0)
        return (b % groups, pblk, b // groups, 0, 0)

    (mla, mla_c), ssd, (na, na_c) = branches
    lat_rows = pl.BlockSpec((tm, HALF), lambda i: (jnp.minimum(i, nlt - 1), 0))
    ctx_rows = pl.BlockSpec((tm, HALF), lambda i: (jnp.maximum(i - nlt, 0), 0))
    return pl.pallas_call(
        functools.partial(_mix_kernel, n_lat_tiles=nlt),
        grid=(rows // tm,),
        in_specs=[lat_rows, ctx_rows, pl.BlockSpec((None, cpt, None, SSD_CHUNK, HALF), ssd_map), lat_rows, ctx_rows]
        + [pl.BlockSpec((None, 3, HALF, D), lambda i: (layer, 0, 0, 0), pipeline_mode=once)]
        + [pl.BlockSpec((tm, D), functools.partial(lambda i, k: (i, k), k=k)) for k in range(3)]
        + [pl.BlockSpec((None, D, D), lambda i: (layer, 0, 0), pipeline_mode=once),
           pl.BlockSpec((tm, D), row),
           _mod_spec(tm, nlt, n_batch, 5),
           pl.BlockSpec((1, D), lambda i: (0, 0)),
           _mod_spec(tm, nlt, n_batch, 6), _mod_spec(tm, nlt, n_batch, 7)],
        out_specs=[pl.BlockSpec((tm, D), row), pl.BlockSpec((tm, D), row)],
        out_shape=[jax.ShapeDtypeStruct((rows, D), F32), jax.ShapeDtypeStruct((rows, D), BF16)],
        compiler_params=_cparams(("arbitrary",)),
        name="mix",
    )(mla, mla_c, ssd, na, na_c, w_branch, proj, proj, proj, w_out, h, m4, n_g.reshape(1, D), m4, m4)


def _rope128(y, cos, sin):
    lane = lax.broadcasted_iota(jnp.int32, y.shape, 1)
    partner = jnp.where((lane % 32) < 16, pltpu.roll(y, 112, 1), pltpu.roll(y, 16, 1))
    return y * cos + partner * sin


def _mla_prep_kernel(qn_ref, qr_ref, ckv_ref, misc_ref, cos_ref, sin_ref, wuk_ref, wuv_ref,
                     gkv_ref, gqn_ref, gqr_ref, gkn_ref, gkr_ref, q_ref, k_ref, v_ref, *, scale):
    cos, sin = cos_ref[...], sin_ref[...]
    lane = lax.broadcasted_iota(jnp.int32, cos.shape, 1)
    low = lane < MLA_ROPE

    ckv = _rms(ckv_ref[...], gkv_ref[...]).astype(BF16)
    k_nope = _dot(ckv, wuk_ref[...])
    v_ref[...] = _dot(ckv, wuv_ref[...]).astype(v_ref.dtype)

    misc = misc_ref[...]
    kr_sq = jnp.where(low, misc * misc, 0.0)
    kr_rot = jnp.where(low, _rope128(misc * gkr_ref[...], cos, sin), 0.0)
    kr_rot_hi = pltpu.roll(kr_rot, MLA_ROPE, 1)

    gqn, gqr, gkn = gqn_ref[...], gqr_ref[...], gkn_ref[...]
    for pair in range(MLA_HEADS // 2):
        qr = qr_ref[:, pair * 128:(pair + 1) * 128]
        qr2 = qr * qr
        rs = []
        for sub in (0, 1):
            hd = 2 * pair + sub
            qn = qn_ref[:, hd * 128:(hd + 1) * 128]
            q_sq = qn * qn + jnp.where(low if sub == 0 else ~low, qr2, 0.0)
            r = lax.rsqrt(jnp.sum(q_sq, axis=-1, keepdims=True) * (1.0 / MLA_QK) + EPS)
            rs.append(r)
            q_ref[:, hd * MLA_PAD:hd * MLA_PAD + 128] = (qn * r * gqn * scale).astype(q_ref.dtype)
            kn = k_nope[:, hd * 128:(hd + 1) * 128]
            rk = lax.rsqrt(jnp.sum(kn * kn + kr_sq, axis=-1, keepdims=True) * (1.0 / MLA_QK) + EPS)
            k_ref[:, hd * MLA_PAD:hd * MLA_PAD + 128] = (kn * rk * gkn).astype(k_ref.dtype)
            k_ref[:, hd * MLA_PAD + 128:(hd + 1) * MLA_PAD] = ((kr_rot if sub == 0 else kr_rot_hi) * rk).astype(k_ref.dtype)
        q_rot = _rope128(qr * jnp.where(low, rs[0], rs[1]) * gqr, cos, sin) * scale
        q_ref[:, (2 * pair) * MLA_PAD + 128:(2 * pair + 1) * MLA_PAD] = jnp.where(low, q_rot, 0.0).astype(q_ref.dtype)
        q_ref[:, (2 * pair + 1) * MLA_PAD + 128:(2 * pair + 2) * MLA_PAD] = jnp.where(low, 0.0, q_rot).astype(q_ref.dtype)


def _mla_prep(proj, cos_t, sin_t, layer, w_uk, w_uv, g_kv, g_q, g_k, n_lat):
    rows = proj.shape[0]
    tm = 256
    nlt = n_lat // tm
    per = SEQ // tm

    def tab_map(i):
        return (jnp.where(i < nlt, i % per, per), 0)

    def row(v):
        return v.reshape(1, -1)

    def rope_gain(g):
        return jnp.concatenate([g[MLA_NOPE:], g[MLA_NOPE:]]).reshape(1, 128)

    const = lambda i: (0, 0)
    return pl.pallas_call(
        functools.partial(_mla_prep_kernel, scale=MLA_QK ** -0.5 * math.log2(math.e)),
        grid=(rows // tm,),
        in_specs=[
            pl.BlockSpec((tm, HALF), lambda i: (i, OFF_QNOPE // HALF)),
            pl.BlockSpec((tm, 512), lambda i: (i, OFF_QROPE // 512)),
            pl.BlockSpec((tm, MLA_RANK), lambda i: (i, OFF_CKV // MLA_RANK)),
            pl.BlockSpec((tm, 128), lambda i: (i, OFF_MISC // 128)),
            pl.BlockSpec((tm, 128), tab_map),
            pl.BlockSpec((tm, 128), tab_map),
            pl.BlockSpec((None, MLA_RANK, HALF), lambda i: (layer, 0, 0)),
            pl.BlockSpec((None, MLA_RANK, HALF), lambda i: (layer, 0, 0)),
            pl.BlockSpec((1, MLA_RANK), const),
            pl.BlockSpec((1, 128), const),
            pl.BlockSpec((1, 128), const),
            pl.BlockSpec((1, 128), const),
            pl.BlockSpec((1, 128), const),
        ],
        out_specs=[
            pl.BlockSpec((tm, MLA_HEADS * MLA_PAD), lambda i: (i, 0)),
            pl.BlockSpec((tm, MLA_HEADS * MLA_PAD), lambda i: (i, 0)),
            pl.BlockSpec((tm, HALF), lambda i: (i, 0)),
        ],
        out_shape=[
            jax.ShapeDtypeStruct((rows, MLA_HEADS * MLA_PAD), BF16),
            jax.ShapeDtypeStruct((rows, MLA_HEADS * MLA_PAD), BF16),
            jax.ShapeDtypeStruct((rows, HALF), BF16),
        ],
        compiler_params=_cparams(("arbitrary",)),
        name="mla_prep",
    )(proj, proj, proj, proj, cos_t, sin_t, w_uk, w_uv, row(g_kv),
      row(g_q[:MLA_NOPE]), rope_gain(g_q), row(g_k[:MLA_NOPE]), rope_gain(g_k))


LOG2E = math.log2(math.e)


def _softmax2_pv(s1, s2, v1, v2):
    m = jnp.maximum(jnp.max(s1, axis=-1, keepdims=True), jnp.max(s2, axis=-1, keepdims=True))
    p1 = jnp.exp2(s1 - m)
    p2 = jnp.exp2(s2 - m)
    denom = jnp.sum(p1, axis=-1, keepdims=True) + jnp.sum(p2, axis=-1, keepdims=True)
    return (_dot(p1.astype(BF16), v1) + _dot(p2.astype(BF16), v2)) / denom


def _softmax_pv(s, v):
    p = jnp.exp2(s - jnp.max(s, axis=-1, keepdims=True))
    return _dot(p.astype(BF16), v) / jnp.sum(p, axis=-1, keepdims=True)


def _mla_attn_kernel(q_ref, kl_ref, kc_ref, vl_ref, vc_ref, o_ref, *, sub):
    kl, kc, vl, vc = kl_ref[...], kc_ref[...], vl_ref[...], vc_ref[...]
    rows = q_ref.shape[0] // sub
    for t in range(sub):
        q = q_ref[t * rows:(t + 1) * rows, :]
        o = _softmax2_pv(_dot_nt(q, kl), _dot_nt(q, kc), vl, vc)
        o_ref[t * rows:(t + 1) * rows, :] = o.astype(o_ref.dtype)


def _mla_attn(q, k, v, n_batch):
    tq = 2048
    nq = SEQ // tq
    lat_blocks = n_batch * SEQ // CTX
    return pl.pallas_call(
        functools.partial(_mla_attn_kernel, sub=8),
        grid=(n_batch, MLA_HEADS, nq),
        in_specs=[
            pl.BlockSpec((tq, MLA_PAD), lambda b, h, i: (b * nq + i, h)),
            pl.BlockSpec((SEQ, MLA_PAD), lambda b, h, i: (b, h)),
            pl.BlockSpec((CTX, MLA_PAD), lambda b, h, i: (lat_blocks + b, h)),
            pl.BlockSpec((SEQ, MLA_V), lambda b, h, i: (b, h)),
            pl.BlockSpec((CTX, MLA_V), lambda b, h, i: (lat_blocks + b, h)),
        ],
        out_specs=pl.BlockSpec((tq, MLA_V), lambda b, h, i: (b * nq + i, h)),
        out_shape=jax.ShapeDtypeStruct((n_batch * SEQ, HALF), BF16),
        compiler_params=_cparams(("arbitrary", "arbitrary", "arbitrary")),
        name="mla_attn",
    )(q, k, k, v, v)


def _ctx_attn_kernel(q_ref, k_ref, v_ref, o_ref):
    o_ref[...] = _softmax_pv(_dot_nt(q_ref[...], k_ref[...]), v_ref[...]).astype(o_ref.dtype)


def _mla_ctx_attn(q, k, v, n_batch):
    lat_blocks = n_batch * SEQ // CTX
    return pl.pallas_call(
        _ctx_attn_kernel,
        grid=(n_batch, MLA_HEADS),
        in_specs=[
            pl.BlockSpec((CTX, MLA_PAD), lambda b, h: (lat_blocks + b, h)),
            pl.BlockSpec((CTX, MLA_PAD), lambda b, h: (lat_blocks + b, h)),
            pl.BlockSpec((CTX, MLA_V), lambda b, h: (lat_blocks + b, h)),
        ],
        out_specs=pl.BlockSpec((CTX, MLA_V), lambda b, h: (b, h)),
        out_shape=jax.ShapeDtypeStruct((n_batch * CTX, HALF), BF16),
        compiler_params=_cparams(("arbitrary", "arbitrary")),
        name="mla_ctx_attn",
    )(q, k, v)


def _na_key_row0(rg):
    return int(np.clip(NA_QROWS * rg - NA_WIN_ROWS // 2, 0, GRID_H - NA_KROWS))


def _na_band0(n):
    return int(np.clip(NA_QCOLS * n - NA_WIN_COLS // 2, 0, GRID_W - NA_KCOLS))


def _na_offsets(q0, k0, nq, nk, win, extent):
    qpos = q0 + np.arange(nq)[:, None]
    kpos = k0 + np.arange(nk)[None, :]
    start = np.clip(qpos - win // 2, 0, extent - win)
    valid = (kpos >= start) & (kpos < start + win)
    return np.where(valid, kpos - qpos + win - 1, 2 * win - 1)


def _na_block_classes():
    rows = [_na_offsets(NA_QROWS * g, _na_key_row0(g), NA_QROWS, NA_KROWS, NA_WIN_ROWS, GRID_H) for g in range(NA_NRG)]
    cols = [_na_offsets(NA_QCOLS * n, _na_band0(n), NA_QCOLS, NA_KCOLS, NA_WIN_COLS, GRID_W) for n in range(NA_NCB)]

    def classes(pats):
        reps, ids = [], []
        for p in pats:
            for k, r in enumerate(reps):
                if np.array_equal(p, r):
                    ids.append(k)
                    break
            else:
                ids.append(len(reps))
                reps.append(p)
        return np.stack(reps), ids

    return classes(rows), classes(cols)


def _na_bias_table(rpb):
    (r_off, _), (c_off, _) = _na_block_classes()
    ext = jnp.pad(rpb * LOG2E, ((0, 0), (0, 1), (0, 1)), constant_values=NEG_BIG)
    r_sel = jnp.asarray(r_off[..., None] == np.arange(2 * NA_WIN_ROWS), F32)
    c_sel = jnp.asarray(c_off[..., None] == np.arange(2 * NA_WIN_COLS), F32)
    bias = jnp.einsum('gika,hab,njcb->hgnijkc', r_sel, ext, c_sel, precision=lax.Precision.HIGHEST)
    return bias.reshape(NA_HEADS, r_off.shape[0] * c_off.shape[0], NA_QROWS * NA_QCOLS, NA_KROWS * NA_KCOLS)


def _na_kernel(q_ref, k_ref, v_ref, qc_ref, kc_ref, vc_ref, gq_ref, gk_ref, bias_ref, o_ref, oc_ref,
               qn_s, kn_s, sw_s, sc_s, pw_s, pc_s, *, scale):
    gq, gk = gq_ref[...], gk_ref[...]
    qn_s[...] = _rms(q_ref[...], gq) * scale
    kn_s[...] = _rms(k_ref[...], gk)
    kc = _rms(kc_ref[...], gk).astype(BF16)
    vc = vc_ref[...].astype(BF16)
    (_, row_cls), (col_reps, col_cls) = _na_block_classes()
    blocks = [(rg, n) for rg in range(NA_NRG) for n in range(NA_NCB)]

    def q_rows(rg, n):
        return [(NA_QROWS * rg + i) * GRID_W + NA_QCOLS * n for i in range(NA_QROWS)]

    def k_rows(rg, n):
        return [(_na_key_row0(rg) + kr) * GRID_W + _na_band0(n) for kr in range(NA_KROWS)]

    for b, (rg, n) in enumerate(blocks):
        qb = jnp.concatenate([qn_s[pl.ds(t, NA_QCOLS), :] for t in q_rows(rg, n)], axis=0).astype(BF16)
        kb = jnp.concatenate([kn_s[pl.ds(t, NA_KCOLS), :] for t in k_rows(rg, n)], axis=0).astype(BF16)
        sw_s[b] = _dot_nt(qb, kb) + bias_ref[row_cls[rg] * len(col_reps) + col_cls[n]]
        sc_s[b] = _dot_nt(qb, kc)
    denoms = []
    for b in range(len(blocks)):
        s1, s2 = sw_s[b], sc_s[b]
        m = jnp.maximum(jnp.max(s1, axis=-1, keepdims=True), jnp.max(s2, axis=-1, keepdims=True))
        p1, p2 = jnp.exp2(s1 - m), jnp.exp2(s2 - m)
        denoms.append(jnp.sum(p1, axis=-1, keepdims=True) + jnp.sum(p2, axis=-1, keepdims=True))
        pw_s[b] = p1.astype(BF16)
        pc_s[b] = p2.astype(BF16)
    for b, (rg, n) in enumerate(blocks):
        vb = jnp.concatenate([v_ref[pl.ds(t, NA_KCOLS), :] for t in k_rows(rg, n)], axis=0).astype(BF16)
        o = ((_dot(pw_s[b], vb) + _dot(pc_s[b], vc)) / denoms[b]).astype(o_ref.dtype)
        for i, t in enumerate(q_rows(rg, n)):
            o_ref[pl.ds(t, NA_QCOLS), :] = o[i * NA_QCOLS:(i + 1) * NA_QCOLS]
    qc = (_rms(qc_ref[...], gq) * scale).astype(BF16)
    oc_ref[...] = _softmax_pv(_dot_nt(qc, kc), vc).astype(oc_ref.dtype)


def _na_attn(proj, g_q, g_k, bias, n_batch):
    base = OFF_NA // NA_DH
    lat_blocks = n_batch * SEQ // CTX

    def lat(part):
        return pl.BlockSpec((SEQ, NA_DH), lambda h, b: (b, base + part * NA_HEADS + h))

    def cx(part):
        return pl.BlockSpec((CTX, NA_DH), lambda h, b: (lat_blocks + b, base + part * NA_HEADS + h))

    const = lambda h, b: (0, 0)
    nblk, nq, nk = bias.shape[1:]
    return pl.pallas_call(
        functools.partial(_na_kernel, scale=NA_DH ** -0.5 * LOG2E),
        grid=(NA_HEADS, n_batch),
        in_specs=[lat(0), lat(1), lat(2), cx(0), cx(1), cx(2),
                  pl.BlockSpec((1, NA_DH), const), pl.BlockSpec((1, NA_DH), const),
                  pl.BlockSpec((None, nblk, nq, nk), lambda h, b: (h, 0, 0, 0))],
        out_specs=[pl.BlockSpec((SEQ, NA_DH), lambda h, b: (b, h)),
                   pl.BlockSpec((CTX, NA_DH), lambda h, b: (b, h))],
        out_shape=[jax.ShapeDtypeStruct((n_batch * SEQ, HALF), BF16),
                   jax.ShapeDtypeStruct((n_batch * CTX, HALF), BF16)],
        scratch_shapes=[pltpu.VMEM((SEQ, NA_DH), F32), pltpu.VMEM((SEQ, NA_DH), F32),
                        pltpu.VMEM((NA_NRG * NA_NCB, nq, nk), F32), pltpu.VMEM((NA_NRG * NA_NCB, nq, CTX), F32),
                        pltpu.VMEM((NA_NRG * NA_NCB, nq, nk), BF16), pltpu.VMEM((NA_NRG * NA_NCB, nq, CTX), BF16)],
        compiler_params=_cparams(("arbitrary", "arbitrary")),
        name="na_attn",
    )(proj, proj, proj, proj, proj, proj, g_q.reshape(1, NA_DH), g_k.reshape(1, NA_DH), bias)


CONV_TM = 256
CONV_HALO = 8


def _conv_kernel(prev_ref, cur_ref, next_ref, w_ref, b_ref, o_ref, pad_s, *, n_lat_tiles):
    i = pl.program_id(0)
    per = SEQ // CONV_TM
    is_ctx = i >= n_lat_tiles
    at_start = is_ctx | (i % per == 0)
    at_end = is_ctx | (i % per == per - 1)
    pad_s[0:CONV_HALO, :] = jnp.where(at_start, 0.0, prev_ref[...])
    pad_s[CONV_HALO:CONV_HALO + CONV_TM, :] = cur_ref[...]
    pad_s[CONV_HALO + CONV_TM:, :] = jnp.where(at_end, 0.0, next_ref[...])
    acc = jnp.broadcast_to(b_ref[...], cur_ref.shape)
    for k in range(SSD_CONV):
        acc = acc + w_ref[k:k + 1, :] * pad_s[pl.ds(CONV_HALO - SSD_CONV // 2 + k, CONV_TM), :]
    o_ref[...] = _silu(acc)


def _ssd_conv(proj, conv_w, conv_b, n_lat):
    rows = proj.shape[0]
    tc = 1024
    assert OFF_XBC % tc == 0 and SSD_XBC % tc == 0
    nlt = n_lat // CONV_TM
    cb0 = OFF_XBC // tc
    hb = CONV_TM // CONV_HALO
    last = rows // CONV_HALO - 1
    return pl.pallas_call(
        functools.partial(_conv_kernel, n_lat_tiles=nlt),
        grid=(rows // CONV_TM, SSD_XBC // tc),
        in_specs=[
            pl.BlockSpec((CONV_HALO, tc), lambda i, j: (jnp.maximum(i * hb - 1, 0), cb0 + j)),
            pl.BlockSpec((CONV_TM, tc), lambda i, j: (i, cb0 + j)),
            pl.BlockSpec((CONV_HALO, tc), lambda i, j: (jnp.minimum((i + 1) * hb, last), cb0 + j)),
            pl.BlockSpec((SSD_CONV, tc), lambda i, j: (0, j)),
            pl.BlockSpec((1, tc), lambda i, j: (0, j)),
        ],
        out_specs=pl.BlockSpec((CONV_TM, tc), lambda i, j: (i, j)),
        out_shape=jax.ShapeDtypeStruct((rows, SSD_XBC), F32),
        scratch_shapes=[pltpu.VMEM((CONV_TM + 2 * CONV_HALO, tc), F32)],
        compiler_params=_cparams(("arbitrary", "arbitrary")),
        name="ssd_conv",
    )(proj, proj, proj, conv_w, conv_b.reshape(1, SSD_XBC))


def _ssd_scan_kernel(*refs, reverse, final, streams):
    brow_ref, bcol_ref, acol_ref, e1_ref = refs[:4]
    pos, per = 4, 3
    if final:
        d_ref, g_ref, yprev_ref = refs[4:7]
        pos, per = 7, 4
    y_ref, state_s = refs[pos + per * streams:]

    @pl.when(pl.program_id(1) == 0)
    def _():
        state_s[...] = jnp.zeros_like(state_s)

    for s in range(streams):
        r = refs[pos + per * s:pos + per * (s + 1)]
        if final:
            rest = (yprev_ref.at[s], r[3], d_ref, g_ref, y_ref.at[s], state_s.at[s])
        else:
            rest = (y_ref.at[s], state_s.at[s])
        _ssd_chunk(r[0], r[1], r[2], brow_ref, bcol_ref, acol_ref, e1_ref, *rest, reverse=reverse, final=final)


def _ssd_chunk(xbc_ref, misc_ref, dtt_ref, brow_ref, bcol_ref, acol_ref, e1_ref, *rest, reverse, final):
    if final:
        yprev_ref, z_ref, d_ref, g_ref, y_ref, state_s = rest
    else:
        y_ref, state_s = rest
    L = SSD_CHUNK

    li = lax.broadcasted_iota(jnp.int32, (L, L), 0)
    si = lax.broadcasted_iota(jnp.int32, (L, L), 1)
    lane = lax.broadcasted_iota(jnp.int32, (L, 128), 1)
    if reverse:
        cum_r = jnp.where(li >= si, 1.0, 0.0).astype(BF16)
        causal = li <= si
    else:
        cum_r = jnp.where(li <= si, 1.0, 0.0).astype(BF16)
        causal = li >= si

    dt_exp = _dot_sel_right(_softplus(misc_ref[...] + brow_ref[...]), e1_ref[...])
    dt_row = _softplus(dtt_ref[...] + bcol_ref[...])
    ac_row = _dot_sel_right(dt_row * (-jnp.exp(acol_ref[...])), cum_r)
    ac_col = [jnp.broadcast_to(ac_row[hd:hd + 1, :], (L, L)).T for hd in range(SSD_HEADS)]
    ac_exp = jnp.concatenate([jnp.where(lane < SSD_P, ac_col[2 * p], ac_col[2 * p + 1])
                              for p in range(SSD_HEADS // 2)], axis=1)

    end = 0 if reverse else L - 1
    tot_exp = ac_exp[end:end + 1, :]
    xbc = xbc_ref[...]
    xdt = xbc[:, :HALF] * dt_exp
    xw = (xdt * jnp.exp(tot_exp - ac_exp)).astype(BF16)
    xdt_b = xdt.astype(BF16)
    eac = jnp.exp(ac_exp)
    chunk_decay = jnp.exp(tot_exp)
    gw = SSD_HEADS // SSD_G * SSD_P
    for g in range(SSD_G):
        gs = slice(g * gw, (g + 1) * gw)
        b_t = xbc[:, HALF + g * SSD_N:HALF + (g + 1) * SSD_N].T.astype(BF16)
        c_g = xbc[:, HALF + SSD_G * SSD_N + g * SSD_N:HALF + SSD_G * SSD_N + (g + 1) * SSD_N].astype(BF16)
        cb = _dot(c_g, b_t)
        st = state_s[:, gs]
        y_off = _dot(c_g, st.astype(BF16)) * eac[:, gs]
        state_s[:, gs] = st * chunk_decay[:, gs] + _dot(b_t, xw[:, gs])
        y_pairs = []
        for pair in range(2):
            ha = 4 * g + 2 * pair
            xp = xdt_b[:, ha * SSD_P:ha * SSD_P + 128]
            ys = []
            for hd in (ha, ha + 1):
                seg = ac_col[hd] - ac_row[hd:hd + 1, :]
                m = (cb * jnp.exp(jnp.where(causal, seg, -jnp.inf))).astype(BF16)
                ys.append(_dot(m, xp))
            y_pairs.append(jnp.where(lane < SSD_P, ys[0], ys[1]))
        y_g = jnp.concatenate(y_pairs, axis=1) + y_off
        if final:
            y_all = yprev_ref[:, gs] + y_g + d_ref[:, gs] * xbc[:, gs]
            y_ref[:, gs] = _rms(y_all * _silu(z_ref[:, gs]), g_ref[:, gs]).astype(y_ref.dtype)
        else:
            y_ref[:, gs] = y_g


def _ssd_streams(n_batch):
    streams = 2 if n_batch % 2 == 0 else 1
    return streams, n_batch // streams


def _ssd_scan(xbc_act, proj, dt_t, dt_bias, a_log, direction, n_batch, final=None):
    reverse = direction == 1
    nlc, ncc = SEQ // SSD_CHUNK, CTX // SSD_CHUNK
    lat_chunks = n_batch * nlc
    streams, groups = _ssd_streams(n_batch)

    def pos(j):
        return jnp.where(j < ncc, ncc - 1 - j, 2 * ncc + nlc - 1 - j) if reverse else j

    def blk(s):
        def f(g, j):
            b, p = s * groups + g, pos(j)
            return jnp.where(p < ncc, lat_chunks + b * ncc + p, b * nlc + p - ncc)
        return f

    lane0 = MISC_DT + SSD_HEADS * direction
    b_row = jnp.zeros((1, 128), F32).at[0, lane0:lane0 + SSD_HEADS].set(dt_bias[direction])
    b_col, a_col = dt_bias[direction].reshape(SSD_HEADS, 1), a_log[direction].reshape(SSD_HEADS, 1)
    e1 = jnp.asarray((np.arange(128)[:, None] - lane0) == (np.arange(HALF)[None] // SSD_P), BF16)
    const = lambda g, j: (0, 0)
    y_spec = pl.BlockSpec((None, None, streams, SSD_CHUNK, HALF), lambda g, j: (g, pos(j), 0, 0, 0))
    in_specs = [pl.BlockSpec((1, 128), const), pl.BlockSpec((SSD_HEADS, 1), const),
                pl.BlockSpec((SSD_HEADS, 1), const), pl.BlockSpec((128, HALF), const)]
    args = [b_row, b_col, a_col, e1]
    if final is not None:
        y_prev, d_skip, norm_g = final
        in_specs += [pl.BlockSpec((1, HALF), const), pl.BlockSpec((1, HALF), const), y_spec]
        args += [jnp.repeat(d_skip, SSD_P).reshape(1, HALF), norm_g.reshape(1, HALF), y_prev]
    for s in range(streams):
        f = blk(s)
        in_specs += [pl.BlockSpec((SSD_CHUNK, SSD_XBC), lambda g, j, f=f: (f(g, j), 0)),
                     pl.BlockSpec((SSD_CHUNK, 128), lambda g, j, f=f: (f(g, j), OFF_MISC // 128)),
                     pl.BlockSpec((SSD_HEADS, SSD_CHUNK), lambda g, j, f=f: (direction, f(g, j)))]
        args += [xbc_act, proj, dt_t]
        if final is not None:
            in_specs.append(pl.BlockSpec((SSD_CHUNK, HALF), lambda g, j, f=f: (f(g, j), OFF_Z // HALF)))
            args.append(proj)
    return pl.pallas_call(
        functools.partial(_ssd_scan_kernel, reverse=reverse, final=final is not None, streams=streams),
        grid=(groups, ncc + nlc),
        in_specs=in_specs,
        out_specs=y_spec,
        out_shape=jax.ShapeDtypeStruct((groups, ncc + nlc, streams, SSD_CHUNK, HALF), F32 if final is None else BF16),
        scratch_shapes=[pltpu.VMEM((streams, SSD_N, HALF), F32)],
        compiler_params=_cparams(("arbitrary", "arbitrary")),
        name="ssd_scan_bwd" if reverse else "ssd_scan_fwd",
    )(*args)


def _w_in_pieces():
    o = np.cumsum((0, MLA_HEADS * MLA_QK, MLA_RANK, MLA_ROPE, HALF, SSD_XBC, 2 * SSD_HEADS, 3 * HALF, 3 * D))
    seg = lambda k: (int(o[k]), int(o[k + 1] - o[k]))
    q_nope = [(hd * MLA_QK, MLA_NOPE) for hd in range(MLA_HEADS)]
    q_rope = [(hd * MLA_QK + MLA_NOPE, MLA_ROPE) for hd in range(MLA_HEADS)]
    return [seg(7), seg(6), seg(4), *q_nope, seg(3), *q_rope, seg(1), seg(2), seg(5)]


def _permute_kernel(wt_ref, o_ref, rows_s):
    dst = 0
    for a, n in _w_in_pieces():
        rows_s[dst:dst + n, :] = wt_ref[a:a + n, :]
        dst += n
    rows_s[dst:, :] = jnp.zeros((rows_s.shape[0] - dst, rows_s.shape[1]), rows_s.dtype)
    for j in range(N_PROJ // 128):
        cols = slice(j * 128, (j + 1) * 128)
        if j * 128 < rows_s.shape[0]:
            o_ref[:, cols] = rows_s[cols, :].T.astype(o_ref.dtype)
        else:
            o_ref[:, cols] = jnp.zeros((o_ref.shape[0], 128), o_ref.dtype)


def _permute_w_in(w):
    depth, _, n_in = w.shape
    tk = 128
    used = OFF_MISC + 128
    assert sum(n for _, n in _w_in_pieces()) == n_in <= used
    return pl.pallas_call(
        _permute_kernel,
        grid=(depth, D // tk),
        in_specs=[pl.BlockSpec((None, n_in, tk), lambda l, i: (l, 0, i))],
        out_specs=pl.BlockSpec((None, tk, N_PROJ), lambda l, i: (l, i, 0)),
        out_shape=jax.ShapeDtypeStruct((depth, D, N_PROJ), BF16),
        scratch_shapes=[pltpu.VMEM((used, tk), F32)],
        compiler_params=_cparams(("arbitrary", "arbitrary")),
        name="permute_w_in",
    )(jnp.swapaxes(w, 1, 2))


def _rope_tables():
    t = jnp.arange(SEQ)
    pos = jnp.stack([t // GRID_W, t % GRID_W], axis=-1).astype(F32)
    n_freq = MLA_ROPE // 4
    inv_freq = ROPE_THETA ** (-jnp.arange(n_freq, dtype=F32) / n_freq)
    ang = pos[:, :, None] * inv_freq
    cos, sin = jnp.cos(ang), jnp.sin(ang)
    c64 = jnp.concatenate([cos[:, 0], cos[:, 0], cos[:, 1], cos[:, 1]], axis=-1)
    s64 = jnp.concatenate([-sin[:, 0], sin[:, 0], -sin[:, 1], sin[:, 1]], axis=-1)
    ident = 256
    cos_t = jnp.concatenate([jnp.tile(c64, (1, 2)), jnp.ones((ident, 128), F32)], axis=0)
    sin_t = jnp.concatenate([jnp.tile(s64, (1, 2)), jnp.zeros((ident, 128), F32)], axis=0)
    return cos_t, sin_t


def kernel(x, c, ctx, c_ctx, w_ada, b_ada, norm_g, ffn1_w_gate, ffn1_w_up, ffn1_w_down, ffn2_w_gate, ffn2_w_up, ffn2_w_down, w_in, mla_kv_norm_g, mla_w_uk, mla_w_uv, mla_q_norm_g, mla_k_norm_g, ssd_conv_w, ssd_conv_b, ssd_a_log, ssd_dt_bias, ssd_d, ssd_norm_g, na_q_norm_g, na_k_norm_g, na_rpb, w_branch, w_out):
    nb = x.shape[0]
    assert x.shape[1:] == (SEQ, D) and ctx.shape[1:] == (CTX, D) and nb < 16
    n_lat, n_ctx = nb * SEQ, nb * CTX
    n_tok = n_lat + n_ctx
    depth = w_ada.shape[0]

    cond =jnp.concatenate([c, c_ctx[None], jnp.zeros((16 - nb - 1, D), F32)], axis=0)
    m_all = _adaln(cond, w_ada, b_ada)
    cos_t, sin_t = _rope_tables()

    bf = lambda w: w.astype(BF16)
    ffn1 = (bf(ffn1_w_gate), bf(ffn1_w_up), bf(ffn1_w_down))
    ffn2 = (bf(ffn2_w_gate), bf(ffn2_w_up), bf(ffn2_w_down))
    w_in_p, w_uk, w_uv, w_br, w_o = _permute_w_in(w_in), bf(mla_w_uk), bf(mla_w_uv), bf(w_branch), bf(w_out)

    m4s = [m_all[l].reshape(16, N_MOD, 1, D) for l in range(depth)]
    h, u = _normmod(x.reshape(n_lat, D), ctx.reshape(n_ctx, D), norm_g[0, 0], m4s[0], base=0, n_batch=nb)
    for l in range(depth):
        need_ctx = l < depth - 1
        rows_out = n_tok if need_ctx else n_lat
        m4 = m4s[l]
        mod = dict(m4=m4, n_lat=n_lat, n_batch=nb)

        h, u = _ffn(u, h, n_tok, l, *ffn1, base=0, nxt=(norm_g[l, 1], m4, 3), **mod)
        proj = _in_proj(u, w_in_p, l)

        q, k, v = _mla_prep(proj, cos_t, sin_t, l, w_uk, w_uv, mla_kv_norm_g[l],
                            mla_q_norm_g[l], mla_k_norm_g[l], n_lat)
        mla = _mla_attn(q, k, v, nb)

        na, na_c = _na_attn(proj, na_q_norm_g[l], na_k_norm_g[l], _na_bias_table(na_rpb[l]), nb)

        xbc_act = _ssd_conv(proj, ssd_conv_w[l], ssd_conv_b[l], n_lat)
        dt_t = proj[:, OFF_MISC + MISC_DT:OFF_MISC + MISC_DT + 2 * SSD_HEADS].T
        y_f = _ssd_scan(xbc_act, proj, dt_t, ssd_dt_bias[l], ssd_a_log[l], 0, nb)
        ssd = _ssd_scan(xbc_act, proj, dt_t, ssd_dt_bias[l], ssd_a_log[l], 1, nb,
                        final=(y_f, ssd_d[l], ssd_norm_g[l]))

        mla_c = _mla_ctx_attn(q, k, v, nb) if need_ctx else na_c
        h, u = _mix(((mla, mla_c), ssd, (na, na_c)), proj, l, w_br, w_o, h, rows_out, n_g=norm_g[l, 2], **mod)

        if need_ctx:
            h, u = _ffn(u, h, rows_out, l, *ffn2, base=6, nxt=(norm_g[l + 1, 0], m4s[l + 1], 0), **mod)
        else:
            h = _ffn(u, h, rows_out, l, *ffn2, base=6, **mod)

    return h[:n_lat].reshape(nb, SEQ, D)
```

```python
import functools
import math

import numpy as np
import jax
import jax.numpy as jnp
from jax import lax
from jax.experimental import pallas as pl
from jax.experimental.pallas import tpu as pltpu

F32 = jnp.float32
BF16 = jnp.bfloat16

D = 2048
SEQ = 2048
CTX = 256
DEPTH = 2
GRID_W = 64
GRID_H = SEQ // GRID_W
EPS = 1e-6
ROPE_THETA = 10000.0
N_MOD = 9
D_FF = 5632
HALF = D // 2

MLA_NOPE, MLA_ROPE, MLA_V = 128, 64, 128
MLA_QK = MLA_NOPE + MLA_ROPE
MLA_HEADS = 8
MLA_RANK = 512
MLA_PAD = 256

SSD_HEADS, SSD_P, SSD_G, SSD_N, SSD_CONV, SSD_CHUNK = 16, 64, 4, 128, 5, 128
SSD_XBC = HALF + 2 * SSD_G * SSD_N

NA_HEADS, NA_DH = 8, 128
NA_WIN_ROWS, NA_WIN_COLS = 8, 16
NA_QROWS, NA_QCOLS = 8, 16
NA_KROWS, NA_KCOLS = NA_QROWS + NA_WIN_ROWS - 1, NA_QCOLS + NA_WIN_COLS
NA_NRG, NA_NCB = GRID_H // NA_QROWS, GRID_W // NA_QCOLS
NEG_BIG = -1e30

OFF_GATE = 0
OFF_NA = 3 * D
OFF_XBC = OFF_NA + 3 * HALF
OFF_QNOPE = OFF_XBC + SSD_XBC
OFF_Z = OFF_QNOPE + HALF
OFF_QROPE = OFF_Z + HALF
OFF_CKV = OFF_QROPE + MLA_HEADS * MLA_ROPE
OFF_MISC = OFF_CKV + MLA_RANK
N_PROJ = 29 * 512
MISC_DT = MLA_ROPE

VMEM_LIMIT = 56 * 1024 * 1024


def _cparams(sem):
    return pltpu.CompilerParams(dimension_semantics=sem, vmem_limit_bytes=VMEM_LIMIT)


def _row_tile(rows, preferred):
    tm = preferred
    while rows % tm:
        tm //= 2
    return tm


def _dot(a, b):
    return jnp.dot(a, b, preferred_element_type=F32)


def _dot_nt(a, b):
    return lax.dot_general(a, b, (((1,), (1,)), ((), ())), preferred_element_type=F32)


def _split3(x):
    hi = x.astype(BF16)
    r1 = x - hi.astype(F32)
    mid = r1.astype(BF16)
    lo = (r1 - mid.astype(F32)).astype(BF16)
    return hi, mid, lo


def _dot_sel_right(x, m01):
    hi, mid, lo = _split3(x)
    return _dot(hi, m01) + _dot(mid, m01) + _dot(lo, m01)


def _dot_sel_left(m01, x):
    hi, mid, lo = _split3(x)
    return _dot(m01, hi) + _dot(m01, mid) + _dot(m01, lo)


def _silu(x):
    return x * jax.nn.sigmoid(x)


def _softplus(x):
    return jnp.maximum(x, 0.0) + jnp.log1p(jnp.exp(-jnp.abs(x)))


def _rms(x, g):
    return x * lax.rsqrt(jnp.mean(x * x, axis=-1, keepdims=True) + EPS) * g


def _ada_kernel(c_ref, w_ref, b_ref, o_ref):
    s = _silu(c_ref[...]).astype(BF16)
    o_ref[...] = _dot(s, w_ref[...].astype(BF16)) + b_ref[...]


def _adaln(cond, w_ada, b_ada):
    depth = w_ada.shape[0]
    n = N_MOD * D
    tn = 1024
    return pl.pallas_call(
        _ada_kernel,
        grid=(depth, n // tn),
        in_specs=[
            pl.BlockSpec((16, D), lambda l, j: (0, 0)),
            pl.BlockSpec((None, D, tn), lambda l, j: (l, 0, j)),
            pl.BlockSpec((None, 1, tn), lambda l, j: (l, 0, j)),
        ],
        out_specs=pl.BlockSpec((None, 16, tn), lambda l, j: (l, 0, j)),
        out_shape=jax.ShapeDtypeStruct((depth, 16, n), F32),
        compiler_params=_cparams(("arbitrary", "arbitrary")),
        name="adaln",
    )(cond, w_ada, b_ada.reshape(depth, 1, n))


def _mod_spec(tm, n_lat_tiles, n_batch, k):
    per = SEQ // tm

    def imap(i, *_):
        return (jnp.where(i < n_lat_tiles, i // per, n_batch), k, 0, 0)

    return pl.BlockSpec((None, None, 1, D), imap)


def _normmod_kernel(x_ref, c_ref, g_ref, shift_ref, scale_ref, h_ref, o_ref, *, n_lat_tiles):
    is_lat = pl.program_id(0) < n_lat_tiles
    h = jnp.where(is_lat, x_ref[...], c_ref[...])
    h_ref[...] = h
    y = _rms(h, g_ref[...])
    o_ref[...] = (y * (1.0 + scale_ref[...]) + shift_ref[...]).astype(o_ref.dtype)


def _normmod(x2d, ctx2d, g, m4, base, n_batch):
    tm = 256
    n_lat, n_ctx = x2d.shape[0], ctx2d.shape[0]
    nlt = n_lat // tm
    rows = n_lat + n_ctx
    row = pl.BlockSpec((tm, D), lambda i: (i, 0))
    return pl.pallas_call(
        functools.partial(_normmod_kernel, n_lat_tiles=nlt),
        grid=(rows // tm,),
        in_specs=[
            pl.BlockSpec((tm, D), lambda i: (jnp.minimum(i, nlt - 1), 0)),
            pl.BlockSpec((tm, D), lambda i: (jnp.maximum(i - nlt, 0), 0)),
            pl.BlockSpec((1, D), lambda i: (0, 0)),
            _mod_spec(tm, nlt, n_batch, base),
            _mod_spec(tm, nlt, n_batch, base + 1),
        ],
        out_specs=[row, row],
        out_shape=[jax.ShapeDtypeStruct((rows, D), F32), jax.ShapeDtypeStruct((rows, D), BF16)],
        compiler_params=_cparams(("arbitrary",)),
        name="normmod",
    )(x2d, ctx2d, g.reshape(1, D), m4, m4)


def _ffn_kernel(u_ref, wg_ref, wu_ref, wd_ref, h_ref, gate_ref, *rest, nf, with_next):
    if with_next:
        ng_ref, nshift_ref, nscale_ref, o_ref, un_ref, acc_ref = rest
    else:
        o_ref, acc_ref = rest
    f = pl.program_id(1)

    @pl.when(f == 0)
    def _():
        acc_ref[...] = jnp.zeros_like(acc_ref)

    u = u_ref[...]
    g = _dot(u, wg_ref[...])
    up = _dot(u, wu_ref[...])
    act = (_silu(g) * up).astype(BF16)
    acc_ref[...] += _dot(act, wd_ref[...])

    @pl.when(f == nf - 1)
    def _():
        h_new = h_ref[...] + (0.5 * gate_ref[...]) * acc_ref[...]
        o_ref[...] = h_new
        if with_next:
            y = _rms(h_new, ng_ref[...])
            un_ref[...] = (y * (1.0 + nscale_ref[...]) + nshift_ref[...]).astype(un_ref.dtype)


def _ffn(u, h, rows, layer, wg, wu, wd, m4, base, n_lat, n_batch, nxt=None):
    tm, tf = _row_tile(rows, 512), 512
    nf = D_FF // tf
    nlt = n_lat // tm
    row_spec = pl.BlockSpec((tm, D), lambda i, f: (i, 0))
    in_specs = [
        row_spec,
        pl.BlockSpec((None, D, tf), lambda i, f: (layer, 0, f)),
        pl.BlockSpec((None, D, tf), lambda i, f: (layer, 0, f)),
        pl.BlockSpec((None, tf, D), lambda i, f: (layer, f, 0)),
        row_spec,
        _mod_spec(tm, nlt, n_batch, base + 2),
    ]
    args = [u, wg, wu, wd, h, m4]
    out_specs, out_shape = row_spec, jax.ShapeDtypeStruct((rows, D), F32)
    if nxt is not None:
        n_g, n_m4, n_base = nxt
        in_specs += [pl.BlockSpec((1, D), lambda i, f: (0, 0)),
                     _mod_spec(tm, nlt, n_batch, n_base), _mod_spec(tm, nlt, n_batch, n_base + 1)]
        args += [n_g.reshape(1, D), n_m4, n_m4]
        out_specs = [row_spec, row_spec]
        out_shape = [out_shape, jax.ShapeDtypeStruct((rows, D), BF16)]
    return pl.pallas_call(
        functools.partial(_ffn_kernel, nf=nf, with_next=nxt is not None),
        grid=(rows // tm, nf),
        in_specs=in_specs,
        out_specs=out_specs,
        out_shape=out_shape,
        scratch_shapes=[pltpu.VMEM((tm, D), F32)],
        compiler_params=_cparams(("arbitrary", "arbitrary")),
        name="ffn",
    )(*args)


def _mm_kernel(a_ref, w_ref, o_ref):
    o_ref[...] = _dot(a_ref[...], w_ref[...]).astype(o_ref.dtype)


def _in_proj(u, w, layer):
    rows = u.shape[0]
    tm, tn = _row_tile(rows, 2048), 512
    return pl.pallas_call(
        _mm_kernel,
        grid=(rows // tm, N_PROJ // tn),
        in_specs=[pl.BlockSpec((tm, D), lambda i, j: (i, 0)),
                  pl.BlockSpec((None, D, tn), lambda i, j: (layer, 0, j))],
        out_specs=pl.BlockSpec((tm, tn), lambda i, j: (i, j)),
        out_shape=jax.ShapeDtypeStruct((rows, N_PROJ), F32),
        compiler_params=_cparams(("arbitrary", "arbitrary")),
        name="in_proj",
    )(u, w)


def _mix_kernel(b0_ref, b0c_ref, b1_ref, b2_ref, b2c_ref, wb_ref, g0_ref, g1_ref, g2_ref, wo_ref, h_ref, gate_ref,
                ng_ref, nshift_ref, nscale_ref, o_ref, un_ref, *, n_lat_tiles):
    is_lat = pl.program_id(0) < n_lat_tiles
    b0 = jnp.where(is_lat, b0_ref[...], b0c_ref[...])
    b2 = jnp.where(is_lat, b2_ref[...], b2c_ref[...])
    acc = jax.nn.sigmoid(g0_ref[...]) * _dot(b0, wb_ref[0])
    b1 = b1_ref[...].reshape(b0.shape)
    acc = acc + jax.nn.sigmoid(g1_ref[...]) * _dot(b1, wb_ref[1])
    acc = acc + jax.nn.sigmoid(g2_ref[...]) * _dot(b2, wb_ref[2])
    h_new = h_ref[...] + gate_ref[...] * _dot(acc.astype(BF16), wo_ref[...])
    o_ref[...] = h_new
    y = _rms(h_new, ng_ref[...])
    un_ref[...] = (y * (1.0 + nscale_ref[...]) + nshift_ref[...]).astype(un_ref.dtype)


def _mix(branches, proj, layer, w_branch, w_out, h, rows, m4, n_lat, n_batch, n_g):
    tm = 256
    nlt = n_lat // tm
    row = lambda i: (i, 0)
    once = pl.Buffered(1)
    cpt = tm // SSD_CHUNK
    per = SEQ // tm
    _, groups = _ssd_streams(n_batch)

    def ssd_map(i):
        b = jnp.where(i < nlt, i // per, i - nlt)
        pblk = jnp.where(i < nlt, CTX // tm + i % per, 0)
        return (b % groups, pblk, b // groups, 0, 0)

    (mla, mla_c), ssd, (na, na_c) = branches
    lat_rows = pl.BlockSpec((tm, HALF), lambda i: (jnp.minimum(i, nlt - 1), 0))
    ctx_rows = pl.BlockSpec((tm, HALF), lambda i: (jnp.maximum(i - nlt, 0), 0))
    return pl.pallas_call(
        functools.partial(_mix_kernel, n_lat_tiles=nlt),
        grid=(rows // tm,),
        in_specs=[lat_rows, ctx_rows, pl.BlockSpec((None, cpt, None, SSD_CHUNK, HALF), ssd_map), lat_rows, ctx_rows]
        + [pl.BlockSpec((None, 3, HALF, D), lambda i: (layer, 0, 0, 0), pipeline_mode=once)]
        + [pl.BlockSpec((tm, D), functools.partial(lambda i, k: (i, k), k=k)) for k in range(3)]
        + [pl.BlockSpec((None, D, D), lambda i: (layer, 0, 0), pipeline_mode=once),
           pl.BlockSpec((tm, D), row),
           _mod_spec(tm, nlt, n_batch, 5),
           pl.BlockSpec((1, D), lambda i: (0, 0)),
           _mod_spec(tm, nlt, n_batch, 6), _mod_spec(tm, nlt, n_batch, 7)],
        out_specs=[pl.BlockSpec((tm, D), row), pl.BlockSpec((tm, D), row)],
        out_shape=[jax.ShapeDtypeStruct((rows, D), F32), jax.ShapeDtypeStruct((rows, D), BF16)],
        compiler_params=_cparams(("arbitrary",)),
        name="mix",
    )(mla, mla_c, ssd, na, na_c, w_branch, proj, proj, proj, w_out, h, m4, n_g.reshape(1, D), m4, m4)


def _rope128(y, cos, sin):
    lane = lax.broadcasted_iota(jnp.int32, y.shape, 1)
    partner = jnp.where((lane % 32) < 16, pltpu.roll(y, 112, 1), pltpu.roll(y, 16, 1))
    return y * cos + partner * sin


def _mla_prep_kernel(qn_ref, qr_ref, ckv_ref, misc_ref, cos_ref, sin_ref, wuk_ref, wuv_ref,
                     gkv_ref, gqn_ref, gqr_ref, gkn_ref, gkr_ref, q_ref, k_ref, v_ref, *, scale):
    cos, sin = cos_ref[...], sin_ref[...]
    lane = lax.broadcasted_iota(jnp.int32, cos.shape, 1)
    low = lane < MLA_ROPE

    ckv = _rms(ckv_ref[...], gkv_ref[...]).astype(BF16)
    k_nope = _dot(ckv, wuk_ref[...])
    v_ref[...] = _dot(ckv, wuv_ref[...]).astype(v_ref.dtype)

    misc = misc_ref[...]
    kr_sq = jnp.where(low, misc * misc, 0.0)
    kr_rot = jnp.where(low, _rope128(misc * gkr_ref[...], cos, sin), 0.0)
    kr_rot_hi = pltpu.roll(kr_rot, MLA_ROPE, 1)

    gqn, gqr, gkn = gqn_ref[...], gqr_ref[...], gkn_ref[...]
    for pair in range(MLA_HEADS // 2):
        qr = qr_ref[:, pair * 128:(pair + 1) * 128]
        qr2 = qr * qr
        rs = []
        for sub in (0, 1):
            hd = 2 * pair + sub
            qn = qn_ref[:, hd * 128:(hd + 1) * 128]
            q_sq = qn * qn + jnp.where(low if sub == 0 else ~low, qr2, 0.0)
            r = lax.rsqrt(jnp.sum(q_sq, axis=-1, keepdims=True) * (1.0 / MLA_QK) + EPS)
            rs.append(r)
            q_ref[:, hd * MLA_PAD:hd * MLA_PAD + 128] = (qn * r * gqn * scale).astype(q_ref.dtype)
            kn = k_nope[:, hd * 128:(hd + 1) * 128]
            rk = lax.rsqrt(jnp.sum(kn * kn + kr_sq, axis=-1, keepdims=True) * (1.0 / MLA_QK) + EPS)
            k_ref[:, hd * MLA_PAD:hd * MLA_PAD + 128] = (kn * rk * gkn).astype(k_ref.dtype)
            k_ref[:, hd * MLA_PAD + 128:(hd + 1) * MLA_PAD] = ((kr_rot if sub == 0 else kr_rot_hi) * rk).astype(k_ref.dtype)
        q_rot = _rope128(qr * jnp.where(low, rs[0], rs[1]) * gqr, cos, sin) * scale
        q_ref[:, (2 * pair) * MLA_PAD + 128:(2 * pair + 1) * MLA_PAD] = jnp.where(low, q_rot, 0.0).astype(q_ref.dtype)
        q_ref[:, (2 * pair + 1) * MLA_PAD + 128:(2 * pair + 2) * MLA_PAD] = jnp.where(low, 0.0, q_rot).astype(q_ref.dtype)


def _mla_prep(proj, cos_t, sin_t, layer, w_uk, w_uv, g_kv, g_q, g_k, n_lat):
    rows = proj.shape[0]
    tm = 256
    nlt = n_lat // tm
    per = SEQ // tm

    def tab_map(i):
        return (jnp.where(i < nlt, i % per, per), 0)

    def row(v):
        return v.reshape(1, -1)

    def rope_gain(g):
        return jnp.concatenate([g[MLA_NOPE:], g[MLA_NOPE:]]).reshape(1, 128)

    const = lambda i: (0, 0)
    return pl.pallas_call(
        functools.partial(_mla_prep_kernel, scale=MLA_QK ** -0.5 * math.log2(math.e)),
        grid=(rows // tm,),
        in_specs=[
            pl.BlockSpec((tm, HALF), lambda i: (i, OFF_QNOPE // HALF)),
            pl.BlockSpec((tm, 512), lambda i: (i, OFF_QROPE // 512)),
            pl.BlockSpec((tm, MLA_RANK), lambda i: (i, OFF_CKV // MLA_RANK)),
            pl.BlockSpec((tm, 128), lambda i: (i, OFF_MISC // 128)),
            pl.BlockSpec((tm, 128), tab_map),
            pl.BlockSpec((tm, 128), tab_map),
            pl.BlockSpec((None, MLA_RANK, HALF), lambda i: (layer, 0, 0)),
            pl.BlockSpec((None, MLA_RANK, HALF), lambda i: (layer, 0, 0)),
            pl.BlockSpec((1, MLA_RANK), const),
            pl.BlockSpec((1, 128), const),
            pl.BlockSpec((1, 128), const),
            pl.BlockSpec((1, 128), const),
            pl.BlockSpec((1, 128), const),
        ],
        out_specs=[
            pl.BlockSpec((tm, MLA_HEADS * MLA_PAD), lambda i: (i, 0)),
            pl.BlockSpec((tm, MLA_HEADS * MLA_PAD), lambda i: (i, 0)),
            pl.BlockSpec((tm, HALF), lambda i: (i, 0)),
        ],
        out_shape=[
            jax.ShapeDtypeStruct((rows, MLA_HEADS * MLA_PAD), BF16),
            jax.ShapeDtypeStruct((rows, MLA_HEADS * MLA_PAD), BF16),
            jax.ShapeDtypeStruct((rows, HALF), BF16),
        ],
        compiler_params=_cparams(("arbitrary",)),
        name="mla_prep",
    )(proj, proj, proj, proj, cos_t, sin_t, w_uk, w_uv, row(g_kv),
      row(g_q[:MLA_NOPE]), rope_gain(g_q), row(g_k[:MLA_NOPE]), rope_gain(g_k))


LOG2E = math.log2(math.e)


def _softmax2_pv(s1, s2, v1, v2):
    m = jnp.maximum(jnp.max(s1, axis=-1, keepdims=True), jnp.max(s2, axis=-1, keepdims=True))
    p1 = jnp.exp2(s1 - m)
    p2 = jnp.exp2(s2 - m)
    denom = jnp.sum(p1, axis=-1, keepdims=True) + jnp.sum(p2, axis=-1, keepdims=True)
    return (_dot(p1.astype(BF16), v1) + _dot(p2.astype(BF16), v2)) / denom


def _softmax_pv(s, v):
    p = jnp.exp2(s - jnp.max(s, axis=-1, keepdims=True))
    return _dot(p.astype(BF16), v) / jnp.sum(p, axis=-1, keepdims=True)


def _mla_attn_kernel(q_ref, kl_ref, kc_ref, vl_ref, vc_ref, o_ref, *, sub):
    kl, kc, vl, vc = kl_ref[...], kc_ref[...], vl_ref[...], vc_ref[...]
    rows = q_ref.shape[0] // sub
    for t in range(sub):
        q = q_ref[t * rows:(t + 1) * rows, :]
        o = _softmax2_pv(_dot_nt(q, kl), _dot_nt(q, kc), vl, vc)
        o_ref[t * rows:(t + 1) * rows, :] = o.astype(o_ref.dtype)


def _mla_attn(q, k, v, n_batch):
    tq = 2048
    nq = SEQ // tq
    lat_blocks = n_batch * SEQ // CTX
    return pl.pallas_call(
        functools.partial(_mla_attn_kernel, sub=8),
        grid=(n_batch, MLA_HEADS, nq),
        in_specs=[
            pl.BlockSpec((tq, MLA_PAD), lambda b, h, i: (b * nq + i, h)),
            pl.BlockSpec((SEQ, MLA_PAD), lambda b, h, i: (b, h)),
            pl.BlockSpec((CTX, MLA_PAD), lambda b, h, i: (lat_blocks + b, h)),
            pl.BlockSpec((SEQ, MLA_V), lambda b, h, i: (b, h)),
            pl.BlockSpec((CTX, MLA_V), lambda b, h, i: (lat_blocks + b, h)),
        ],
        out_specs=pl.BlockSpec((tq, MLA_V), lambda b, h, i: (b * nq + i, h)),
        out_shape=jax.ShapeDtypeStruct((n_batch * SEQ, HALF), BF16),
        compiler_params=_cparams(("arbitrary", "arbitrary", "arbitrary")),
        name="mla_attn",
    )(q, k, k, v, v)


def _ctx_attn_kernel(q_ref, k_ref, v_ref, o_ref):
    o_ref[...] = _softmax_pv(_dot_nt(q_ref[...], k_ref[...]), v_ref[...]).astype(o_ref.dtype)


def _mla_ctx_attn(q, k, v, n_batch):
    lat_blocks = n_batch * SEQ // CTX
    return pl.pallas_call(
        _ctx_attn_kernel,
        grid=(n_batch, MLA_HEADS),
        in_specs=[
            pl.BlockSpec((CTX, MLA_PAD), lambda b, h: (lat_blocks + b, h)),
            pl.BlockSpec((CTX, MLA_PAD), lambda b, h: (lat_blocks + b, h)),
            pl.BlockSpec((CTX, MLA_V), lambda b, h: (lat_blocks + b, h)),
        ],
        out_specs=pl.BlockSpec((CTX, MLA_V), lambda b, h: (b, h)),
        out_shape=jax.ShapeDtypeStruct((n_batch * CTX, HALF), BF16),
        compiler_params=_cparams(("arbitrary", "arbitrary")),
        name="mla_ctx_attn",
    )(q, k, v)


def _na_key_row0(rg):
    return int(np.clip(NA_QROWS * rg - NA_WIN_ROWS // 2, 0, GRID_H - NA_KROWS))


def _na_band0(n):
    return int(np.clip(NA_QCOLS * n - NA_WIN_COLS // 2, 0, GRID_W - NA_KCOLS))


def _na_offsets(q0, k0, nq, nk, win, extent):
    qpos = q0 + np.arange(nq)[:, None]
    kpos = k0 + np.arange(nk)[None, :]
    start = np.clip(qpos - win // 2, 0, extent - win)
    valid = (kpos >= start) & (kpos < start + win)
    return np.where(valid, kpos - qpos + win - 1, 2 * win - 1)


def _na_block_classes():
    rows = [_na_offsets(NA_QROWS * g, _na_key_row0(g), NA_QROWS, NA_KROWS, NA_WIN_ROWS, GRID_H) for g in range(NA_NRG)]
    cols = [_na_offsets(NA_QCOLS * n, _na_band0(n), NA_QCOLS, NA_KCOLS, NA_WIN_COLS, GRID_W) for n in range(NA_NCB)]

    def classes(pats):
        reps, ids = [], []
        for p in pats:
            for k, r in enumerate(reps):
                if np.array_equal(p, r):
                    ids.append(k)
                    break
            else:
                ids.append(len(reps))
                reps.append(p)
        return np.stack(reps), ids

    return classes(rows), classes(cols)


def _na_bias_table(rpb):
    (r_off, _), (c_off, _) = _na_block_classes()
    ext = jnp.pad(rpb * LOG2E, ((0, 0), (0, 1), (0, 1)), constant_values=NEG_BIG)
    r_sel = jnp.asarray(r_off[..., None] == np.arange(2 * NA_WIN_ROWS), F32)
    c_sel = jnp.asarray(c_off[..., None] == np.arange(2 * NA_WIN_COLS), F32)
    bias = jnp.einsum('gika,hab,njcb->hgnijkc', r_sel, ext, c_sel, precision=lax.Precision.HIGHEST)
    return bias.reshape(NA_HEADS, r_off.shape[0] * c_off.shape[0], NA_QROWS * NA_QCOLS, NA_KROWS * NA_KCOLS)


def _na_kernel(q_ref, k_ref, v_ref, qc_ref, kc_ref, vc_ref, gq_ref, gk_ref, bias_ref, o_ref, oc_ref,
               qn_s, kn_s, sw_s, sc_s, pw_s, pc_s, *, scale):
    gq, gk = gq_ref[...], gk_ref[...]
    qn_s[...] = _rms(q_ref[...], gq) * scale
    kn_s[...] = _rms(k_ref[...], gk)
    kc = _rms(kc_ref[...], gk).astype(BF16)
    vc = vc_ref[...].astype(BF16)
    (_, row_cls), (col_reps, col_cls) = _na_block_classes()
    blocks = [(rg, n) for rg in range(NA_NRG) for n in range(NA_NCB)]

    def q_rows(rg, n):
        return [(NA_QROWS * rg + i) * GRID_W + NA_QCOLS * n for i in range(NA_QROWS)]

    def k_rows(rg, n):
        return [(_na_key_row0(rg) + kr) * GRID_W + _na_band0(n) for kr in range(NA_KROWS)]

    for b, (rg, n) in enumerate(blocks):
        qb = jnp.concatenate([qn_s[pl.ds(t, NA_QCOLS), :] for t in q_rows(rg, n)], axis=0).astype(BF16)
        kb = jnp.concatenate([kn_s[pl.ds(t, NA_KCOLS), :] for t in k_rows(rg, n)], axis=0).astype(BF16)
        sw_s[b] = _dot_nt(qb, kb) + bias_ref[row_cls[rg] * len(col_reps) + col_cls[n]]
        sc_s[b] = _dot_nt(qb, kc)
    denoms = []
    for b in range(len(blocks)):
        s1, s2 = sw_s[b], sc_s[b]
        m = jnp.maximum(jnp.max(s1, axis=-1, keepdims=True), jnp.max(s2, axis=-1, keepdims=True))
        p1, p2 = jnp.exp2(s1 - m), jnp.exp2(s2 - m)
        denoms.append(jnp.sum(p1, axis=-1, keepdims=True) + jnp.sum(p2, axis=-1, keepdims=True))
        pw_s[b] = p1.astype(BF16)
        pc_s[b] = p2.astype(BF16)
    for b, (rg, n) in enumerate(blocks):
        vb = jnp.concatenate([v_ref[pl.ds(t, NA_KCOLS), :] for t in k_rows(rg, n)], axis=0).astype(BF16)
        o = ((_dot(pw_s[b], vb) + _dot(pc_s[b], vc)) / denoms[b]).astype(o_ref.dtype)
        for i, t in enumerate(q_rows(rg, n)):
            o_ref[pl.ds(t, NA_QCOLS), :] = o[i * NA_QCOLS:(i + 1) * NA_QCOLS]
    qc = (_rms(qc_ref[...], gq) * scale).astype(BF16)
    oc_ref[...] = _softmax_pv(_dot_nt(qc, kc), vc).astype(oc_ref.dtype)


def _na_attn(proj, g_q, g_k, bias, n_batch):
    base = OFF_NA // NA_DH
    lat_blocks = n_batch * SEQ // CTX

    def lat(part):
        return pl.BlockSpec((SEQ, NA_DH), lambda h, b: (b, base + part * NA_HEADS + h))

    def cx(part):
        return pl.BlockSpec((CTX, NA_DH), lambda h, b: (lat_blocks + b, base + part * NA_HEADS + h))

    const = lambda h, b: (0, 0)
    nblk, nq, nk = bias.shape[1:]
    return pl.pallas_call(
        functools.partial(_na_kernel, scale=NA_DH ** -0.5 * LOG2E),
        grid=(NA_HEADS, n_batch),
        in_specs=[lat(0), lat(1), lat(2), cx(0), cx(1), cx(2),
                  pl.BlockSpec((1, NA_DH), const), pl.BlockSpec((1, NA_DH), const),
                  pl.BlockSpec((None, nblk, nq, nk), lambda h, b: (h, 0, 0, 0))],
        out_specs=[pl.BlockSpec((SEQ, NA_DH), lambda h, b: (b, h)),
                   pl.BlockSpec((CTX, NA_DH), lambda h, b: (b, h))],
        out_shape=[jax.ShapeDtypeStruct((n_batch * SEQ, HALF), BF16),
                   jax.ShapeDtypeStruct((n_batch * CTX, HALF), BF16)],
        scratch_shapes=[pltpu.VMEM((SEQ, NA_DH), F32), pltpu.VMEM((SEQ, NA_DH), F32),
                        pltpu.VMEM((NA_NRG * NA_NCB, nq, nk), F32), pltpu.VMEM((NA_NRG * NA_NCB, nq, CTX), F32),
                        pltpu.VMEM((NA_NRG * NA_NCB, nq, nk), BF16), pltpu.VMEM((NA_NRG * NA_NCB, nq, CTX), BF16)],
        compiler_params=_cparams(("arbitrary", "arbitrary")),
        name="na_attn",
    )(proj, proj, proj, proj, proj, proj, g_q.reshape(1, NA_DH), g_k.reshape(1, NA_DH), bias)


CONV_TM = 256
CONV_HALO = 8


def _conv_kernel(prev_ref, cur_ref, next_ref, w_ref, b_ref, o_ref, pad_s, *, n_lat_tiles):
    i = pl.program_id(0)
    per = SEQ // CONV_TM
    is_ctx = i >= n_lat_tiles
    at_start = is_ctx | (i % per == 0)
    at_end = is_ctx | (i % per == per - 1)
    pad_s[0:CONV_HALO, :] = jnp.where(at_start, 0.0, prev_ref[...])
    pad_s[CONV_HALO:CONV_HALO + CONV_TM, :] = cur_ref[...]
    pad_s[CONV_HALO + CONV_TM:, :] = jnp.where(at_end, 0.0, next_ref[...])
    acc = jnp.broadcast_to(b_ref[...], cur_ref.shape)
    padded = pad_s[...]
    n_pad = padded.shape[0]
    for k in range(SSD_CONV):
        shift = (SSD_CONV // 2 - k) % n_pad
        tap = padded if shift == 0 else pltpu.roll(padded, shift, 0)
        acc = acc + w_ref[k:k + 1, :] * tap[CONV_HALO:CONV_HALO + CONV_TM, :]
    o_ref[...] = _silu(acc)


def _ssd_conv(proj, conv_w, conv_b, n_lat):
    rows = proj.shape[0]
    tc = 1024
    assert OFF_XBC % tc == 0 and SSD_XBC % tc == 0
    nlt = n_lat // CONV_TM
    cb0 = OFF_XBC // tc
    hb = CONV_TM // CONV_HALO
    last = rows // CONV_HALO - 1
    return pl.pallas_call(
        functools.partial(_conv_kernel, n_lat_tiles=nlt),
        grid=(rows // CONV_TM, SSD_XBC // tc),
        in_specs=[
            pl.BlockSpec((CONV_HALO, tc), lambda i, j: (jnp.maximum(i * hb - 1, 0), cb0 + j)),
            pl.BlockSpec((CONV_TM, tc), lambda i, j: (i, cb0 + j)),
            pl.BlockSpec((CONV_HALO, tc), lambda i, j: (jnp.minimum((i + 1) * hb, last), cb0 + j)),
            pl.BlockSpec((SSD_CONV, tc), lambda i, j: (0, j)),
            pl.BlockSpec((1, tc), lambda i, j: (0, j)),
        ],
        out_specs=pl.BlockSpec((CONV_TM, tc), lambda i, j: (i, j)),
        out_shape=jax.ShapeDtypeStruct((rows, SSD_XBC), F32),
        scratch_shapes=[pltpu.VMEM((CONV_TM + 2 * CONV_HALO, tc), F32)],
        compiler_params=_cparams(("arbitrary", "arbitrary")),
        name="ssd_conv",
    )(proj, proj, proj, conv_w, conv_b.reshape(1, SSD_XBC))


def _ssd_scan_kernel(*refs, reverse, final, streams):
    brow_ref, bcol_ref, acol_ref, e1_ref = refs[:4]
    pos, per = 4, 3
    if final:
        d_ref, g_ref, yprev_ref = refs[4:7]
        pos, per = 7, 4
    y_ref, state_s = refs[pos + per * streams:]

    @pl.when(pl.program_id(1) == 0)
    def _():
        state_s[...] = jnp.zeros_like(state_s)

    for s in range(streams):
        r = refs[pos + per * s:pos + per * (s + 1)]
        if final:
            rest = (yprev_ref.at[s], r[3], d_ref, g_ref, y_ref.at[s], state_s.at[s])
        else:
            rest = (y_ref.at[s], state_s.at[s])
        _ssd_chunk(r[0], r[1], r[2], brow_ref, bcol_ref, acol_ref, e1_ref, *rest, reverse=reverse, final=final)


def _ssd_chunk(xbc_ref, misc_ref, dtt_ref, brow_ref, bcol_ref, acol_ref, e1_ref, *rest, reverse, final):
    if final:
        yprev_ref, z_ref, d_ref, g_ref, y_ref, state_s = rest
    else:
        y_ref, state_s = rest
    L = SSD_CHUNK

    li = lax.broadcasted_iota(jnp.int32, (L, L), 0)
    si = lax.broadcasted_iota(jnp.int32, (L, L), 1)
    lane = lax.broadcasted_iota(jnp.int32, (L, 128), 1)
    if reverse:
        cum_r = jnp.where(li >= si, 1.0, 0.0).astype(BF16)
        causal = li <= si
    else:
        cum_r = jnp.where(li <= si, 1.0, 0.0).astype(BF16)
        causal = li >= si

    dt_exp = _dot_sel_right(_softplus(misc_ref[...] + brow_ref[...]), e1_ref[...])
    dt_row = _softplus(dtt_ref[...] + bcol_ref[...])
    ac_row = _dot_sel_right(dt_row * (-jnp.exp(acol_ref[...])), cum_r)
    ac_col = [jnp.broadcast_to(ac_row[hd:hd + 1, :], (L, L)).T for hd in range(SSD_HEADS)]
    ac_exp = jnp.concatenate([jnp.where(lane < SSD_P, ac_col[2 * p], ac_col[2 * p + 1])
                              for p in range(SSD_HEADS // 2)], axis=1)

    end = 0 if reverse else L - 1
    tot_exp = ac_exp[end:end + 1, :]
    xbc = xbc_ref[...]
    xdt = xbc[:, :HALF] * dt_exp
    xw = (xdt * jnp.exp(tot_exp - ac_exp)).astype(BF16)
    xdt_b = xdt.astype(BF16)
    eac = jnp.exp(ac_exp)
    chunk_decay = jnp.exp(tot_exp)
    gw = SSD_HEADS // SSD_G * SSD_P
    for g in range(SSD_G):
        gs = slice(g * gw, (g + 1) * gw)
        b_t = xbc[:, HALF + g * SSD_N:HALF + (g + 1) * SSD_N].T.astype(BF16)
        c_g = xbc[:, HALF + SSD_G * SSD_N + g * SSD_N:HALF + SSD_G * SSD_N + (g + 1) * SSD_N].astype(BF16)
        cb = _dot(c_g, b_t)
        st = state_s[:, gs]
        y_off = _dot(c_g, st.astype(BF16)) * eac[:, gs]
        state_s[:, gs] = st * chunk_decay[:, gs] + _dot(b_t, xw[:, gs])
        y_pairs = []
        for pair in range(2):
            ha = 4 * g + 2 * pair
            xp = xdt_b[:, ha * SSD_P:ha * SSD_P + 128]
            ys = []
            for hd in (ha, ha + 1):
                seg = ac_col[hd] - ac_row[hd:hd + 1, :]
                m = (cb * jnp.exp(jnp.where(causal, seg, -jnp.inf))).astype(BF16)
                ys.append(_dot(m, xp))
            y_pairs.append(jnp.where(lane < SSD_P, ys[0], ys[1]))
        y_g = jnp.concatenate(y_pairs, axis=1) + y_off
        if final:
            y_all = yprev_ref[:, gs] + y_g + d_ref[:, gs] * xbc[:, gs]
            y_ref[:, gs] = _rms(y_all * _silu(z_ref[:, gs]), g_ref[:, gs]).astype(y_ref.dtype)
        else:
            y_ref[:, gs] = y_g


def _ssd_streams(n_batch):
    streams = 2 if n_batch % 2 == 0 else 1
    return streams, n_batch // streams


def _ssd_scan(xbc_act, proj, dt_t, dt_bias, a_log, direction, n_batch, final=None):
    reverse = direction == 1
    nlc, ncc = SEQ // SSD_CHUNK, CTX // SSD_CHUNK
    lat_chunks = n_batch * nlc
    streams, groups = _ssd_streams(n_batch)

    def pos(j):
        return jnp.where(j < ncc, ncc - 1 - j, 2 * ncc + nlc - 1 - j) if reverse else j

    def blk(s):
        def f(g, j):
            b, p = s * groups + g, pos(j)
            return jnp.where(p < ncc, lat_chunks + b * ncc + p, b * nlc + p - ncc)
        return f

    lane0 = MISC_DT + SSD_HEADS * direction
    b_row = jnp.zeros((1, 128), F32).at[0, lane0:lane0 + SSD_HEADS].set(dt_bias[direction])
    b_col, a_col = dt_bias[direction].reshape(SSD_HEADS, 1), a_log[direction].reshape(SSD_HEADS, 1)
    e1 = jnp.asarray((np.arange(128)[:, None] - lane0) == (np.arange(HALF)[None] // SSD_P), BF16)
    const = lambda g, j: (0, 0)
    y_spec = pl.BlockSpec((None, None, streams, SSD_CHUNK, HALF), lambda g, j: (g, pos(j), 0, 0, 0))
    in_specs = [pl.BlockSpec((1, 128), const), pl.BlockSpec((SSD_HEADS, 1), const),
                pl.BlockSpec((SSD_HEADS, 1), const), pl.BlockSpec((128, HALF), const)]
    args = [b_row, b_col, a_col, e1]
    if final is not None:
        y_prev, d_skip, norm_g = final
        in_specs += [pl.BlockSpec((1, HALF), const), pl.BlockSpec((1, HALF), const), y_spec]
        args += [jnp.repeat(d_skip, SSD_P).reshape(1, HALF), norm_g.reshape(1, HALF), y_prev]
    for s in range(streams):
        f = blk(s)
        in_specs += [pl.BlockSpec((SSD_CHUNK, SSD_XBC), lambda g, j, f=f: (f(g, j), 0)),
                     pl.BlockSpec((SSD_CHUNK, 128), lambda g, j, f=f: (f(g, j), OFF_MISC // 128)),
                     pl.BlockSpec((SSD_HEADS, SSD_CHUNK), lambda g, j, f=f: (direction, f(g, j)))]
        args += [xbc_act, proj, dt_t]
        if final is not None:
            in_specs.append(pl.BlockSpec((SSD_CHUNK, HALF), lambda g, j, f=f: (f(g, j), OFF_Z // HALF)))
            args.append(proj)
    return pl.pallas_call(
        functools.partial(_ssd_scan_kernel, reverse=reverse, final=final is not None, streams=streams),
        grid=(groups, ncc + nlc),
        in_specs=in_specs,
        out_specs=y_spec,
        out_shape=jax.ShapeDtypeStruct((groups, ncc + nlc, streams, SSD_CHUNK, HALF), F32 if final is None else BF16),
        scratch_shapes=[pltpu.VMEM((streams, SSD_N, HALF), F32)],
        compiler_params=_cparams(("arbitrary", "arbitrary")),
        name="ssd_scan_bwd" if reverse else "ssd_scan_fwd",
    )(*args)


def _w_in_pieces():
    o = np.cumsum((0, MLA_HEADS * MLA_QK, MLA_RANK, MLA_ROPE, HALF, SSD_XBC, 2 * SSD_HEADS, 3 * HALF, 3 * D))
    seg = lambda k: (int(o[k]), int(o[k + 1] - o[k]))
    q_nope = [(hd * MLA_QK, MLA_NOPE) for hd in range(MLA_HEADS)]
    q_rope = [(hd * MLA_QK + MLA_NOPE, MLA_ROPE) for hd in range(MLA_HEADS)]
    return [seg(7), seg(6), seg(4), *q_nope, seg(3), *q_rope, seg(1), seg(2), seg(5)]


def _permute_kernel(wt_ref, o_ref, rows_s):
    dst = 0
    for a, n in _w_in_pieces():
        rows_s[dst:dst + n, :] = wt_ref[a:a + n, :]
        dst += n
    rows_s[dst:, :] = jnp.zeros((rows_s.shape[0] - dst, rows_s.shape[1]), rows_s.dtype)
    for j in range(N_PROJ // 128):
        cols = slice(j * 128, (j + 1) * 128)
        if j * 128 < rows_s.shape[0]:
            o_ref[:, cols] = rows_s[cols, :].T.astype(o_ref.dtype)
        else:
            o_ref[:, cols] = jnp.zeros((o_ref.shape[0], 128), o_ref.dtype)


def _permute_w_in(w):
    depth, _, n_in = w.shape
    tk = 128
    used = OFF_MISC + 128
    assert sum(n for _, n in _w_in_pieces()) == n_in <= used
    return pl.pallas_call(
        _permute_kernel,
        grid=(depth, D // tk),
        in_specs=[pl.BlockSpec((None, n_in, tk), lambda l, i: (l, 0, i))],
        out_specs=pl.BlockSpec((None, tk, N_PROJ), lambda l, i: (l, i, 0)),
        out_shape=jax.ShapeDtypeStruct((depth, D, N_PROJ), BF16),
        scratch_shapes=[pltpu.VMEM((used, tk), F32)],
        compiler_params=_cparams(("arbitrary", "arbitrary")),
        name="permute_w_in",
    )(jnp.swapaxes(w, 1, 2))


def _rope_tables():
    t = jnp.arange(SEQ)
    pos = jnp.stack([t // GRID_W, t % GRID_W], axis=-1).astype(F32)
    n_freq = MLA_ROPE // 4
    inv_freq = ROPE_THETA ** (-jnp.arange(n_freq, dtype=F32) / n_freq)
    ang = pos[:, :, None] * inv_freq
    cos, sin = jnp.cos(ang), jnp.sin(ang)
    c64 = jnp.concatenate([cos[:, 0], cos[:, 0], cos[:, 1], cos[:, 1]], axis=-1)
    s64 = jnp.concatenate([-sin[:, 0], sin[:, 0], -sin[:, 1], sin[:, 1]], axis=-1)
    ident = 256
    cos_t = jnp.concatenate([jnp.tile(c64, (1, 2)), jnp.ones((ident, 128), F32)], axis=0)
    sin_t = jnp.concatenate([jnp.tile(s64, (1, 2)), jnp.zeros((ident, 128), F32)], axis=0)
    return cos_t, sin_t


def kernel(x, c, ctx, c_ctx, w_ada, b_ada, norm_g, ffn1_w_gate, ffn1_w_up, ffn1_w_down, ffn2_w_gate, ffn2_w_up, ffn2_w_down, w_in, mla_kv_norm_g, mla_w_uk, mla_w_uv, mla_q_norm_g, mla_k_norm_g, ssd_conv_w, ssd_conv_b, ssd_a_log, ssd_dt_bias, ssd_d, ssd_norm_g, na_q_norm_g, na_k_norm_g, na_rpb, w_branch, w_out):
    nb = x.shape[0]
    assert x.shape[1:] == (SEQ, D) and ctx.shape[1:] == (CTX, D) and nb < 16
    n_lat, n_ctx = nb * SEQ, nb * CTX
    n_tok = n_lat + n_ctx
    depth = w_ada.shape[0]

    cond =jnp.concatenate([c, c_ctx[None], jnp.zeros((16 - nb - 1, D), F32)], axis=0)
    m_all = _adaln(cond, w_ada, b_ada)
    cos_t, sin_t = _rope_tables()

    bf = lambda w: w.astype(BF16)
    ffn1 = (bf(ffn1_w_gate), bf(ffn1_w_up), bf(ffn1_w_down))
    ffn2 = (bf(ffn2_w_gate), bf(ffn2_w_up), bf(ffn2_w_down))
    w_in_p, w_uk, w_uv, w_br, w_o = _permute_w_in(w_in), bf(mla_w_uk), bf(mla_w_uv), bf(w_branch), bf(w_out)

    m4s = [m_all[l].reshape(16, N_MOD, 1, D) for l in range(depth)]
    h, u = _normmod(x.reshape(n_lat, D), ctx.reshape(n_ctx, D), norm_g[0, 0], m4s[0], base=0, n_batch=nb)
    for l in range(depth):
        need_ctx = l < depth - 1
        rows_out = n_tok if need_ctx else n_lat
        m4 = m4s[l]
        mod = dict(m4=m4, n_lat=n_lat, n_batch=nb)

        h, u = _ffn(u, h, n_tok, l, *ffn1, base=0, nxt=(norm_g[l, 1], m4, 3), **mod)
        proj = _in_proj(u, w_in_p, l)

        q, k, v = _mla_prep(proj, cos_t, sin_t, l, w_uk, w_uv, mla_kv_norm_g[l],
                            mla_q_norm_g[l], mla_k_norm_g[l], n_lat)
        mla = _mla_attn(q, k, v, nb)

        na, na_c = _na_attn(proj, na_q_norm_g[l], na_k_norm_g[l], _na_bias_table(na_rpb[l]), nb)

        xbc_act = _ssd_conv(proj, ssd_conv_w[l], ssd_conv_b[l], n_lat)
        dt_t = proj[:, OFF_MISC + MISC_DT:OFF_MISC + MISC_DT + 2 * SSD_HEADS].T
        y_f = _ssd_scan(xbc_act, proj, dt_t, ssd_dt_bias[l], ssd_a_log[l], 0, nb)
        ssd = _ssd_scan(xbc_act, proj, dt_t, ssd_dt_bias[l], ssd_a_log[l], 1, nb,
                        final=(y_f, ssd_d[l], ssd_norm_g[l]))

        mla_c = _mla_ctx_attn(q, k, v, nb) if need_ctx else na_c
        h, u = _mix(((mla, mla_c), ssd, (na, na_c)), proj, l, w_br, w_o, h, rows_out, n_g=norm_g[l, 2], **mod)

        if need_ctx:
            h, u = _ffn(u, h, rows_out, l, *ffn2, base=6, nxt=(norm_g[l + 1, 0], m4s[l + 1], 0), **mod)
        else:
            h = _ffn(u, h, rows_out, l, *ffn2, base=6, **mod)

    return h[:n_lat].reshape(nb, SEQ, D)
```
